```python
import math
import jax, jax.numpy as jnp
from jax import lax
import numpy as np

D_MODEL = 1024
BATCH = 16
SEQ = 256
DEPTH = 4
DEC_BATCH = 2
DEC_SEQ = 1024
PAST_LEN = 256

GRID_W = 64
BLOCK = 128
GROUP_W = D_MODEL // 4
A_HEADS = 4
A_HD = GROUP_W // A_HEADS
B_HEADS = 4
B_KV = 2
B_HD = GROUP_W // B_HEADS
B_GROUP = B_HEADS // B_KV
WINDOW = 128
C_HEADS = 4
C_HD = GROUP_W // C_HEADS
C_SUB = C_HD // 2
D_GROUPS = 4
D_GW = GROUP_W // D_GROUPS
POOL_SIZES = (2, 4, 8, 16)
PROJ_WIDTHS = (GROUP_W, GROUP_W,
               B_HEADS * B_HD, B_KV * B_HD, B_KV * B_HD,
               GROUP_W, GROUP_W, GROUP_W,
               GROUP_W)
P_COLS = sum(PROJ_WIDTHS)
PROJ_SPLITS = [int(s) for s in np.cumsum(PROJ_WIDTHS)[:-1]]
PK_HEADS = 8
N_KEYS = 128
N_EXPERTS = N_KEYS * N_KEYS
D_KEY = 128
PK_TOPK = 16
ROPE_BASE = 10000.0
ALPHA = (2 * DEPTH) ** 0.25
BETA = (8 * DEPTH) ** -0.25
LN_EPS = 1e-5

kernel_name = 'hybrid_flow_prefix_step'


def layer_norm(x):
    xf = x.astype(jnp.float32)
    mu = jnp.mean(xf, axis=-1, keepdims=True)
    var = jnp.mean(jnp.square(xf - mu), axis=-1, keepdims=True)
    return ((xf - mu) * lax.rsqrt(var + LN_EPS)).astype(x.dtype)


def rms_norm(x, g):
    xf = x.astype(jnp.float32)
    return (xf * lax.rsqrt(jnp.mean(xf * xf, axis=-1, keepdims=True) + LN_EPS)).astype(x.dtype) * g


def axial_rope(n_tok, dim):
    rows = n_tok // GRID_W
    row = jnp.repeat(jnp.arange(rows, dtype=jnp.float32), GRID_W)
    col = jnp.tile(jnp.arange(GRID_W, dtype=jnp.float32), rows)
    nf = dim // 4
    inv = ROPE_BASE ** (-jnp.arange(nf, dtype=jnp.float32) / nf)
    ar = row[:, None] * inv
    ac = col[:, None] * inv
    ang = jnp.concatenate([ar, ar, ac, ac], axis=-1)
    return jnp.cos(ang), jnp.sin(ang)


def apply_rope(x, rope):
    cos, sin = rope
    shp = (cos.shape[0],) + (1,) * (x.ndim - 3) + (cos.shape[1],)
    cos = cos.reshape(shp)
    sin = sin.reshape(shp)
    x1, x2, x3, x4 = jnp.split(x, 4, axis=-1)
    rot = jnp.concatenate([-x2, x1, -x4, x3], axis=-1)
    return (x * cos + rot * sin).astype(x.dtype)


def sweep_query_blocks(fn, q):
    b, t = q.shape[:2]
    qb = jnp.moveaxis(q.reshape((b, t // BLOCK, BLOCK) + q.shape[2:]), 1, 0)
    out = jnp.moveaxis(lax.map(fn, qb), 0, 1)
    return out.reshape((b, t) + out.shape[3:])


def softmax_with_sink(logits, sink):
    full = jnp.concatenate([logits, jnp.broadcast_to(sink, logits.shape[:-1] + (1,))], axis=-1)
    return jax.nn.softmax(full, axis=-1)[..., :-1]


def chunk_gating(u, v, w_s, b_s):
    b, s, h, d = v.shape
    vc = v.reshape(b, s // BLOCK, BLOCK, h, d)
    mixed = jnp.einsum('hpq,bnqhd->bnphd', w_s, vc) + b_s.T[None, None, :, :, None]
    return u * mixed.reshape(b, s, h, d)


def window_attn_context(q, k, v, sink):
    scale = B_HD ** -0.5
    sink = sink.astype(jnp.float32).reshape(B_KV, B_GROUP, 1, 1)

    def blk(qb):
        qg = qb.reshape(qb.shape[0], BLOCK, B_KV, B_GROUP, B_HD)
        s = jnp.einsum('bqkgd,bskd->bkgqs', qg, k).astype(jnp.float32) * scale
        p = softmax_with_sink(s, sink).astype(v.dtype)
        return jnp.einsum('bkgqs,bskd->bqkgd', p, v).reshape(qb.shape)

    return sweep_query_blocks(blk, q)


def window_attn_latent(q, k, v, k_ctx, v_ctx, sink):
    b, t = q.shape[:2]
    nb = t // BLOCK
    scale = B_HD ** -0.5
    qb = q.reshape(b, nb, BLOCK, B_KV, B_GROUP, B_HD)

    def band(x):
        xb = x.reshape(b, nb, BLOCK, B_KV, B_HD)
        xp = jnp.pad(xb, ((0, 0), (1, 1), (0, 0), (0, 0), (0, 0)))
        return jnp.concatenate([xp[:, :-2], xp[:, 1:-1], xp[:, 2:]], axis=2)

    kb, vb = band(k), band(v)
    qpos = jnp.arange(nb)[:, None] * BLOCK + jnp.arange(BLOCK)[None, :]
    kpos = (jnp.arange(nb)[:, None] - 1) * BLOCK + jnp.arange(3 * BLOCK)[None, :]
    rel = kpos[:, None, :] - qpos[:, :, None]
    valid = (jnp.abs(rel) <= WINDOW) & (kpos[:, None, :] >= 0) & (kpos[:, None, :] < t)
    s_loc = jnp.einsum('bnqkgd,bnskd->bnkgqs', qb, kb).astype(jnp.float32) * scale
    s_loc = jnp.where(valid[None, :, None, None], s_loc, -jnp.inf)
    s_ctx = jnp.einsum('bnqkgd,bskd->bnkgqs', qb, k_ctx).astype(jnp.float32) * scale
    p = softmax_with_sink(jnp.concatenate([s_loc, s_ctx], axis=-1),
                          sink.astype(jnp.float32).reshape(B_KV, B_GROUP, 1, 1)).astype(v.dtype)
    o = (jnp.einsum('bnkgqs,bnskd->bnqkgd', p[..., :3 * BLOCK], vb)
         + jnp.einsum('bnkgqs,bskd->bnqkgd', p[..., 3 * BLOCK:], v_ctx))
    return o.reshape(b, t, B_HEADS, B_HD)


def diff_attn(q, k, v, lam):
    scale = C_SUB ** -0.5

    def blk(qb):
        s = jnp.einsum('bqhjd,bshjd->bhjqs', qb, k).astype(jnp.float32) * scale
        p = jax.nn.softmax(s, axis=-1)
        w = p[:, :, 0] - lam * p[:, :, 1]
        return jnp.einsum('bhqs,bshd->bqhd', w.astype(v.dtype), v)

    return sweep_query_blocks(blk, q)


def multiscale_pool(z, w_pool, pool_scale):
    b, s, _ = z.shape
    zf = z.astype(jnp.float32)
    cs = jnp.concatenate([jnp.zeros((b, 1, GROUP_W), jnp.float32), jnp.cumsum(zf, axis=1)], axis=1)
    t = jnp.arange(s)
    outs = []
    for g, w in enumerate(POOL_SIZES):
        lo = jnp.clip(t - w // 2, 0, s)
        hi = jnp.clip(t + w // 2, 0, s)
        c0, c1 = g * D_GW, (g + 1) * D_GW
        mean = (cs[:, hi, c0:c1] - cs[:, lo, c0:c1]) / (hi - lo).astype(jnp.float32)[None, :, None]
        outs.append(mean - zf[:, :, c0:c1])
    pooled = jnp.stack(outs, axis=2)
    y = jnp.einsum('bsgc,gcd->bsgd', pooled, w_pool).reshape(b, s, GROUP_W)
    return (y * pool_scale).astype(z.dtype)


def peer(x, w_q, sub_keys, u_tab, v_tab):
    b, s, d = x.shape
    xt = x.reshape(-1, d)
    nt = xt.shape[0]
    q = (xt @ w_q).reshape(nt, PK_HEADS, 2, D_KEY // 2)
    sc = jnp.einsum('thjd,jnd->thjn', q, sub_keys).astype(jnp.float32)
    sv, si = lax.top_k(sc, PK_TOPK)
    cand = sv[:, :, 0, :, None] + sv[:, :, 1, None, :]
    cv, ci = lax.top_k(cand.reshape(nt, PK_HEADS, PK_TOPK * PK_TOPK), PK_TOPK)
    i1 = jnp.take_along_axis(si[:, :, 0], ci // PK_TOPK, axis=-1)
    i2 = jnp.take_along_axis(si[:, :, 1], ci % PK_TOPK, axis=-1)
    idx = i1 * N_KEYS + i2
    gate = jax.nn.softmax(cv, axis=-1)

    def blk(args):
        xb, ib, gb = args
        ue = jnp.take(u_tab, ib, axis=0)
        act = jax.nn.gelu(jnp.einsum('thkd,td->thk', ue, xb).astype(jnp.float32))
        ve = jnp.take(v_tab, ib, axis=0)
        return jnp.einsum('thk,thkd->td', (gb * act).astype(xb.dtype), ve)

    nblk = nt // BLOCK
    out = lax.map(blk, (xt.reshape(nblk, BLOCK, d),
                        idx.reshape(nblk, BLOCK, PK_HEADS, PK_TOPK),
                        gate.reshape(nblk, BLOCK, PK_HEADS, PK_TOPK)))
    return out.reshape(b, s, d)


def trunk_layer(x, mod, w_in, w_out, chunk_w, chunk_b, sink, lam, subln_g, lam_init,
                pool_w, pool_scale, ln_g, ln_b, peer_wq, peer_keys, peer_u, peer_v, latent=None):
    b, s, _ = x.shape
    sh1, sc1, g1, sh2, sc2, g2 = jnp.split(mod, 6, axis=-1)
    h = layer_norm(x) * (1 + sc1) + sh1
    a_u, a_v, bq, bk, bv, cq, ck, cv, dz = jnp.split(h @ w_in, PROJ_SPLITS, axis=-1)
    u = jax.nn.gelu(a_u).reshape(b, s, A_HEADS, A_HD)
    v = jax.nn.gelu(a_v).reshape(b, s, A_HEADS, A_HD)
    o_a = chunk_gating(u, v, chunk_w, chunk_b).reshape(b, s, GROUP_W)
    bq = bq.reshape(b, s, B_HEADS, B_HD)
    bk = bk.reshape(b, s, B_KV, B_HD)
    bv = bv.reshape(b, s, B_KV, B_HD)
    cq = cq.reshape(b, s, C_HEADS, 2, C_SUB)
    ck = ck.reshape(b, s, C_HEADS, 2, C_SUB)
    cv = cv.reshape(b, s, C_HEADS, C_HD)
    if latent is None:
        o_b = window_attn_context(bq, bk, bv, sink)
        o_c = diff_attn(cq, ck, cv, lam)
    else:
        rope_b, rope_c, kb_ctx, vb_ctx, kc_ctx, vc_ctx = latent
        o_b = window_attn_latent(apply_rope(bq, rope_b), apply_rope(bk, rope_b), bv, kb_ctx, vb_ctx, sink)
        o_c = diff_attn(apply_rope(cq, rope_c),
                        jnp.concatenate([kc_ctx, apply_rope(ck, rope_c)], axis=1),
                        jnp.concatenate([vc_ctx, cv], axis=1), lam)
    o_c = rms_norm(o_c, subln_g) * (1.0 - lam_init)
    o_d = multiscale_pool(dz, pool_w, pool_scale)
    o = jnp.concatenate([o_a, o_b.reshape(b, s, GROUP_W), o_c.reshape(b, s, GROUP_W), o_d], axis=-1) @ w_out
    x = layer_norm(ALPHA * x + g1 * o) * ln_g[0] + ln_b[0]
    h = layer_norm(x) * (1 + sc2) + sh2
    f = peer(h, peer_wq, peer_keys, peer_u, peer_v)
    x = layer_norm(ALPHA * x + g2 * f) * ln_g[1] + ln_b[1]
    return x, (bk, bv, ck, cv)


def setup_inputs(seed: int = 0) -> dict:
    key = jax.random.key(seed)
    ks = jax.random.split(key, 26)

    def nrm(k, shape, s):
        return jax.random.normal(k, shape, jnp.float32) * s

    return {
        'x_prompt': nrm(ks[0], (BATCH, SEQ, D_MODEL), 1.0),
        'x_sample': nrm(ks[1], (DEC_BATCH, DEC_SEQ, D_MODEL), 1.0),
        'c': nrm(ks[2], (DEC_BATCH, D_MODEL), 1.0),
        'cache_win_k': nrm(ks[3], (DEC_BATCH, DEPTH, PAST_LEN, B_KV, B_HD), 1.0),
        'cache_win_v': nrm(ks[4], (DEC_BATCH, DEPTH, PAST_LEN, B_KV, B_HD), 1.0),
        'cache_diff_k': nrm(ks[5], (DEC_BATCH, DEPTH, PAST_LEN, C_HEADS, 2, C_SUB), 1.0),
        'cache_diff_v': nrm(ks[6], (DEC_BATCH, DEPTH, PAST_LEN, C_HEADS, C_HD), 1.0),
        'c_ctx': nrm(ks[7], (D_MODEL,), 1.0),
        'w_mod': nrm(ks[8], (DEPTH, D_MODEL, 6 * D_MODEL), 0.5 * D_MODEL ** -0.5),
        'b_mod': nrm(ks[9], (DEPTH, 6 * D_MODEL), 0.01),
        'w_in': nrm(ks[10], (DEPTH, D_MODEL, P_COLS), D_MODEL ** -0.5),
        'w_out': nrm(ks[11], (DEPTH, D_MODEL, D_MODEL), BETA * D_MODEL ** -0.5),
        'chunk_w': nrm(ks[12], (DEPTH, A_HEADS, BLOCK, BLOCK), BLOCK ** -0.5),
        'chunk_b': 1.0 + nrm(ks[13], (DEPTH, A_HEADS, BLOCK), 0.01),
        'win_sink': nrm(ks[14], (DEPTH, B_HEADS), 0.5),
        'diff_lam_q': nrm(ks[15], (DEPTH, 2, C_SUB), 0.1),
        'diff_lam_k': nrm(ks[16], (DEPTH, 2, C_SUB), 0.1),
        'diff_subln_g': 1.0 + nrm(ks[17], (DEPTH, C_HD), 0.02),
        'pool_w': nrm(ks[18], (DEPTH, D_GROUPS, D_GW, D_GW), D_GW ** -0.5),
        'pool_scale': 1.0 + nrm(ks[19], (DEPTH, GROUP_W), 0.02),
        'ln_g': 1.0 + nrm(ks[20], (DEPTH, 2, D_MODEL), 0.02),
        'ln_b': nrm(ks[21], (DEPTH, 2, D_MODEL), 0.01),
        'peer_wq': nrm(ks[22], (DEPTH, D_MODEL, PK_HEADS * D_KEY), D_MODEL ** -0.5),
        'peer_keys': nrm(ks[23], (DEPTH, 2, N_KEYS, D_KEY // 2), (D_KEY // 2) ** -0.5),
        'peer_u': nrm(ks[24], (DEPTH, N_EXPERTS, D_MODEL), D_MODEL ** -0.5),
        'peer_v': nrm(ks[25], (DEPTH, N_EXPERTS, D_MODEL), BETA),
    }


def reference(x_prompt, x_sample, c, cache_win_k, cache_win_v, cache_diff_k, cache_diff_v, c_ctx,
              w_mod, b_mod, w_in, w_out, chunk_w, chunk_b, win_sink, diff_lam_q, diff_lam_k,
              diff_subln_g, pool_w, pool_scale, ln_g, ln_b, peer_wq, peer_keys, peer_u, peer_v):
    n_lat = x_sample.shape[1]
    rope_b = axial_rope(n_lat, B_HD)
    rope_c = axial_rope(n_lat, C_SUB)
    y_p, y_s = x_prompt, x_sample
    kbs, vbs, kcs, vcs = [], [], [], []
    for l in range(DEPTH):
        lam_init = 0.8 - 0.6 * math.exp(-0.3 * l)
        lam = (jnp.exp(jnp.sum(diff_lam_q[l, 0] * diff_lam_k[l, 0]).astype(jnp.float32))
               - jnp.exp(jnp.sum(diff_lam_q[l, 1] * diff_lam_k[l, 1]).astype(jnp.float32)) + lam_init)
        shared = (w_in[l], w_out[l], chunk_w[l], chunk_b[l], win_sink[l], lam, diff_subln_g[l], lam_init,
                  pool_w[l], pool_scale[l], ln_g[l], ln_b[l], peer_wq[l], peer_keys[l], peer_u[l], peer_v[l])
        mod_ctx = (jax.nn.silu(c_ctx) @ w_mod[l] + b_mod[l])[None, None, :]
        y_p, (kb, vb, kc, vc) = trunk_layer(y_p, mod_ctx, *shared)
        kbs.append(kb)
        vbs.append(vb)
        kcs.append(kc)
        vcs.append(vc)
        mod_lat = (jax.nn.silu(c) @ w_mod[l] + b_mod[l])[:, None, :]
        y_s, _ = trunk_layer(y_s, mod_lat, *shared,
                             latent=(rope_b, rope_c, cache_win_k[:, l], cache_win_v[:, l],
                                     cache_diff_k[:, l], cache_diff_v[:, l]))
    new_win_k = jnp.stack(kbs, axis=1)
    new_win_v = jnp.stack(vbs, axis=1)
    new_diff_k = jnp.stack(kcs, axis=1)
    new_diff_v = jnp.stack(vcs, axis=1)
    return (y_p, y_sample_out := y_s, new_win_k, new_win_v, new_diff_k, new_diff_v)
```

```python
import functools
import math

import jax
import jax.numpy as jnp
import numpy as np
from jax import lax
from jax.experimental import pallas as pl
from jax.experimental.pallas import tpu as pltpu

F32 = jnp.float32
BF16 = jnp.bfloat16
HIGHEST = lax.Precision.HIGHEST

D_MODEL = 1024
DEPTH = 4
N_CTX_SEQ = 16
CTX_LEN = 256
N_LAT_SEQ = 2
LAT_LEN = 1024
N_CTX_TOK = N_CTX_SEQ * CTX_LEN
N_LAT_TOK = N_LAT_SEQ * LAT_LEN
N_TOK = N_CTX_TOK + N_LAT_TOK
GRID_W = 64
BLOCK = 128
GROUP_W = 256
HEAD_D = 64
C_SUB = 32
WINDOW = 128
POOL_SIZES = (2, 4, 8, 16)
P_COLS = 2048
COL_AU, COL_AV, COL_BQ, COL_BK, COL_BV, COL_CQ, COL_CK, COL_CV, COL_DZ = (
    0, 256, 512, 768, 896, 1024, 1280, 1536, 1792)
PK_HEADS = 8
N_KEYS = 128
N_EXPERTS = N_KEYS * N_KEYS
PK_TOPK = 16
N_PICKS = PK_HEADS * PK_TOPK
ROPE_BASE = 10000.0
ALPHA = (2 * DEPTH) ** 0.25
LN_EPS = 1e-5
NEG = -1e30

ROW_BLOCK = 512
N_ROW_BLOCKS = N_TOK // ROW_BLOCK
N_CTX_ROW_BLOCKS = N_CTX_TOK // ROW_BLOCK
ROW_BLOCKS_PER_LAT_SEQ = LAT_LEN // ROW_BLOCK
ROUTE_BLOCK = 256
PEER_BLOCK = 64
SUBLANES = 8
LANES = 128
HALF_EXPERTS = N_EXPERTS // 2
VMEM_LIMIT = 56 * 1024 * 1024


def _mm(a, b):
    return jnp.dot(a.astype(BF16), b.astype(BF16), preferred_element_type=F32)


def _mm_nt(a, b):
    return lax.dot_general(a.astype(BF16), b.astype(BF16), (((1,), (1,)), ((), ())),
                           preferred_element_type=F32)


def _gelu(x):
    return x * (0.5 * (1.0 + jnp.tanh(0.7978845608028654 * (x + 0.044715 * (x * x * x)))))


def _ln(x):
    mu = jnp.mean(x, axis=-1, keepdims=True)
    xc = x - mu
    var = jnp.mean(xc * xc, axis=-1, keepdims=True)
    return xc * lax.rsqrt(var + LN_EPS)


def _mod_group(i):
    return jnp.where(i < N_CTX_ROW_BLOCKS, 0, 1 + (i - N_CTX_ROW_BLOCKS) // ROW_BLOCKS_PER_LAT_SEQ)


def _mod_kernel(c_ref, w_ref, b_ref, o_ref):
    c = c_ref[...]
    s = c * jax.nn.sigmoid(c)
    o_ref[...] = jnp.dot(s, w_ref[...], precision=HIGHEST, preferred_element_type=F32) + b_ref[...]


def _modulation(cvec, w_mod, b_mod):
    nj = 4
    cols = 6 * D_MODEL // nj
    return pl.pallas_call(
        _mod_kernel,
        grid=(DEPTH, nj),
        in_specs=[pl.BlockSpec((8, D_MODEL), lambda l, j: (0, 0)),
                  pl.BlockSpec((None, D_MODEL, cols), lambda l, j: (l, 0, j)),
                  pl.BlockSpec((None, 1, cols), lambda l, j: (l, 0, j))],
        out_specs=pl.BlockSpec((None, 8, cols), lambda l, j: (l, 0, j)),
        out_shape=jax.ShapeDtypeStruct((DEPTH, 8, 6 * D_MODEL), F32),
        compiler_params=pltpu.CompilerParams(vmem_limit_bytes=VMEM_LIMIT),
        name="modulation",
    )(cvec, w_mod, b_mod.reshape(DEPTH, 1, 6 * D_MODEL))


def _in_proj_kernel(x_ref, mod_ref, w_ref, o_ref):
    m = mod_ref[...]
    h = _ln(x_ref[...]) * (1.0 + m[:, D_MODEL:2 * D_MODEL]) + m[:, 0:D_MODEL]
    o_ref[...] = jnp.dot(h.astype(BF16), w_ref[...], preferred_element_type=F32)


def _in_proj(x, mods, w_in_bf16, layer):
    return pl.pallas_call(
        _in_proj_kernel,
        grid=(N_ROW_BLOCKS,),
        in_specs=[pl.BlockSpec((ROW_BLOCK, D_MODEL), lambda i: (i, 0)),
                  pl.BlockSpec((None, None, 1, 6 * D_MODEL), lambda i: (layer, _mod_group(i), 0, 0)),
                  pl.BlockSpec((None, D_MODEL, P_COLS), lambda i: (layer, 0, 0))],
        out_specs=pl.BlockSpec((ROW_BLOCK, P_COLS), lambda i: (i, 0)),
        out_shape=jax.ShapeDtypeStruct((N_TOK, P_COLS), F32),
        compiler_params=pltpu.CompilerParams(vmem_limit_bytes=VMEM_LIMIT),
        name="in_proj",
    )(x, mods, w_in_bf16)


def _chunk_gating(u_raw, v_raw, cw_ref, cb_full):
    u = _gelu(u_raw)
    v = _gelu(v_raw).astype(BF16)
    head = lax.broadcasted_iota(jnp.int32, (BLOCK, GROUP_W), 1) // HEAD_D
    mixed = cb_full
    for h in range(4):
        mixed = mixed + jnp.where(head == h, jnp.dot(cw_ref[h], v, preferred_element_type=F32), 0.0)
    return u * mixed


def _pool(z_rows, z_win, row0, key0, seq_len, wbd_ref, pool_scale):
    r, k = z_rows.shape[0], z_win.shape[0]
    t = row0 + lax.broadcasted_iota(jnp.int32, (r, k), 0)
    pos = key0 + lax.broadcasted_iota(jnp.int32, (r, k), 1)
    t1 = row0 + lax.broadcasted_iota(jnp.int32, (r, 1), 0)
    z_hi = z_win.astype(BF16)
    z_lo = (z_win - z_hi.astype(F32)).astype(BF16)
    group = lax.broadcasted_iota(jnp.int32, (r, GROUP_W), 1) // HEAD_D
    mean = jnp.zeros((r, GROUP_W), F32)
    for g, w in enumerate(POOL_SIZES):
        lo = jnp.maximum(t - w // 2, 0)
        hi = jnp.minimum(t + w // 2, seq_len)
        band = jnp.where((pos >= lo) & (pos < hi), 1.0, 0.0).astype(BF16)
        tot = (jnp.dot(band, z_hi, preferred_element_type=F32)
               + jnp.dot(band, z_lo, preferred_element_type=F32))
        cnt = (jnp.minimum(t1 + w // 2, seq_len) - jnp.maximum(t1 - w // 2, 0)).astype(F32)
        mean = mean + jnp.where(group == g, tot / cnt, 0.0)
    pooled = mean - z_rows
    return jnp.dot(pooled.astype(BF16), wbd_ref[...], preferred_element_type=F32) * pool_scale


def _diff_lambda(lq_ref, lk_ref, lam_init):
    lq = lq_ref[...]
    lk = lk_ref[...]
    s0 = jnp.sum(lq[0:1] * lk[0:1], axis=-1, keepdims=True)
    s1 = jnp.sum(lq[1:2] * lk[1:2], axis=-1, keepdims=True)
    return jnp.exp(s0) - jnp.exp(s1) + lam_init


def _sub_norm(o, g, lam_init):
    ms = jnp.mean(o * o, axis=-1, keepdims=True)
    return o * lax.rsqrt(ms + LN_EPS) * g * (1.0 - lam_init)


def _ctx_mix_kernel(lam_init, p_ref, cw_ref, cb_ref, sink_ref, lq_ref, lk_ref, sg_ref, wbd_ref,
                    ps_ref, o_ref):
    s_len = CTX_LEN
    for n in range(s_len // BLOCK):
        rows = slice(n * BLOCK, (n + 1) * BLOCK)
        o_ref[rows, 0:GROUP_W] = _chunk_gating(p_ref[rows, COL_AU:COL_AU + GROUP_W],
                                               p_ref[rows, COL_AV:COL_AV + GROUP_W], cw_ref, cb_ref[...])
    for kv in range(2):
        kk = p_ref[:, COL_BK + kv * HEAD_D:COL_BK + (kv + 1) * HEAD_D].astype(BF16)
        vv = p_ref[:, COL_BV + kv * HEAD_D:COL_BV + (kv + 1) * HEAD_D].astype(BF16)
        for g in range(2):
            h = kv * 2 + g
            q = p_ref[:, COL_BQ + h * HEAD_D:COL_BQ + (h + 1) * HEAD_D] * (HEAD_D ** -0.5)
            s = _mm_nt(q, kk)
            sk = sink_ref[h]
            m = jnp.maximum(jnp.max(s, axis=-1, keepdims=True), sk)
            e = jnp.exp(s - m)
            den = jnp.sum(e, axis=-1, keepdims=True) + jnp.exp(sk - m)
            o_ref[:, GROUP_W + h * HEAD_D:GROUP_W + (h + 1) * HEAD_D] = _mm(e, vv) / den
    lam = _diff_lambda(lq_ref, lk_ref, lam_init)
    scale = C_SUB ** -0.5
    for h in range(4):
        vv = p_ref[:, COL_CV + h * HEAD_D:COL_CV + (h + 1) * HEAD_D].astype(BF16)
        parts = []
        for j in range(2):
            c0 = h * HEAD_D + j * C_SUB
            s = _mm_nt(p_ref[:, COL_CQ + c0:COL_CQ + c0 + C_SUB],
                       p_ref[:, COL_CK + c0:COL_CK + c0 + C_SUB]) * scale
            e = jnp.exp(s - jnp.max(s, axis=-1, keepdims=True))
            parts.append(_mm(e, vv) / jnp.sum(e, axis=-1, keepdims=True))
        o = parts[0] - lam * parts[1]
        o_ref[:, 2 * GROUP_W + h * HEAD_D:2 * GROUP_W + (h + 1) * HEAD_D] = _sub_norm(o, sg_ref[...], lam_init)
    z = p_ref[:, COL_DZ:COL_DZ + GROUP_W]
    o_ref[:, 3 * GROUP_W:4 * GROUP_W] = _pool(z, z, 0, 0, s_len, wbd_ref, ps_ref[...])


def _whole(shape):
    nd = len(shape)
    return pl.BlockSpec(shape, lambda *_: (0,) * nd)


def _ctx_mix(p, layer_w, lam_init):
    cw, cb_full, sink, lq, lk, sg, wbd, ps = layer_w
    return pl.pallas_call(
        functools.partial(_ctx_mix_kernel, lam_init),
        grid=(N_CTX_SEQ,),
        in_specs=[pl.BlockSpec((CTX_LEN, P_COLS), lambda b: (b, 0)),
                  _whole(cw.shape), _whole(cb_full.shape),
                  pl.BlockSpec(memory_space=pltpu.SMEM),
                  _whole(lq.shape), _whole(lk.shape), _whole(sg.shape), _whole(wbd.shape),
                  _whole(ps.shape)],
        out_specs=pl.BlockSpec((CTX_LEN, D_MODEL), lambda b: (b, 0)),
        out_shape=jax.ShapeDtypeStruct((N_CTX_TOK, D_MODEL), F32),
        compiler_params=pltpu.CompilerParams(vmem_limit_bytes=VMEM_LIMIT),
        name="ctx_mix",
    )(p, cw, cb_full, sink, lq, lk, sg, wbd, ps)


def _rope(x, cos, sin_up, sin_dn, shift):
    outs = []
    for c0 in range(0, x.shape[1], LANES):
        xs = x[:, c0:c0 + LANES]
        up = pltpu.roll(xs, LANES - shift, 1)
        dn = pltpu.roll(xs, shift, 1)
        outs.append(xs * cos[:, c0:c0 + LANES] + up * sin_up[:, c0:c0 + LANES]
                    + dn * sin_dn[:, c0:c0 + LANES])
    return outs[0] if len(outs) == 1 else jnp.concatenate(outs, axis=1)


def _lat_mix_kernel(lam_init, p_ref, wk_ref, wv_ref, dk_ref, dv_ref,
                    cosb_ref, sbu_ref, sbd_ref, cosc_ref, scu_ref, scd_ref,
                    cw_ref, cb_ref, sink_ref, lq_ref, lk_ref, sg_ref, wbd_ref, ps_ref,
                    o_ref, kb_s, kc_s):
    n = pl.program_id(1)
    s_len = LAT_LEN

    @pl.when(n == 0)
    def _():
        kb_s[...] = _rope(p_ref[:, COL_BK:COL_BK + 128], cosb_ref[:, 0:128], sbu_ref[:, 0:128],
                          sbd_ref[:, 0:128], HEAD_D // 4).astype(BF16)
        kc_s[...] = _rope(p_ref[:, COL_CK:COL_CK + GROUP_W], cosc_ref[...], scu_ref[...],
                          scd_ref[...], C_SUB // 4).astype(BF16)

    r0 = pl.multiple_of(n * BLOCK, BLOCK)
    rows = pl.ds(r0, BLOCK)
    o_ref[:, 0:GROUP_W] = _chunk_gating(p_ref[rows, COL_AU:COL_AU + GROUP_W],
                                        p_ref[rows, COL_AV:COL_AV + GROUP_W], cw_ref, cb_ref[...])
    w0 = pl.multiple_of(jnp.clip((n - 1) * BLOCK, 0, s_len - 3 * BLOCK), BLOCK)
    win = pl.ds(w0, 3 * BLOCK)
    qb = _rope(p_ref[rows, COL_BQ:COL_BQ + GROUP_W], cosb_ref[rows, :], sbu_ref[rows, :],
               sbd_ref[rows, :], HEAD_D // 4) * (HEAD_D ** -0.5)
    qpos = r0 + lax.broadcasted_iota(jnp.int32, (BLOCK, 3 * BLOCK), 0)
    kpos = w0 + lax.broadcasted_iota(jnp.int32, (BLOCK, 3 * BLOCK), 1)
    valid = jnp.abs(kpos - qpos) <= WINDOW
    for kv in range(2):
        cs = slice(kv * HEAD_D, (kv + 1) * HEAD_D)
        k_loc = kb_s[win, cs]
        v_loc = p_ref[win, COL_BV + kv * HEAD_D:COL_BV + (kv + 1) * HEAD_D].astype(BF16)
        k_ctx = wk_ref[:, cs].astype(BF16)
        v_ctx = wv_ref[:, cs].astype(BF16)
        for g in range(2):
            h = kv * 2 + g
            q = qb[:, h * HEAD_D:(h + 1) * HEAD_D]
            s_loc = jnp.where(valid, _mm_nt(q, k_loc), NEG)
            s_ctx = _mm_nt(q, k_ctx)
            sk = sink_ref[h]
            m = jnp.maximum(jnp.maximum(jnp.max(s_loc, axis=-1, keepdims=True),
                                        jnp.max(s_ctx, axis=-1, keepdims=True)), sk)
            e_loc = jnp.exp(s_loc - m)
            e_ctx = jnp.exp(s_ctx - m)
            den = (jnp.sum(e_loc, axis=-1, keepdims=True) + jnp.sum(e_ctx, axis=-1, keepdims=True)
                   + jnp.exp(sk - m))
            o_ref[:, GROUP_W + h * HEAD_D:GROUP_W + (h + 1) * HEAD_D] = (
                _mm(e_loc, v_loc) + _mm(e_ctx, v_ctx)) / den
    lam = _diff_lambda(lq_ref, lk_ref, lam_init)
    scale = C_SUB ** -0.5
    qc = _rope(p_ref[rows, COL_CQ:COL_CQ + GROUP_W], cosc_ref[rows, :], scu_ref[rows, :],
               scd_ref[rows, :], C_SUB // 4)
    for h in range(4):
        v_ctx = dv_ref[:, h * HEAD_D:(h + 1) * HEAD_D].astype(BF16)
        v_lat = p_ref[:, COL_CV + h * HEAD_D:COL_CV + (h + 1) * HEAD_D].astype(BF16)
        parts = []
        for j in range(2):
            cs = slice(h * HEAD_D + j * C_SUB, h * HEAD_D + (j + 1) * C_SUB)
            q = qc[:, cs]
            s_ctx = _mm_nt(q, dk_ref[:, cs]) * scale
            s_lat = _mm_nt(q, kc_s[:, cs]) * scale
            m = jnp.maximum(jnp.max(s_ctx, axis=-1, keepdims=True), jnp.max(s_lat, axis=-1, keepdims=True))
            e_ctx = jnp.exp(s_ctx - m)
            e_lat = jnp.exp(s_lat - m)
            den = jnp.sum(e_ctx, axis=-1, keepdims=True) + jnp.sum(e_lat, axis=-1, keepdims=True)
            parts.append((_mm(e_ctx, v_ctx) + _mm(e_lat, v_lat)) / den)
        o = parts[0] - lam * parts[1]
        o_ref[:, 2 * GROUP_W + h * HEAD_D:2 * GROUP_W + (h + 1) * HEAD_D] = _sub_norm(o, sg_ref[...], lam_init)
    o_ref[:, 3 * GROUP_W:4 * GROUP_W] = _pool(p_ref[rows, COL_DZ:COL_DZ + GROUP_W],
                                              p_ref[win, COL_DZ:COL_DZ + GROUP_W],
                                              r0, w0, s_len, wbd_ref, ps_ref[...])


def _lat_mix(p, caches, rope_tabs, layer_w, lam_init, layer):
    cw, cb_full, sink, lq, lk, sg, wbd, ps = layer_w
    nb = LAT_LEN // BLOCK
    ctx_blocks = N_CTX_TOK // LAT_LEN
    cache_spec = lambda width: pl.BlockSpec((None, None, CTX_LEN, width), lambda b, n: (b, layer, 0, 0))
    rope_spec = pl.BlockSpec((LAT_LEN, GROUP_W), lambda b, n: (0, 0))
    return pl.pallas_call(
        functools.partial(_lat_mix_kernel, lam_init),
        grid=(N_LAT_SEQ, nb),
        in_specs=[pl.BlockSpec((LAT_LEN, P_COLS), lambda b, n: (ctx_blocks + b, 0)),
                  cache_spec(128), cache_spec(128), cache_spec(256), cache_spec(256)]
                 + [rope_spec] * 6
                 + [_whole(cw.shape), _whole(cb_full.shape), pl.BlockSpec(memory_space=pltpu.SMEM),
                    _whole(lq.shape), _whole(lk.shape), _whole(sg.shape), _whole(wbd.shape),
                    _whole(ps.shape)],
        out_specs=pl.BlockSpec((BLOCK, D_MODEL), lambda b, n: (b * nb + n, 0)),
        out_shape=jax.ShapeDtypeStruct((N_LAT_TOK, D_MODEL), F32),
        scratch_shapes=[pltpu.VMEM((LAT_LEN, 128), BF16), pltpu.VMEM((LAT_LEN, GROUP_W), BF16)],
        compiler_params=pltpu.CompilerParams(vmem_limit_bytes=VMEM_LIMIT,
                                             dimension_semantics=("arbitrary", "arbitrary")),
        name="lat_mix",
    )(p, *caches, *rope_tabs, cw, cb_full, sink, lq, lk, sg, wbd, ps)


def _out_proj_kernel(o_ref, x_ref, mod_ref, wo_ref, g_ref, b_ref, wq_ref, x1_ref, h2_ref, q_ref):
    m = mod_ref[...]
    g1 = m[:, 2 * D_MODEL:3 * D_MODEL]
    sh2 = m[:, 3 * D_MODEL:4 * D_MODEL]
    sc2 = m[:, 4 * D_MODEL:5 * D_MODEL]
    y = jnp.dot(o_ref[...].astype(BF16), wo_ref[...], preferred_element_type=F32)
    x1 = _ln(ALPHA * x_ref[...] + g1 * y) * g_ref[0:1, :] + b_ref[0:1, :]
    x1_ref[...] = x1
    h2 = _ln(x1) * (1.0 + sc2) + sh2
    h2_ref[...] = h2
    q_ref[...] = jnp.dot(h2.astype(BF16), wq_ref[...], preferred_element_type=F32)


def _out_proj(o, x, mods, w_out_bf16, ln_g, ln_b, wq_bf16, layer):
    row_spec = pl.BlockSpec((ROW_BLOCK, D_MODEL), lambda i: (i, 0))
    mat_spec = pl.BlockSpec((None, D_MODEL, D_MODEL), lambda i: (layer, 0, 0))
    vec_spec = pl.BlockSpec((None, 2, D_MODEL), lambda i: (layer, 0, 0))
    shape = jax.ShapeDtypeStruct((N_TOK, D_MODEL), F32)
    return pl.pallas_call(
        _out_proj_kernel,
        grid=(N_ROW_BLOCKS,),
        in_specs=[row_spec, row_spec,
                  pl.BlockSpec((None, None, 1, 6 * D_MODEL), lambda i: (layer, _mod_group(i), 0, 0)),
                  mat_spec, vec_spec, vec_spec, mat_spec],
        out_specs=[row_spec, row_spec, row_spec],
        out_shape=[shape, shape, shape],
        compiler_params=pltpu.CompilerParams(vmem_limit_bytes=VMEM_LIMIT),
        name="out_proj",
    )(o, x, mods, w_out_bf16, ln_g, ln_b, wq_bf16)


def _top16_rows(sc, payload=None):
    r = sc.shape[0]
    row = lax.broadcasted_iota(jnp.int32, sc.shape, 0)
    out_row = lax.broadcasted_iota(jnp.int32, (PK_TOPK, sc.shape[1]), 0)
    vals = jnp.zeros((PK_TOPK, sc.shape[1]), F32)
    sel = jnp.zeros((PK_TOPK, sc.shape[1]), jnp.int32)
    for i in range(PK_TOPK):
        m = jnp.max(sc, axis=0, keepdims=True)
        am = jnp.min(jnp.where(sc == m, row, r), axis=0, keepdims=True)
        hit = row == am
        if payload is None:
            picked = am
        else:
            picked = jnp.sum(jnp.where(hit, payload, 0), axis=0, keepdims=True)
        vals = jnp.where(out_row == i, m, vals)
        sel = jnp.where(out_row == i, picked, sel)
        sc = jnp.where(hit, NEG, sc)
    return vals, sel


def _route_kernel(q_ref, keys_ref, idx_ref, gate_ref):
    idx_heads = []
    gate_heads = []
    for h in range(PK_HEADS):
        sv = []
        si = []
        for j in range(2):
            c0 = (h * 2 + j) * HEAD_D
            sc = lax.dot_general(keys_ref[j], q_ref[:, c0:c0 + HEAD_D], (((1,), (1,)), ((), ())),
                                 precision=HIGHEST, preferred_element_type=F32)
            v, i = _top16_rows(sc)
            sv.append(v)
            si.append(i)
        cand = jnp.concatenate([sv[0][a:a + 1, :] + sv[1] for a in range(PK_TOPK)], axis=0)
        cidx = jnp.concatenate([si[0][a:a + 1, :] * N_KEYS + si[1] for a in range(PK_TOPK)], axis=0)
        cv, ce = _top16_rows(cand, cidx)
        e = jnp.exp(cv - jnp.max(cv, axis=0, keepdims=True))
        gate_heads.append(e / jnp.sum(e, axis=0, keepdims=True))
        idx_heads.append(ce)
    idx_t = jnp.concatenate(idx_heads, axis=0)
    gate_t = jnp.concatenate(gate_heads, axis=0)
    idx_ref[...] = idx_t.astype(F32).T.astype(jnp.int32)
    gate_ref[...] = gate_t.T


def _route(q, keys):
    out_spec = pl.BlockSpec((ROUTE_BLOCK, N_PICKS), lambda i: (i, 0))
    return pl.pallas_call(
        _route_kernel,
        grid=(N_TOK // ROUTE_BLOCK,),
        in_specs=[pl.BlockSpec((ROUTE_BLOCK, D_MODEL), lambda i: (i, 0)), _whole(keys.shape)],
        out_specs=[out_spec, out_spec],
        out_shape=[jax.ShapeDtypeStruct((N_TOK, N_PICKS), jnp.int32),
                   jax.ShapeDtypeStruct((N_TOK, N_PICKS), F32)],
        compiler_params=pltpu.CompilerParams(vmem_limit_bytes=VMEM_LIMIT),
        name="route",
    )(q, keys)


def _half_offsets(idx, half):
    local = idx - half * HALF_EXPERTS
    inside = (local >= 0) & (local < HALF_EXPERTS)
    return jnp.where(inside, local, 0) * SUBLANES, inside


def _peer_dots(loc_ref, h_ref, tab_ref, out_ref):
    lane = lax.broadcasted_iota(jnp.int32, (SUBLANES, LANES), 1)

    def token(t, carry):
        ht = h_ref[pl.ds(pl.multiple_of(t * SUBLANES, SUBLANES), SUBLANES), :]
        acc = jnp.zeros((SUBLANES, LANES), F32)
        for e in range(N_PICKS):
            r = pl.multiple_of(loc_ref[t, e], SUBLANES)
            s = jnp.sum(tab_ref[pl.ds(r, SUBLANES), :] * ht, axis=1, keepdims=True)
            acc = jnp.where(lane == e, s, acc)
        out_ref[pl.ds(t, 1), :] = jnp.sum(acc, axis=0, keepdims=True)
        return carry

    lax.fori_loop(0, PEER_BLOCK, token, 0)


def _peer_u_first_kernel(loc_ref, h_ref, tab_ref, act_ref):
    _peer_dots(loc_ref, h_ref, tab_ref, act_ref)


def _peer_u_second_kernel(loc_ref, h_ref, tab_ref, act0_ref, idx_ref, gate_ref, w0_ref, w1_ref, act_s):
    _peer_dots(loc_ref, h_ref, tab_ref, act_s)
    in0 = idx_ref[...] < HALF_EXPERTS
    w = gate_ref[...] * _gelu(jnp.where(in0, act0_ref[...], act_s[...]))
    w0_ref[...] = jnp.where(in0, w, 0.0)
    w1_ref[...] = jnp.where(in0, 0.0, w)


def _peer_sum(loc_ref, w_ref, tab_ref, init, f_ref):
    def token(t, carry):
        rows = pl.ds(pl.multiple_of(t * SUBLANES, SUBLANES), SUBLANES)
        accs = [init(rows)] + [jnp.zeros((SUBLANES, LANES), F32)] * 3
        for e in range(N_PICKS):
            r = pl.multiple_of(loc_ref[t, e], SUBLANES)
            accs[e % 4] = accs[e % 4] + tab_ref[pl.ds(r, SUBLANES), :] * w_ref[t, e]
        f_ref[rows, :] = (accs[0] + accs[1]) + (accs[2] + accs[3])
        return carry

    lax.fori_loop(0, PEER_BLOCK, token, 0)


def _peer_v_first_kernel(loc_ref, w_ref, tab_ref, f_ref):
    _peer_sum(loc_ref, w_ref, tab_ref, lambda rows: jnp.zeros((SUBLANES, LANES), F32), f_ref)


def _peer_v_second_kernel(loc_ref, w_ref, tab_ref, f0_ref, f_ref):
    _peer_sum(loc_ref, w_ref, tab_ref, lambda rows: f0_ref[rows, :], f_ref)


def _peer(h2, idx, gate, u_tab, v_tab, layer):
    nblk = N_TOK // PEER_BLOCK
    h_tiles = h2.reshape(N_TOK * SUBLANES, LANES)
    loc0, _ = _half_offsets(idx, 0)
    loc1, _ = _half_offsets(idx, 1)
    smem_spec = pl.BlockSpec((PEER_BLOCK, N_PICKS), lambda i: (i, 0), memory_space=pltpu.SMEM)
    pick_spec = pl.BlockSpec((PEER_BLOCK, N_PICKS), lambda i: (i, 0))
    tile_spec = pl.BlockSpec((PEER_BLOCK * SUBLANES, LANES), lambda i: (i, 0))
    half_spec = lambda half: pl.BlockSpec((None, HALF_EXPERTS * SUBLANES, LANES),
                                          lambda i: (layer, half, 0), pipeline_mode=pl.Buffered(1))
    pick_shape = jax.ShapeDtypeStruct((N_TOK, N_PICKS), F32)
    tile_shape = jax.ShapeDtypeStruct((N_TOK * SUBLANES, LANES), F32)
    params = pltpu.CompilerParams(vmem_limit_bytes=VMEM_LIMIT)

    act0 = pl.pallas_call(
        _peer_u_first_kernel, grid=(nblk,),
        in_specs=[smem_spec, tile_spec, half_spec(0)],
        out_specs=pick_spec, out_shape=pick_shape, compiler_params=params, name="peer_u0",
    )(loc0, h_tiles, u_tab)
    w0, w1 = pl.pallas_call(
        _peer_u_second_kernel, grid=(nblk,),
        in_specs=[smem_spec, tile_spec, half_spec(1), pick_spec, pick_spec, pick_spec],
        out_specs=[pick_spec, pick_spec], out_shape=[pick_shape, pick_shape],
        scratch_shapes=[pltpu.VMEM((PEER_BLOCK, N_PICKS), F32)],
        compiler_params=params, name="peer_u1",
    )(loc1, h_tiles, u_tab, act0, idx, gate)
    f0 = pl.pallas_call(
        _peer_v_first_kernel, grid=(nblk,),
        in_specs=[smem_spec, smem_spec, half_spec(0)],
        out_specs=tile_spec, out_shape=tile_shape, compiler_params=params, name="peer_v0",
    )(loc0, w0, v_tab)
    f = pl.pallas_call(
        _peer_v_second_kernel, grid=(nblk,),
        in_specs=[smem_spec, smem_spec, half_spec(1), tile_spec],
        out_specs=tile_spec, out_shape=tile_shape, compiler_params=params, name="peer_v1",
    )(loc1, w1, v_tab, f0)
    return f.reshape(N_TOK, D_MODEL)


def _close_kernel(x1_ref, f_ref, mod_ref, g_ref, b_ref, o_ref):
    g2 = mod_ref[...][:, 5 * D_MODEL:6 * D_MODEL]
    o_ref[...] = _ln(ALPHA * x1_ref[...] + g2 * f_ref[...]) * g_ref[1:2, :] + b_ref[1:2, :]


def _close(x1, f, mods, ln_g, ln_b, layer):
    row_spec = pl.BlockSpec((ROW_BLOCK, D_MODEL), lambda i: (i, 0))
    vec_spec = pl.BlockSpec((None, 2, D_MODEL), lambda i: (layer, 0, 0))
    return pl.pallas_call(
        _close_kernel,
        grid=(N_ROW_BLOCKS,),
        in_specs=[row_spec, row_spec,
                  pl.BlockSpec((None, None, 1, 6 * D_MODEL), lambda i: (layer, _mod_group(i), 0, 0)),
                  vec_spec, vec_spec],
        out_specs=row_spec,
        out_shape=jax.ShapeDtypeStruct((N_TOK, D_MODEL), F32),
        compiler_params=pltpu.CompilerParams(vmem_limit_bytes=VMEM_LIMIT),
        name="close",
    )(x1, f, mods, ln_g, ln_b)


def _rope_tables(dim, copies):
    rows = LAT_LEN // GRID_W
    row = jnp.repeat(jnp.arange(rows, dtype=F32), GRID_W)
    col = jnp.tile(jnp.arange(GRID_W, dtype=F32), rows)
    nf = dim // 4
    inv = ROPE_BASE ** (-jnp.arange(nf, dtype=F32) / nf)
    ar = row[:, None] * inv
    ac = col[:, None] * inv
    ang = jnp.concatenate([ar, ar, ac, ac], axis=-1)
    cos, sin = jnp.cos(ang), jnp.sin(ang)
    even_quarter = (jnp.arange(dim) // nf) % 2 == 0
    sin_up = jnp.where(even_quarter, -sin, 0.0)
    sin_dn = jnp.where(even_quarter, 0.0, sin)
    return tuple(jnp.tile(t, (1, copies)) for t in (cos, sin_up, sin_dn))


def kernel(x_prompt, x_sample, c, cache_win_k, cache_win_v, cache_diff_k, cache_diff_v, c_ctx, w_mod, b_mod, w_in, w_out, chunk_w, chunk_b, win_sink, diff_lam_q, diff_lam_k, diff_subln_g, pool_w, pool_scale, ln_g, ln_b, peer_wq, peer_keys, peer_u, peer_v):
    cvec = jnp.concatenate([c_ctx[None, :], c, jnp.zeros((8 - 1 - N_LAT_SEQ, D_MODEL), F32)], axis=0)
    mods = _modulation(cvec, w_mod, b_mod)[:, :1 + N_LAT_SEQ].reshape(DEPTH, 1 + N_LAT_SEQ, 1, 6 * D_MODEL)

    w_in_b = w_in.astype(BF16)
    w_out_b = w_out.astype(BF16)
    wq_b = peer_wq.astype(BF16)
    cw_b = chunk_w.astype(BF16)
    cb_full = jnp.repeat(jnp.swapaxes(chunk_b, 1, 2), HEAD_D, axis=2)
    eye = jnp.eye(4, dtype=F32)
    wbd = jnp.einsum('lgcd,gh->lgchd', pool_w, eye).reshape(DEPTH, GROUP_W, GROUP_W).astype(BF16)
    caches = (cache_win_k.reshape(N_LAT_SEQ, DEPTH, CTX_LEN, 128),
              cache_win_v.reshape(N_LAT_SEQ, DEPTH, CTX_LEN, 128),
              cache_diff_k.reshape(N_LAT_SEQ, DEPTH, CTX_LEN, GROUP_W),
              cache_diff_v.reshape(N_LAT_SEQ, DEPTH, CTX_LEN, GROUP_W))
    rope_tabs = _rope_tables(HEAD_D, GROUP_W // HEAD_D) + _rope_tables(C_SUB, GROUP_W // C_SUB)
    u_tiles = peer_u.reshape(DEPTH, N_EXPERTS * SUBLANES, LANES)
    v_tiles = peer_v.reshape(DEPTH, N_EXPERTS * SUBLANES, LANES)

    x = jnp.concatenate([x_prompt.reshape(N_CTX_TOK, D_MODEL), x_sample.reshape(N_LAT_TOK, D_MODEL)], axis=0)
    kbs, vbs, kcs, vcs = [], [], [], []
    for l in range(DEPTH):
        lam_init = 0.8 - 0.6 * math.exp(-0.3 * l)
        layer_w = (cw_b[l], cb_full[l], win_sink[l], diff_lam_q[l], diff_lam_k[l],
                   diff_subln_g[l].reshape(1, HEAD_D), wbd[l], pool_scale[l].reshape(1, GROUP_W))
        p = _in_proj(x, mods, w_in_b, l)
        pc = p[:N_CTX_TOK].reshape(N_CTX_SEQ, CTX_LEN, P_COLS)
        kbs.append(pc[..., COL_BK:COL_BV].reshape(N_CTX_SEQ, CTX_LEN, 2, HEAD_D))
        vbs.append(pc[..., COL_BV:COL_CQ].reshape(N_CTX_SEQ, CTX_LEN, 2, HEAD_D))
        kcs.append(pc[..., COL_CK:COL_CV].reshape(N_CTX_SEQ, CTX_LEN, 4, 2, C_SUB))
        vcs.append(pc[..., COL_CV:COL_DZ].reshape(N_CTX_SEQ, CTX_LEN, 4, HEAD_D))
        o = jnp.concatenate([_ctx_mix(p, layer_w, lam_init),
                             _lat_mix(p, caches, rope_tabs, layer_w, lam_init, l)], axis=0)
        x1, h2, q = _out_proj(o, x, mods, w_out_b, ln_g, ln_b, wq_b, l)
        idx, gate = _route(q, peer_keys[l])
        f = _peer(h2, idx, gate, u_tiles, v_tiles, l)
        x = _close(x1, f, mods, ln_g, ln_b, l)
    return (x[:N_CTX_TOK].reshape(N_CTX_SEQ, CTX_LEN, D_MODEL),
            x[N_CTX_TOK:].reshape(N_LAT_SEQ, LAT_LEN, D_MODEL),
            jnp.stack(kbs, axis=1), jnp.stack(vbs, axis=1), jnp.stack(kcs, axis=1), jnp.stack(vcs, axis=1))
```

```python
import functools
import math

import jax
import jax.numpy as jnp
import numpy as np
from jax import lax
from jax.experimental import pallas as pl
from jax.experimental.pallas import tpu as pltpu

F32 = jnp.float32
BF16 = jnp.bfloat16
HIGHEST = lax.Precision.HIGHEST

D_MODEL = 1024
DEPTH = 4
N_CTX_SEQ = 16
CTX_LEN = 256
N_LAT_SEQ = 2
LAT_LEN = 1024
N_CTX_TOK = N_CTX_SEQ * CTX_LEN
N_LAT_TOK = N_LAT_SEQ * LAT_LEN
N_TOK = N_CTX_TOK + N_LAT_TOK
GRID_W = 64
BLOCK = 128
GROUP_W = 256
HEAD_D = 64
C_SUB = 32
WINDOW = 128
POOL_SIZES = (2, 4, 8, 16)
P_COLS = 2048
COL_AU, COL_AV, COL_BQ, COL_BK, COL_BV, COL_CQ, COL_CK, COL_CV, COL_DZ = (
    0, 256, 512, 768, 896, 1024, 1280, 1536, 1792)
PK_HEADS = 8
N_KEYS = 128
N_EXPERTS = N_KEYS * N_KEYS
PK_TOPK = 16
N_PICKS = PK_HEADS * PK_TOPK
ROPE_BASE = 10000.0
ALPHA = (2 * DEPTH) ** 0.25
LN_EPS = 1e-5
NEG = -1e30

ROW_BLOCK = 512
N_ROW_BLOCKS = N_TOK // ROW_BLOCK
N_CTX_ROW_BLOCKS = N_CTX_TOK // ROW_BLOCK
ROW_BLOCKS_PER_LAT_SEQ = LAT_LEN // ROW_BLOCK
ROUTE_BLOCK = 256
PEER_BLOCK = 128
SUBLANES = 8
LANES = 128
HALF_EXPERTS = N_EXPERTS // 2
VMEM_LIMIT = 56 * 1024 * 1024


def _mm(a, b):
    return jnp.dot(a.astype(BF16), b.astype(BF16), preferred_element_type=F32)


def _mm_nt(a, b):
    return lax.dot_general(a.astype(BF16), b.astype(BF16), (((1,), (1,)), ((), ())),
                           preferred_element_type=F32)


def _gelu(x):
    return x * (0.5 * (1.0 + jnp.tanh(0.7978845608028654 * (x + 0.044715 * (x * x * x)))))


def _ln(x):
    mu = jnp.mean(x, axis=-1, keepdims=True)
    xc = x - mu
    var = jnp.mean(xc * xc, axis=-1, keepdims=True)
    return xc * lax.rsqrt(var + LN_EPS)


def _mod_group(i):
    return jnp.where(i < N_CTX_ROW_BLOCKS, 0, 1 + (i - N_CTX_ROW_BLOCKS) // ROW_BLOCKS_PER_LAT_SEQ)


def _mod_kernel(c_ref, w_ref, b_ref, o_ref):
    c = c_ref[...]
    s = c * jax.nn.sigmoid(c)
    o_ref[...] = jnp.dot(s, w_ref[...], precision=HIGHEST, preferred_element_type=F32) + b_ref[...]


def _modulation(cvec, w_mod, b_mod):
    nj = 4
    cols = 6 * D_MODEL // nj
    return pl.pallas_call(
        _mod_kernel,
        grid=(DEPTH, nj),
        in_specs=[pl.BlockSpec((8, D_MODEL), lambda l, j: (0, 0)),
                  pl.BlockSpec((None, D_MODEL, cols), lambda l, j: (l, 0, j)),
                  pl.BlockSpec((None, 1, cols), lambda l, j: (l, 0, j))],
        out_specs=pl.BlockSpec((None, 8, cols), lambda l, j: (l, 0, j)),
        out_shape=jax.ShapeDtypeStruct((DEPTH, 8, 6 * D_MODEL), F32),
        compiler_params=pltpu.CompilerParams(vmem_limit_bytes=VMEM_LIMIT),
        name="modulation",
    )(cvec, w_mod, b_mod.reshape(DEPTH, 1, 6 * D_MODEL))


def _in_proj_kernel(x_ref, mod_ref, w_ref, o_ref):
    m = mod_ref[...]
    h = _ln(x_ref[...]) * (1.0 + m[:, D_MODEL:2 * D_MODEL]) + m[:, 0:D_MODEL]
    o_ref[...] = jnp.dot(h.astype(BF16), w_ref[...], preferred_element_type=F32)


def _in_proj(x, mods, w_in_bf16, layer):
    return pl.pallas_call(
        _in_proj_kernel,
        grid=(N_ROW_BLOCKS,),
        in_specs=[pl.BlockSpec((ROW_BLOCK, D_MODEL), lambda i: (i, 0)),
                  pl.BlockSpec((None, None, 1, 6 * D_MODEL), lambda i: (layer, _mod_group(i), 0, 0)),
                  pl.BlockSpec((None, D_MODEL, P_COLS), lambda i: (layer, 0, 0))],
        out_specs=pl.BlockSpec((ROW_BLOCK, P_COLS), lambda i: (i, 0)),
        out_shape=jax.ShapeDtypeStruct((N_TOK, P_COLS), F32),
        compiler_params=pltpu.CompilerParams(vmem_limit_bytes=VMEM_LIMIT),
        name="in_proj",
    )(x, mods, w_in_bf16)


def _chunk_gating(u_raw, v_raw, cw_ref, cb_full):
    u = _gelu(u_raw)
    v = _gelu(v_raw).astype(BF16)
    head = lax.broadcasted_iota(jnp.int32, (BLOCK, GROUP_W), 1) // HEAD_D
    mixed = cb_full
    for h in range(4):
        mixed = mixed + jnp.where(head == h, jnp.dot(cw_ref[h], v, preferred_element_type=F32), 0.0)
    return u * mixed


def _pool(z_rows, z_win, row0, key0, seq_len, wbd_ref, pool_scale):
    r, k = z_rows.shape[0], z_win.shape[0]
    t = row0 + lax.broadcasted_iota(jnp.int32, (r, k), 0)
    pos = key0 + lax.broadcasted_iota(jnp.int32, (r, k), 1)
    t1 = row0 + lax.broadcasted_iota(jnp.int32, (r, 1), 0)
    z_hi = z_win.astype(BF16)
    z_lo = (z_win - z_hi.astype(F32)).astype(BF16)
    group = lax.broadcasted_iota(jnp.int32, (r, GROUP_W), 1) // HEAD_D
    mean = jnp.zeros((r, GROUP_W), F32)
    for g, w in enumerate(POOL_SIZES):
        lo = jnp.maximum(t - w // 2, 0)
        hi = jnp.minimum(t + w // 2, seq_len)
        band = jnp.where((pos >= lo) & (pos < hi), 1.0, 0.0).astype(BF16)
        tot = (jnp.dot(band, z_hi, preferred_element_type=F32)
               + jnp.dot(band, z_lo, preferred_element_type=F32))
        cnt = (jnp.minimum(t1 + w // 2, seq_len) - jnp.maximum(t1 - w // 2, 0)).astype(F32)
        mean = mean + jnp.where(group == g, tot / cnt, 0.0)
    pooled = mean - z_rows
    return jnp.dot(pooled.astype(BF16), wbd_ref[...], preferred_element_type=F32) * pool_scale


def _diff_lambda(lq_ref, lk_ref, lam_init):
    lq = lq_ref[...]
    lk = lk_ref[...]
    s0 = jnp.sum(lq[0:1] * lk[0:1], axis=-1, keepdims=True)
    s1 = jnp.sum(lq[1:2] * lk[1:2], axis=-1, keepdims=True)
    return jnp.exp(s0) - jnp.exp(s1) + lam_init


def _sub_norm(o, g, lam_init):
    ms = jnp.mean(o * o, axis=-1, keepdims=True)
    return o * lax.rsqrt(ms + LN_EPS) * g * (1.0 - lam_init)


def _ctx_mix_kernel(lam_init, p_ref, cw_ref, cb_ref, sink_ref, lq_ref, lk_ref, sg_ref, wbd_ref,
                    ps_ref, o_ref):
    s_len = CTX_LEN
    for n in range(s_len // BLOCK):
        rows = slice(n * BLOCK, (n + 1) * BLOCK)
        o_ref[rows, 0:GROUP_W] = _chunk_gating(p_ref[rows, COL_AU:COL_AU + GROUP_W],
                                               p_ref[rows, COL_AV:COL_AV + GROUP_W], cw_ref, cb_ref[...])
    for kv in range(2):
        kk = p_ref[:, COL_BK + kv * HEAD_D:COL_BK + (kv + 1) * HEAD_D].astype(BF16)
        vv = p_ref[:, COL_BV + kv * HEAD_D:COL_BV + (kv + 1) * HEAD_D].astype(BF16)
        for g in range(2):
            h = kv * 2 + g
            q = p_ref[:, COL_BQ + h * HEAD_D:COL_BQ + (h + 1) * HEAD_D] * (HEAD_D ** -0.5)
            s = _mm_nt(q, kk)
            sk = sink_ref[h]
            m = jnp.maximum(jnp.max(s, axis=-1, keepdims=True), sk)
            e = jnp.exp(s - m)
            den = jnp.sum(e, axis=-1, keepdims=True) + jnp.exp(sk - m)
            o_ref[:, GROUP_W + h * HEAD_D:GROUP_W + (h + 1) * HEAD_D] = _mm(e, vv) / den
    lam = _diff_lambda(lq_ref, lk_ref, lam_init)
    scale = C_SUB ** -0.5
    for h in range(4):
        vv = p_ref[:, COL_CV + h * HEAD_D:COL_CV + (h + 1) * HEAD_D].astype(BF16)
        parts = []
        for j in range(2):
            c0 = h * HEAD_D + j * C_SUB
            s = _mm_nt(p_ref[:, COL_CQ + c0:COL_CQ + c0 + C_SUB],
                       p_ref[:, COL_CK + c0:COL_CK + c0 + C_SUB]) * scale
            e = jnp.exp(s - jnp.max(s, axis=-1, keepdims=True))
            parts.append(_mm(e, vv) / jnp.sum(e, axis=-1, keepdims=True))
        o = parts[0] - lam * parts[1]
        o_ref[:, 2 * GROUP_W + h * HEAD_D:2 * GROUP_W + (h + 1) * HEAD_D] = _sub_norm(o, sg_ref[...], lam_init)
    z = p_ref[:, COL_DZ:COL_DZ + GROUP_W]
    o_ref[:, 3 * GROUP_W:4 * GROUP_W] = _pool(z, z, 0, 0, s_len, wbd_ref, ps_ref[...])


def _whole(shape):
    nd = len(shape)
    return pl.BlockSpec(shape, lambda *_: (0,) * nd)


def _ctx_mix(p, layer_w, lam_init):
    cw, cb_full, sink, lq, lk, sg, wbd, ps = layer_w
    return pl.pallas_call(
        functools.partial(_ctx_mix_kernel, lam_init),
        grid=(N_CTX_SEQ,),
        in_specs=[pl.BlockSpec((CTX_LEN, P_COLS), lambda b: (b, 0)),
                  _whole(cw.shape), _whole(cb_full.shape),
                  pl.BlockSpec(memory_space=pltpu.SMEM),
                  _whole(lq.shape), _whole(lk.shape), _whole(sg.shape), _whole(wbd.shape),
                  _whole(ps.shape)],
        out_specs=pl.BlockSpec((CTX_LEN, D_MODEL), lambda b: (b, 0)),
        out_shape=jax.ShapeDtypeStruct((N_CTX_TOK, D_MODEL), F32),
        compiler_params=pltpu.CompilerParams(vmem_limit_bytes=VMEM_LIMIT),
        name="ctx_mix",
    )(p, cw, cb_full, sink, lq, lk, sg, wbd, ps)


def _rope(x, cos, sin_up, sin_dn, shift):
    outs = []
    for c0 in range(0, x.shape[1], LANES):
        xs = x[:, c0:c0 + LANES]
        up = pltpu.roll(xs, LANES - shift, 1)
        dn = pltpu.roll(xs, shift, 1)
        outs.append(xs * cos[:, c0:c0 + LANES] + up * sin_up[:, c0:c0 + LANES]
                    + dn * sin_dn[:, c0:c0 + LANES])
    return outs[0] if len(outs) == 1 else jnp.concatenate(outs, axis=1)


def _lat_mix_kernel(lam_init, p_ref, wk_ref, wv_ref, dk_ref, dv_ref,
                    cosb_ref, sbu_ref, sbd_ref, cosc_ref, scu_ref, scd_ref,
                    cw_ref, cb_ref, sink_ref, lq_ref, lk_ref, sg_ref, wbd_ref, ps_ref,
                    o_ref, kb_s, kc_s):
    n = pl.program_id(1)
    s_len = LAT_LEN

    @pl.when(n == 0)
    def _():
        kb_s[...] = _rope(p_ref[:, COL_BK:COL_BK + 128], cosb_ref[:, 0:128], sbu_ref[:, 0:128],
                          sbd_ref[:, 0:128], HEAD_D // 4).astype(BF16)
        kc_s[...] = _rope(p_ref[:, COL_CK:COL_CK + GROUP_W], cosc_ref[...], scu_ref[...],
                          scd_ref[...], C_SUB // 4).astype(BF16)

    r0 = pl.multiple_of(n * BLOCK, BLOCK)
    rows = pl.ds(r0, BLOCK)
    o_ref[:, 0:GROUP_W] = _chunk_gating(p_ref[rows, COL_AU:COL_AU + GROUP_W],
                                        p_ref[rows, COL_AV:COL_AV + GROUP_W], cw_ref, cb_ref[...])
    w0 = pl.multiple_of(jnp.clip((n - 1) * BLOCK, 0, s_len - 3 * BLOCK), BLOCK)
    win = pl.ds(w0, 3 * BLOCK)
    qb = _rope(p_ref[rows, COL_BQ:COL_BQ + GROUP_W], cosb_ref[rows, :], sbu_ref[rows, :],
               sbd_ref[rows, :], HEAD_D // 4) * (HEAD_D ** -0.5)
    qpos = r0 + lax.broadcasted_iota(jnp.int32, (BLOCK, 3 * BLOCK), 0)
    kpos = w0 + lax.broadcasted_iota(jnp.int32, (BLOCK, 3 * BLOCK), 1)
    valid = jnp.abs(kpos - qpos) <= WINDOW
    for kv in range(2):
        cs = slice(kv * HEAD_D, (kv + 1) * HEAD_D)
        k_loc = kb_s[win, cs]
        v_loc = p_ref[win, COL_BV + kv * HEAD_D:COL_BV + (kv + 1) * HEAD_D].astype(BF16)
        k_ctx = wk_ref[:, cs].astype(BF16)
        v_ctx = wv_ref[:, cs].astype(BF16)
        for g in range(2):
            h = kv * 2 + g
            q = qb[:, h * HEAD_D:(h + 1) * HEAD_D]
            s_loc = jnp.where(valid, _mm_nt(q, k_loc), NEG)
            s_ctx = _mm_nt(q, k_ctx)
            sk = sink_ref[h]
            m = jnp.maximum(jnp.maximum(jnp.max(s_loc, axis=-1, keepdims=True),
                                        jnp.max(s_ctx, axis=-1, keepdims=True)), sk)
            e_loc = jnp.exp(s_loc - m)
            e_ctx = jnp.exp(s_ctx - m)
            den = (jnp.sum(e_loc, axis=-1, keepdims=True) + jnp.sum(e_ctx, axis=-1, keepdims=True)
                   + jnp.exp(sk - m))
            o_ref[:, GROUP_W + h * HEAD_D:GROUP_W + (h + 1) * HEAD_D] = (
                _mm(e_loc, v_loc) + _mm(e_ctx, v_ctx)) / den
    lam = _diff_lambda(lq_ref, lk_ref, lam_init)
    scale = C_SUB ** -0.5
    qc = _rope(p_ref[rows, COL_CQ:COL_CQ + GROUP_W], cosc_ref[rows, :], scu_ref[rows, :],
               scd_ref[rows, :], C_SUB // 4)
    for h in range(4):
        v_ctx = dv_ref[:, h * HEAD_D:(h + 1) * HEAD_D].astype(BF16)
        v_lat = p_ref[:, COL_CV + h * HEAD_D:COL_CV + (h + 1) * HEAD_D].astype(BF16)
        parts = []
        for j in range(2):
            cs = slice(h * HEAD_D + j * C_SUB, h * HEAD_D + (j + 1) * C_SUB)
            q = qc[:, cs]
            s_ctx = _mm_nt(q, dk_ref[:, cs]) * scale
            s_lat = _mm_nt(q, kc_s[:, cs]) * scale
            m = jnp.maximum(jnp.max(s_ctx, axis=-1, keepdims=True), jnp.max(s_lat, axis=-1, keepdims=True))
            e_ctx = jnp.exp(s_ctx - m)
            e_lat = jnp.exp(s_lat - m)
            den = jnp.sum(e_ctx, axis=-1, keepdims=True) + jnp.sum(e_lat, axis=-1, keepdims=True)
            parts.append((_mm(e_ctx, v_ctx) + _mm(e_lat, v_lat)) / den)
        o = parts[0] - lam * parts[1]
        o_ref[:, 2 * GROUP_W + h * HEAD_D:2 * GROUP_W + (h + 1) * HEAD_D] = _sub_norm(o, sg_ref[...], lam_init)
    o_ref[:, 3 * GROUP_W:4 * GROUP_W] = _pool(p_ref[rows, COL_DZ:COL_DZ + GROUP_W],
                                              p_ref[win, COL_DZ:COL_DZ + GROUP_W],
                                              r0, w0, s_len, wbd_ref, ps_ref[...])


def _lat_mix(p, caches, rope_tabs, layer_w, lam_init, layer):
    cw, cb_full, sink, lq, lk, sg, wbd, ps = layer_w
    nb = LAT_LEN // BLOCK
    ctx_blocks = N_CTX_TOK // LAT_LEN
    cache_spec = lambda width: pl.BlockSpec((None, None, CTX_LEN, width), lambda b, n: (b, layer, 0, 0))
    rope_spec = pl.BlockSpec((LAT_LEN, GROUP_W), lambda b, n: (0, 0))
    return pl.pallas_call(
        functools.partial(_lat_mix_kernel, lam_init),
        grid=(N_LAT_SEQ, nb),
        in_specs=[pl.BlockSpec((LAT_LEN, P_COLS), lambda b, n: (ctx_blocks + b, 0)),
                  cache_spec(128), cache_spec(128), cache_spec(256), cache_spec(256)]
                 + [rope_spec] * 6
                 + [_whole(cw.shape), _whole(cb_full.shape), pl.BlockSpec(memory_space=pltpu.SMEM),
                    _whole(lq.shape), _whole(lk.shape), _whole(sg.shape), _whole(wbd.shape),
                    _whole(ps.shape)],
        out_specs=pl.BlockSpec((BLOCK, D_MODEL), lambda b, n: (b * nb + n, 0)),
        out_shape=jax.ShapeDtypeStruct((N_LAT_TOK, D_MODEL), F32),
        scratch_shapes=[pltpu.VMEM((LAT_LEN, 128), BF16), pltpu.VMEM((LAT_LEN, GROUP_W), BF16)],
        compiler_params=pltpu.CompilerParams(vmem_limit_bytes=VMEM_LIMIT,
                                             dimension_semantics=("arbitrary", "arbitrary")),
        name="lat_mix",
    )(p, *caches, *rope_tabs, cw, cb_full, sink, lq, lk, sg, wbd, ps)


def _out_proj_kernel(o_ref, x_ref, mod_ref, wo_ref, g_ref, b_ref, wq_ref, x1_ref, h2_ref, q_ref):
    m = mod_ref[...]
    g1 = m[:, 2 * D_MODEL:3 * D_MODEL]
    sh2 = m[:, 3 * D_MODEL:4 * D_MODEL]
    sc2 = m[:, 4 * D_MODEL:5 * D_MODEL]
    y = jnp.dot(o_ref[...].astype(BF16), wo_ref[...], preferred_element_type=F32)
    x1 = _ln(ALPHA * x_ref[...] + g1 * y) * g_ref[0:1, :] + b_ref[0:1, :]
    x1_ref[...] = x1
    h2 = _ln(x1) * (1.0 + sc2) + sh2
    h2_ref[...] = h2
    q_ref[...] = jnp.dot(h2.astype(BF16), wq_ref[...], preferred_element_type=F32)


def _out_proj(o, x, mods, w_out_bf16, ln_g, ln_b, wq_bf16, layer):
    row_spec = pl.BlockSpec((ROW_BLOCK, D_MODEL), lambda i: (i, 0))
    mat_spec = pl.BlockSpec((None, D_MODEL, D_MODEL), lambda i: (layer, 0, 0))
    vec_spec = pl.BlockSpec((None, 2, D_MODEL), lambda i: (layer, 0, 0))
    shape = jax.ShapeDtypeStruct((N_TOK, D_MODEL), F32)
    return pl.pallas_call(
        _out_proj_kernel,
        grid=(N_ROW_BLOCKS,),
        in_specs=[row_spec, row_spec,
                  pl.BlockSpec((None, None, 1, 6 * D_MODEL), lambda i: (layer, _mod_group(i), 0, 0)),
                  mat_spec, vec_spec, vec_spec, mat_spec],
        out_specs=[row_spec, row_spec, row_spec],
        out_shape=[shape, shape, shape],
        compiler_params=pltpu.CompilerParams(vmem_limit_bytes=VMEM_LIMIT),
        name="out_proj",
    )(o, x, mods, w_out_bf16, ln_g, ln_b, wq_bf16)


def _top16_rows(sc, payload=None):
    r = sc.shape[0]
    row = lax.broadcasted_iota(jnp.int32, sc.shape, 0)
    out_row = lax.broadcasted_iota(jnp.int32, (PK_TOPK, sc.shape[1]), 0)
    vals = jnp.zeros((PK_TOPK, sc.shape[1]), F32)
    sel = jnp.zeros((PK_TOPK, sc.shape[1]), jnp.int32)
    for i in range(PK_TOPK):
        m = jnp.max(sc, axis=0, keepdims=True)
        am = jnp.min(jnp.where(sc == m, row, r), axis=0, keepdims=True)
        hit = row == am
        if payload is None:
            picked = am
        else:
            picked = jnp.sum(jnp.where(hit, payload, 0), axis=0, keepdims=True)
        vals = jnp.where(out_row == i, m, vals)
        sel = jnp.where(out_row == i, picked, sel)
        sc = jnp.where(hit, NEG, sc)
    return vals, sel


def _half_offsets(idx, half):
    local = idx - half * HALF_EXPERTS
    inside = (local >= 0) & (local < HALF_EXPERTS)
    return jnp.where(inside, local, 0) * SUBLANES


def _route_kernel(q_ref, keys_ref, loc0_ref, loc1_ref, idxt_ref, gatet_ref):
    idx_heads = []
    gate_heads = []
    for h in range(PK_HEADS):
        sv = []
        si = []
        for j in range(2):
            c0 = (h * 2 + j) * HEAD_D
            sc = lax.dot_general(keys_ref[j], q_ref[:, c0:c0 + HEAD_D], (((1,), (1,)), ((), ())),
                                 precision=HIGHEST, preferred_element_type=F32)
            v, i = _top16_rows(sc)
            sv.append(v)
            si.append(i)
        cand = jnp.concatenate([sv[0][a:a + 1, :] + sv[1] for a in range(PK_TOPK)], axis=0)
        cidx = jnp.concatenate([si[0][a:a + 1, :] * N_KEYS + si[1] for a in range(PK_TOPK)], axis=0)
        cv, ce = _top16_rows(cand, cidx)
        e = jnp.exp(cv - jnp.max(cv, axis=0, keepdims=True))
        gate_heads.append(e / jnp.sum(e, axis=0, keepdims=True))
        idx_heads.append(ce)
    idx_t = jnp.concatenate(idx_heads, axis=0)
    idxt_ref[...] = idx_t
    gatet_ref[...] = jnp.concatenate(gate_heads, axis=0)
    idx = idx_t.astype(F32).T.astype(jnp.int32)
    loc0_ref[...] = _half_offsets(idx, 0)
    loc1_ref[...] = _half_offsets(idx, 1)


def _route(q, keys):
    row_spec = pl.BlockSpec((ROUTE_BLOCK, N_PICKS), lambda i: (i, 0))
    col_spec = pl.BlockSpec((N_PICKS, ROUTE_BLOCK), lambda i: (0, i))
    return pl.pallas_call(
        _route_kernel,
        grid=(N_TOK // ROUTE_BLOCK,),
        in_specs=[pl.BlockSpec((ROUTE_BLOCK, D_MODEL), lambda i: (i, 0)), _whole(keys.shape)],
        out_specs=[row_spec, row_spec, col_spec, col_spec],
        out_shape=[jax.ShapeDtypeStruct((N_TOK, N_PICKS), jnp.int32),
                   jax.ShapeDtypeStruct((N_TOK, N_PICKS), jnp.int32),
                   jax.ShapeDtypeStruct((N_PICKS, N_TOK), jnp.int32),
                   jax.ShapeDtypeStruct((N_PICKS, N_TOK), F32)],
        compiler_params=pltpu.CompilerParams(vmem_limit_bytes=VMEM_LIMIT),
        name="route",
    )(q, keys)


_MERGE_ORDER = (0, 4, 2, 6, 1, 5, 3, 7)


def _merge8(tiles, sub):
    m4 = sub < 4
    m2 = (sub & 3) < 2
    m1 = (sub & 1) == 0

    def l1(a, b):
        return jnp.where(m4, a, b) + pltpu.roll(jnp.where(m4, b, a), 4, 0)

    def l2(c, d):
        return jnp.where(m2, c, d) + jnp.where(m2, pltpu.roll(c, 6, 0), pltpu.roll(d, 2, 0))

    def l3(e, f):
        return jnp.where(m1, e, f) + jnp.where(m1, pltpu.roll(e, 7, 0), pltpu.roll(f, 1, 0))

    c = [l1(tiles[2 * i], tiles[2 * i + 1]) for i in range(4)]
    return l3(l2(c[0], c[1]), l2(c[2], c[3]))


def _peer_dots(loc_ref, h_ref, tab_ref, actt_ref, gm_a, gm_b):
    sub = lax.broadcasted_iota(jnp.int32, (SUBLANES, LANES), 0)
    lane = lax.broadcasted_iota(jnp.int32, (N_PICKS, PEER_BLOCK), 1)
    actt_ref[...] = jnp.zeros((N_PICKS, PEER_BLOCK), F32)
    gm_b[...] = jnp.zeros((N_PICKS, LANES), F32)

    def partial_sums(t, gm):
        ht = h_ref[pl.ds(pl.multiple_of(t * SUBLANES, SUBLANES), SUBLANES), :]
        for g in range(N_PICKS // SUBLANES):
            prods = []
            for s in range(SUBLANES):
                r = pl.multiple_of(loc_ref[t, g * SUBLANES + _MERGE_ORDER[s]], SUBLANES)
                prods.append(tab_ref[pl.ds(r, SUBLANES), :] * ht)
            gm[g * SUBLANES:(g + 1) * SUBLANES, :] = _merge8(prods, sub)

    def reduce_into_column(t, gm):
        col = jnp.sum(gm[...], axis=1, keepdims=True)
        actt_ref[...] = jnp.where(lane == t, col, actt_ref[...])

    def token_pair(k, carry):
        t = 2 * k
        reduce_into_column(t - 1, gm_b)
        partial_sums(t, gm_a)
        reduce_into_column(t, gm_a)
        partial_sums(t + 1, gm_b)
        return carry

    lax.fori_loop(0, PEER_BLOCK // 2, token_pair, 0)
    reduce_into_column(PEER_BLOCK - 1, gm_b)


def _peer_u_first_kernel(loc_ref, h_ref, tab_ref, actt_ref, gm_a, gm_b):
    _peer_dots(loc_ref, h_ref, tab_ref, actt_ref, gm_a, gm_b)


def _peer_u_second_kernel(loc_ref, h_ref, tab_ref, act0t_ref, idxt_ref, gatet_ref, w0_ref, w1_ref,
                          actt_s, gm_a, gm_b):
    _peer_dots(loc_ref, h_ref, tab_ref, actt_s, gm_a, gm_b)
    in0 = idxt_ref[...] < HALF_EXPERTS
    w = gatet_ref[...] * _gelu(jnp.where(in0, act0t_ref[...], actt_s[...]))
    w0_ref[...] = jnp.where(in0, w, 0.0).T
    w1_ref[...] = jnp.where(in0, 0.0, w).T


def _peer_sum(loc_ref, w_ref, tab_ref, init, f_ref, wb_a, wb_b):
    def splat_weights(t, wb):
        wb[...] = jnp.broadcast_to(w_ref[pl.ds(t, 1), :], (N_PICKS, N_PICKS)).T

    def weighted_sum(t, wb):
        rows = pl.ds(pl.multiple_of(t * SUBLANES, SUBLANES), SUBLANES)
        accs = [init(rows)] + [jnp.zeros((SUBLANES, LANES), F32)] * 3
        for e in range(N_PICKS):
            r = pl.multiple_of(loc_ref[t, e], SUBLANES)
            w = jnp.broadcast_to(wb[e:e + 1, :], (SUBLANES, LANES))
            accs[e % 4] = accs[e % 4] + tab_ref[pl.ds(r, SUBLANES), :] * w
        f_ref[rows, :] = (accs[0] + accs[1]) + (accs[2] + accs[3])

    splat_weights(0, wb_a)

    def token_pair(k, carry):
        t = 2 * k
        splat_weights(t + 1, wb_b)
        weighted_sum(t, wb_a)
        splat_weights(jnp.minimum(t + 2, PEER_BLOCK - 1), wb_a)
        weighted_sum(t + 1, wb_b)
        return carry

    lax.fori_loop(0, PEER_BLOCK // 2, token_pair, 0)


def _peer_v_first_kernel(loc_ref, w_ref, tab_ref, f_ref, wb_a, wb_b):
    _peer_sum(loc_ref, w_ref, tab_ref, lambda rows: jnp.zeros((SUBLANES, LANES), F32), f_ref, wb_a, wb_b)


def _peer_v_second_kernel(loc_ref, w_ref, tab_ref, f0_ref, f_ref, wb_a, wb_b):
    _peer_sum(loc_ref, w_ref, tab_ref, lambda rows: f0_ref[rows, :], f_ref, wb_a, wb_b)


def _peer(h2, loc0, loc1, idx_t, gate_t, u_tab, v_tab, layer):
    nblk = N_TOK // PEER_BLOCK
    h_tiles = h2.reshape(N_TOK * SUBLANES, LANES)
    smem_spec = pl.BlockSpec((PEER_BLOCK, N_PICKS), lambda i: (i, 0), memory_space=pltpu.SMEM)
    row_spec = pl.BlockSpec((PEER_BLOCK, N_PICKS), lambda i: (i, 0))
    col_spec = pl.BlockSpec((N_PICKS, PEER_BLOCK), lambda i: (0, i))
    tile_spec = pl.BlockSpec((PEER_BLOCK * SUBLANES, LANES), lambda i: (i, 0))
    half_spec = lambda half: pl.BlockSpec((None, HALF_EXPERTS * SUBLANES, LANES),
                                          lambda i: (layer, half, 0), pipeline_mode=pl.Buffered(1))
    row_shape = jax.ShapeDtypeStruct((N_TOK, N_PICKS), F32)
    col_shape = jax.ShapeDtypeStruct((N_PICKS, N_TOK), F32)
    tile_shape = jax.ShapeDtypeStruct((N_TOK * SUBLANES, LANES), F32)
    params = pltpu.CompilerParams(vmem_limit_bytes=VMEM_LIMIT)
    pair_scratch = [pltpu.VMEM((N_PICKS, LANES), F32)] * 2

    act0_t = pl.pallas_call(
        _peer_u_first_kernel, grid=(nblk,),
        in_specs=[smem_spec, tile_spec, half_spec(0)],
        out_specs=col_spec, out_shape=col_shape, scratch_shapes=pair_scratch,
        compiler_params=params, name="peer_u0",
    )(loc0, h_tiles, u_tab)
    w0, w1 = pl.pallas_call(
        _peer_u_second_kernel, grid=(nblk,),
        in_specs=[smem_spec, tile_spec, half_spec(1), col_spec, col_spec, col_spec],
        out_specs=[row_spec, row_spec], out_shape=[row_shape, row_shape],
        scratch_shapes=[pltpu.VMEM((N_PICKS, PEER_BLOCK), F32)] + pair_scratch,
        compiler_params=params, name="peer_u1",
    )(loc1, h_tiles, u_tab, act0_t, idx_t, gate_t)
    f0 = pl.pallas_call(
        _peer_v_first_kernel, grid=(nblk,),
        in_specs=[smem_spec, row_spec, half_spec(0)],
        out_specs=tile_spec, out_shape=tile_shape, scratch_shapes=pair_scratch,
        compiler_params=params, name="peer_v0",
    )(loc0, w0, v_tab)
    f = pl.pallas_call(
        _peer_v_second_kernel, grid=(nblk,),
        in_specs=[smem_spec, row_spec, half_spec(1), tile_spec],
        out_specs=tile_spec, out_shape=tile_shape, scratch_shapes=pair_scratch,
        compiler_params=params, name="peer_v1",
    )(loc1, w1, v_tab, f0)
    return f.reshape(N_TOK, D_MODEL)


def _close_kernel(x1_ref, f_ref, mod_ref, g_ref, b_ref, o_ref):
    g2 = mod_ref[...][:, 5 * D_MODEL:6 * D_MODEL]
    o_ref[...] = _ln(ALPHA * x1_ref[...] + g2 * f_ref[...]) * g_ref[1:2, :] + b_ref[1:2, :]


def _close(x1, f, mods, ln_g, ln_b, layer):
    row_spec = pl.BlockSpec((ROW_BLOCK, D_MODEL), lambda i: (i, 0))
    vec_spec = pl.BlockSpec((None, 2, D_MODEL), lambda i: (layer, 0, 0))
    return pl.pallas_call(
        _close_kernel,
        grid=(N_ROW_BLOCKS,),
        in_specs=[row_spec, row_spec,
                  pl.BlockSpec((None, None, 1, 6 * D_MODEL), lambda i: (layer, _mod_group(i), 0, 0)),
                  vec_spec, vec_spec],
        out_specs=row_spec,
        out_shape=jax.ShapeDtypeStruct((N_TOK, D_MODEL), F32),
        compiler_params=pltpu.CompilerParams(vmem_limit_bytes=VMEM_LIMIT),
        name="close",
    )(x1, f, mods, ln_g, ln_b)


def _rope_tables(dim, copies):
    rows = LAT_LEN // GRID_W
    row = jnp.repeat(jnp.arange(rows, dtype=F32), GRID_W)
    col = jnp.tile(jnp.arange(GRID_W, dtype=F32), rows)
    nf = dim // 4
    inv = ROPE_BASE ** (-jnp.arange(nf, dtype=F32) / nf)
    ar = row[:, None] * inv
    ac = col[:, None] * inv
    ang = jnp.concatenate([ar, ar, ac, ac], axis=-1)
    cos, sin = jnp.cos(ang), jnp.sin(ang)
    even_quarter = (jnp.arange(dim) // nf) % 2 == 0
    sin_up = jnp.where(even_quarter, -sin, 0.0)
    sin_dn = jnp.where(even_quarter, 0.0, sin)
    return tuple(jnp.tile(t, (1, copies)) for t in (cos, sin_up, sin_dn))


def kernel(x_prompt, x_sample, c, cache_win_k, cache_win_v, cache_diff_k, cache_diff_v, c_ctx, w_mod, b_mod, w_in, w_out, chunk_w, chunk_b, win_sink, diff_lam_q, diff_lam_k, diff_subln_g, pool_w, pool_scale, ln_g, ln_b, peer_wq, peer_keys, peer_u, peer_v):
    cvec = jnp.concatenate([c_ctx[None, :], c, jnp.zeros((8 - 1 - N_LAT_SEQ, D_MODEL), F32)], axis=0)
    mods = _modulation(cvec, w_mod, b_mod)[:, :1 + N_LAT_SEQ].reshape(DEPTH, 1 + N_LAT_SEQ, 1, 6 * D_MODEL)

    w_in_b = w_in.astype(BF16)
    w_out_b = w_out.astype(BF16)
    wq_b = peer_wq.astype(BF16)
    cw_b = chunk_w.astype(BF16)
    cb_full = jnp.repeat(jnp.swapaxes(chunk_b, 1, 2), HEAD_D, axis=2)
    eye = jnp.eye(4, dtype=F32)
    wbd = jnp.einsum('lgcd,gh->lgchd', pool_w, eye).reshape(DEPTH, GROUP_W, GROUP_W).astype(BF16)
    caches = (cache_win_k.reshape(N_LAT_SEQ, DEPTH, CTX_LEN, 128),
              cache_win_v.reshape(N_LAT_SEQ, DEPTH, CTX_LEN, 128),
              cache_diff_k.reshape(N_LAT_SEQ, DEPTH, CTX_LEN, GROUP_W),
              cache_diff_v.reshape(N_LAT_SEQ, DEPTH, CTX_LEN, GROUP_W))
    rope_tabs = _rope_tables(HEAD_D, GROUP_W // HEAD_D) + _rope_tables(C_SUB, GROUP_W // C_SUB)
    u_tiles = peer_u.reshape(DEPTH, N_EXPERTS * SUBLANES, LANES)
    v_tiles = peer_v.reshape(DEPTH, N_EXPERTS * SUBLANES, LANES)

    x = jnp.concatenate([x_prompt.reshape(N_CTX_TOK, D_MODEL), x_sample.reshape(N_LAT_TOK, D_MODEL)], axis=0)
    kbs, vbs, kcs, vcs = [], [], [], []
    for l in range(DEPTH):
        lam_init = 0.8 - 0.6 * math.exp(-0.3 * l)
        layer_w = (cw_b[l], cb_full[l], win_sink[l], diff_lam_q[l], diff_lam_k[l],
                   diff_subln_g[l].reshape(1, HEAD_D), wbd[l], pool_scale[l].reshape(1, GROUP_W))
        p = _in_proj(x, mods, w_in_b, l)
        pc = p[:N_CTX_TOK].reshape(N_CTX_SEQ, CTX_LEN, P_COLS)
        kbs.append(pc[..., COL_BK:COL_BV].reshape(N_CTX_SEQ, CTX_LEN, 2, HEAD_D))
        vbs.append(pc[..., COL_BV:COL_CQ].reshape(N_CTX_SEQ, CTX_LEN, 2, HEAD_D))
        kcs.append(pc[..., COL_CK:COL_CV].reshape(N_CTX_SEQ, CTX_LEN, 4, 2, C_SUB))
        vcs.append(pc[..., COL_CV:COL_DZ].reshape(N_CTX_SEQ, CTX_LEN, 4, HEAD_D))
        o = jnp.concatenate([_ctx_mix(p, layer_w, lam_init),
                             _lat_mix(p, caches, rope_tabs, layer_w, lam_init, l)], axis=0)
        x1, h2, q = _out_proj(o, x, mods, w_out_b, ln_g, ln_b, wq_b, l)
        loc0, loc1, idx_t, gate_t = _route(q, peer_keys[l])
        f = _peer(h2, loc0, loc1, idx_t, gate_t, u_tiles, v_tiles, l)
        x = _close(x1, f, mods, ln_g, ln_b, l)
    return (x[:N_CTX_TOK].reshape(N_CTX_SEQ, CTX_LEN, D_MODEL),
            x[N_CTX_TOK:].reshape(N_LAT_SEQ, LAT_LEN, D_MODEL),
            jnp.stack(kbs, axis=1), jnp.stack(vbs, axis=1), jnp.stack(kcs, axis=1), jnp.stack(vcs, axis=1))
```

```python
import functools
import math

import jax
import jax.numpy as jnp
import numpy as np
from jax import lax
from jax.experimental import pallas as pl
from jax.experimental.pallas import tpu as pltpu

F32 = jnp.float32
BF16 = jnp.bfloat16
HIGHEST = lax.Precision.HIGHEST

D_MODEL = 1024
DEPTH = 4
N_CTX_SEQ = 16
CTX_LEN = 256
N_LAT_SEQ = 2
LAT_LEN = 1024
N_CTX_TOK = N_CTX_SEQ * CTX_LEN
N_LAT_TOK = N_LAT_SEQ * LAT_LEN
N_TOK = N_CTX_TOK + N_LAT_TOK
GRID_W = 64
BLOCK = 128
GROUP_W = 256
HEAD_D = 64
C_SUB = 32
WINDOW = 128
POOL_SIZES = (2, 4, 8, 16)
P_COLS = 2048
COL_AU, COL_AV, COL_BQ, COL_BK, COL_BV, COL_CQ, COL_CK, COL_CV, COL_DZ = (
    0, 256, 512, 768, 896, 1024, 1280, 1536, 1792)
PK_HEADS = 8
N_KEYS = 128
N_EXPERTS = N_KEYS * N_KEYS
PK_TOPK = 16
N_PICKS = PK_HEADS * PK_TOPK
ROPE_BASE = 10000.0
ALPHA = (2 * DEPTH) ** 0.25
LN_EPS = 1e-5
NEG = -1e30

ROW_BLOCK = 512
N_ROW_BLOCKS = N_TOK // ROW_BLOCK
N_CTX_ROW_BLOCKS = N_CTX_TOK // ROW_BLOCK
ROW_BLOCKS_PER_LAT_SEQ = LAT_LEN // ROW_BLOCK
ROUTE_BLOCK = 256
PEER_BLOCK = 128
SUBLANES = 8
LANES = 128
HALF_EXPERTS = N_EXPERTS // 2
VMEM_LIMIT = 56 * 1024 * 1024


def _mm(a, b):
    return jnp.dot(a.astype(BF16), b.astype(BF16), preferred_element_type=F32)


def _mm_nt(a, b):
    return lax.dot_general(a.astype(BF16), b.astype(BF16), (((1,), (1,)), ((), ())),
                           preferred_element_type=F32)


def _gelu(x):
    return x * (0.5 * (1.0 + jnp.tanh(0.7978845608028654 * (x + 0.044715 * (x * x * x)))))


def _ln(x):
    mu = jnp.mean(x, axis=-1, keepdims=True)
    xc = x - mu
    var = jnp.mean(xc * xc, axis=-1, keepdims=True)
    return xc * lax.rsqrt(var + LN_EPS)


def _mod_group(i):
    return jnp.where(i < N_CTX_ROW_BLOCKS, 0, 1 + (i - N_CTX_ROW_BLOCKS) // ROW_BLOCKS_PER_LAT_SEQ)


def _mod_kernel(c_ref, w_ref, b_ref, o_ref):
    c = c_ref[...]
    s = c * jax.nn.sigmoid(c)
    o_ref[...] = jnp.dot(s, w_ref[...], precision=HIGHEST, preferred_element_type=F32) + b_ref[...]


def _modulation(cvec, w_mod, b_mod):
    nj = 4
    cols = 6 * D_MODEL // nj
    return pl.pallas_call(
        _mod_kernel,
        grid=(DEPTH, nj),
        in_specs=[pl.BlockSpec((8, D_MODEL), lambda l, j: (0, 0)),
                  pl.BlockSpec((None, D_MODEL, cols), lambda l, j: (l, 0, j)),
                  pl.BlockSpec((None, 1, cols), lambda l, j: (l, 0, j))],
        out_specs=pl.BlockSpec((None, 8, cols), lambda l, j: (l, 0, j)),
        out_shape=jax.ShapeDtypeStruct((DEPTH, 8, 6 * D_MODEL), F32),
        compiler_params=pltpu.CompilerParams(vmem_limit_bytes=VMEM_LIMIT),
        name="modulation",
    )(cvec, w_mod, b_mod.reshape(DEPTH, 1, 6 * D_MODEL))


def _in_proj_kernel(x_ref, mod_ref, w_ref, o_ref):
    m = mod_ref[...]
    h = _ln(x_ref[...]) * (1.0 + m[:, D_MODEL:2 * D_MODEL]) + m[:, 0:D_MODEL]
    o_ref[...] = jnp.dot(h.astype(BF16), w_ref[...], preferred_element_type=F32)


def _in_proj(x, mods, w_in_bf16, layer):
    return pl.pallas_call(
        _in_proj_kernel,
        grid=(N_ROW_BLOCKS,),
        in_specs=[pl.BlockSpec((ROW_BLOCK, D_MODEL), lambda i: (i, 0)),
                  pl.BlockSpec((None, None, 1, 6 * D_MODEL), lambda i: (layer, _mod_group(i), 0, 0)),
                  pl.BlockSpec((None, D_MODEL, P_COLS), lambda i: (layer, 0, 0))],
        out_specs=pl.BlockSpec((ROW_BLOCK, P_COLS), lambda i: (i, 0)),
        out_shape=jax.ShapeDtypeStruct((N_TOK, P_COLS), F32),
        compiler_params=pltpu.CompilerParams(vmem_limit_bytes=VMEM_LIMIT),
        name="in_proj",
    )(x, mods, w_in_bf16)


def _chunk_gating(u_raw, v_raw, cw_ref, cb_full):
    u = _gelu(u_raw)
    v = _gelu(v_raw).astype(BF16)
    head = lax.broadcasted_iota(jnp.int32, (BLOCK, GROUP_W), 1) // HEAD_D
    mixed = cb_full
    for h in range(4):
        mixed = mixed + jnp.where(head == h, jnp.dot(cw_ref[h], v, preferred_element_type=F32), 0.0)
    return u * mixed


def _pool(z_rows, z_win, row0, key0, seq_len, wbd_ref, pool_scale):
    r, k = z_rows.shape[0], z_win.shape[0]
    t = row0 + lax.broadcasted_iota(jnp.int32, (r, k), 0)
    pos = key0 + lax.broadcasted_iota(jnp.int32, (r, k), 1)
    t1 = row0 + lax.broadcasted_iota(jnp.int32, (r, 1), 0)
    z_hi = z_win.astype(BF16)
    z_lo = (z_win - z_hi.astype(F32)).astype(BF16)
    group = lax.broadcasted_iota(jnp.int32, (r, GROUP_W), 1) // HEAD_D
    mean = jnp.zeros((r, GROUP_W), F32)
    for g, w in enumerate(POOL_SIZES):
        lo = jnp.maximum(t - w // 2, 0)
        hi = jnp.minimum(t + w // 2, seq_len)
        band = jnp.where((pos >= lo) & (pos < hi), 1.0, 0.0).astype(BF16)
        tot = (jnp.dot(band, z_hi, preferred_element_type=F32)
               + jnp.dot(band, z_lo, preferred_element_type=F32))
        cnt = (jnp.minimum(t1 + w // 2, seq_len) - jnp.maximum(t1 - w // 2, 0)).astype(F32)
        mean = mean + jnp.where(group == g, tot / cnt, 0.0)
    pooled = mean - z_rows
    return jnp.dot(pooled.astype(BF16), wbd_ref[...], preferred_element_type=F32) * pool_scale


def _diff_lambda(lq_ref, lk_ref, lam_init):
    lq = lq_ref[...]
    lk = lk_ref[...]
    s0 = jnp.sum(lq[0:1] * lk[0:1], axis=-1, keepdims=True)
    s1 = jnp.sum(lq[1:2] * lk[1:2], axis=-1, keepdims=True)
    return jnp.exp(s0) - jnp.exp(s1) + lam_init


def _sub_norm(o, g, lam_init):
    ms = jnp.mean(o * o, axis=-1, keepdims=True)
    return o * lax.rsqrt(ms + LN_EPS) * g * (1.0 - lam_init)


def _ctx_mix_kernel(lam_init, p_ref, cw_ref, cb_ref, sink_ref, lq_ref, lk_ref, sg_ref, wbd_ref,
                    ps_ref, o_ref):
    s_len = CTX_LEN
    for n in range(s_len // BLOCK):
        rows = slice(n * BLOCK, (n + 1) * BLOCK)
        o_ref[rows, 0:GROUP_W] = _chunk_gating(p_ref[rows, COL_AU:COL_AU + GROUP_W],
                                               p_ref[rows, COL_AV:COL_AV + GROUP_W], cw_ref, cb_ref[...])
    for kv in range(2):
        kk = p_ref[:, COL_BK + kv * HEAD_D:COL_BK + (kv + 1) * HEAD_D].astype(BF16)
        vv = p_ref[:, COL_BV + kv * HEAD_D:COL_BV + (kv + 1) * HEAD_D].astype(BF16)
        for g in range(2):
            h = kv * 2 + g
            q = p_ref[:, COL_BQ + h * HEAD_D:COL_BQ + (h + 1) * HEAD_D] * (HEAD_D ** -0.5)
            s = _mm_nt(q, kk)
            sk = sink_ref[h]
            m = jnp.maximum(jnp.max(s, axis=-1, keepdims=True), sk)
            e = jnp.exp(s - m)
            den = jnp.sum(e, axis=-1, keepdims=True) + jnp.exp(sk - m)
            o_ref[:, GROUP_W + h * HEAD_D:GROUP_W + (h + 1) * HEAD_D] = _mm(e, vv) / den
    lam = _diff_lambda(lq_ref, lk_ref, lam_init)
    scale = C_SUB ** -0.5
    for h in range(4):
        vv = p_ref[:, COL_CV + h * HEAD_D:COL_CV + (h + 1) * HEAD_D].astype(BF16)
        parts = []
        for j in range(2):
            c0 = h * HEAD_D + j * C_SUB
            s = _mm_nt(p_ref[:, COL_CQ + c0:COL_CQ + c0 + C_SUB],
                       p_ref[:, COL_CK + c0:COL_CK + c0 + C_SUB]) * scale
            e = jnp.exp(s - jnp.max(s, axis=-1, keepdims=True))
            parts.append(_mm(e, vv) / jnp.sum(e, axis=-1, keepdims=True))
        o = parts[0] - lam * parts[1]
        o_ref[:, 2 * GROUP_W + h * HEAD_D:2 * GROUP_W + (h + 1) * HEAD_D] = _sub_norm(o, sg_ref[...], lam_init)
    z = p_ref[:, COL_DZ:COL_DZ + GROUP_W]
    o_ref[:, 3 * GROUP_W:4 * GROUP_W] = _pool(z, z, 0, 0, s_len, wbd_ref, ps_ref[...])


def _whole(shape):
    nd = len(shape)
    return pl.BlockSpec(shape, lambda *_: (0,) * nd)


def _ctx_mix(p, layer_w, lam_init):
    cw, cb_full, sink, lq, lk, sg, wbd, ps = layer_w
    return pl.pallas_call(
        functools.partial(_ctx_mix_kernel, lam_init),
        grid=(N_CTX_SEQ,),
        in_specs=[pl.BlockSpec((CTX_LEN, P_COLS), lambda b: (b, 0)),
                  _whole(cw.shape), _whole(cb_full.shape),
                  pl.BlockSpec(memory_space=pltpu.SMEM),
                  _whole(lq.shape), _whole(lk.shape), _whole(sg.shape), _whole(wbd.shape),
                  _whole(ps.shape)],
        out_specs=pl.BlockSpec((CTX_LEN, D_MODEL), lambda b: (b, 0)),
        out_shape=jax.ShapeDtypeStruct((N_CTX_TOK, D_MODEL), F32),
        compiler_params=pltpu.CompilerParams(vmem_limit_bytes=VMEM_LIMIT),
        name="ctx_mix",
    )(p, cw, cb_full, sink, lq, lk, sg, wbd, ps)


def _rope(x, cos, sin_up, sin_dn, shift):
    outs = []
    for c0 in range(0, x.shape[1], LANES):
        xs = x[:, c0:c0 + LANES]
        up = pltpu.roll(xs, LANES - shift, 1)
        dn = pltpu.roll(xs, shift, 1)
        outs.append(xs * cos[:, c0:c0 + LANES] + up * sin_up[:, c0:c0 + LANES]
                    + dn * sin_dn[:, c0:c0 + LANES])
    return outs[0] if len(outs) == 1 else jnp.concatenate(outs, axis=1)


def _lat_mix_kernel(lam_init, p_ref, wk_ref, wv_ref, dk_ref, dv_ref,
                    cosb_ref, sbu_ref, sbd_ref, cosc_ref, scu_ref, scd_ref,
                    cw_ref, cb_ref, sink_ref, lq_ref, lk_ref, sg_ref, wbd_ref, ps_ref,
                    o_ref, kb_s, kc_s):
    n = pl.program_id(1)
    s_len = LAT_LEN

    @pl.when(n == 0)
    def _():
        kb_s[...] = _rope(p_ref[:, COL_BK:COL_BK + 128], cosb_ref[:, 0:128], sbu_ref[:, 0:128],
                          sbd_ref[:, 0:128], HEAD_D // 4).astype(BF16)
        kc_s[...] = _rope(p_ref[:, COL_CK:COL_CK + GROUP_W], cosc_ref[...], scu_ref[...],
                          scd_ref[...], C_SUB // 4).astype(BF16)

    r0 = pl.multiple_of(n * BLOCK, BLOCK)
    rows = pl.ds(r0, BLOCK)
    o_ref[:, 0:GROUP_W] = _chunk_gating(p_ref[rows, COL_AU:COL_AU + GROUP_W],
                                        p_ref[rows, COL_AV:COL_AV + GROUP_W], cw_ref, cb_ref[...])
    w0 = pl.multiple_of(jnp.clip((n - 1) * BLOCK, 0, s_len - 3 * BLOCK), BLOCK)
    win = pl.ds(w0, 3 * BLOCK)
    qb = _rope(p_ref[rows, COL_BQ:COL_BQ + GROUP_W], cosb_ref[rows, :], sbu_ref[rows, :],
               sbd_ref[rows, :], HEAD_D // 4) * (HEAD_D ** -0.5)
    qpos = r0 + lax.broadcasted_iota(jnp.int32, (BLOCK, 3 * BLOCK), 0)
    kpos = w0 + lax.broadcasted_iota(jnp.int32, (BLOCK, 3 * BLOCK), 1)
    valid = jnp.abs(kpos - qpos) <= WINDOW
    for kv in range(2):
        cs = slice(kv * HEAD_D, (kv + 1) * HEAD_D)
        k_loc = kb_s[win, cs]
        v_loc = p_ref[win, COL_BV + kv * HEAD_D:COL_BV + (kv + 1) * HEAD_D].astype(BF16)
        k_ctx = wk_ref[:, cs].astype(BF16)
        v_ctx = wv_ref[:, cs].astype(BF16)
        for g in range(2):
            h = kv * 2 + g
            q = qb[:, h * HEAD_D:(h + 1) * HEAD_D]
            s_loc = jnp.where(valid, _mm_nt(q, k_loc), NEG)
            s_ctx = _mm_nt(q, k_ctx)
            sk = sink_ref[h]
            m = jnp.maximum(jnp.maximum(jnp.max(s_loc, axis=-1, keepdims=True),
                                        jnp.max(s_ctx, axis=-1, keepdims=True)), sk)
            e_loc = jnp.exp(s_loc - m)
            e_ctx = jnp.exp(s_ctx - m)
            den = (jnp.sum(e_loc, axis=-1, keepdims=True) + jnp.sum(e_ctx, axis=-1, keepdims=True)
                   + jnp.exp(sk - m))
            o_ref[:, GROUP_W + h * HEAD_D:GROUP_W + (h + 1) * HEAD_D] = (
                _mm(e_loc, v_loc) + _mm(e_ctx, v_ctx)) / den
    lam = _diff_lambda(lq_ref, lk_ref, lam_init)
    scale = C_SUB ** -0.5
    qc = _rope(p_ref[rows, COL_CQ:COL_CQ + GROUP_W], cosc_ref[rows, :], scu_ref[rows, :],
               scd_ref[rows, :], C_SUB // 4)
    for h in range(4):
        v_ctx = dv_ref[:, h * HEAD_D:(h + 1) * HEAD_D].astype(BF16)
        v_lat = p_ref[:, COL_CV + h * HEAD_D:COL_CV + (h + 1) * HEAD_D].astype(BF16)
        parts = []
        for j in range(2):
            cs = slice(h * HEAD_D + j * C_SUB, h * HEAD_D + (j + 1) * C_SUB)
            q = qc[:, cs]
            s_ctx = _mm_nt(q, dk_ref[:, cs]) * scale
            s_lat = _mm_nt(q, kc_s[:, cs]) * scale
            m = jnp.maximum(jnp.max(s_ctx, axis=-1, keepdims=True), jnp.max(s_lat, axis=-1, keepdims=True))
            e_ctx = jnp.exp(s_ctx - m)
            e_lat = jnp.exp(s_lat - m)
            den = jnp.sum(e_ctx, axis=-1, keepdims=True) + jnp.sum(e_lat, axis=-1, keepdims=True)
            parts.append((_mm(e_ctx, v_ctx) + _mm(e_lat, v_lat)) / den)
        o = parts[0] - lam * parts[1]
        o_ref[:, 2 * GROUP_W + h * HEAD_D:2 * GROUP_W + (h + 1) * HEAD_D] = _sub_norm(o, sg_ref[...], lam_init)
    o_ref[:, 3 * GROUP_W:4 * GROUP_W] = _pool(p_ref[rows, COL_DZ:COL_DZ + GROUP_W],
                                              p_ref[win, COL_DZ:COL_DZ + GROUP_W],
                                              r0, w0, s_len, wbd_ref, ps_ref[...])


def _lat_mix(p, caches, rope_tabs, layer_w, lam_init, layer):
    cw, cb_full, sink, lq, lk, sg, wbd, ps = layer_w
    nb = LAT_LEN // BLOCK
    ctx_blocks = N_CTX_TOK // LAT_LEN
    cache_spec = lambda width: pl.BlockSpec((None, None, CTX_LEN, width), lambda b, n: (b, layer, 0, 0))
    rope_spec = pl.BlockSpec((LAT_LEN, GROUP_W), lambda b, n: (0, 0))
    return pl.pallas_call(
        functools.partial(_lat_mix_kernel, lam_init),
        grid=(N_LAT_SEQ, nb),
        in_specs=[pl.BlockSpec((LAT_LEN, P_COLS), lambda b, n: (ctx_blocks + b, 0)),
                  cache_spec(128), cache_spec(128), cache_spec(256), cache_spec(256)]
                 + [rope_spec] * 6
                 + [_whole(cw.shape), _whole(cb_full.shape), pl.BlockSpec(memory_space=pltpu.SMEM),
                    _whole(lq.shape), _whole(lk.shape), _whole(sg.shape), _whole(wbd.shape),
                    _whole(ps.shape)],
        out_specs=pl.BlockSpec((BLOCK, D_MODEL), lambda b, n: (b * nb + n, 0)),
        out_shape=jax.ShapeDtypeStruct((N_LAT_TOK, D_MODEL), F32),
        scratch_shapes=[pltpu.VMEM((LAT_LEN, 128), BF16), pltpu.VMEM((LAT_LEN, GROUP_W), BF16)],
        compiler_params=pltpu.CompilerParams(vmem_limit_bytes=VMEM_LIMIT,
                                             dimension_semantics=("arbitrary", "arbitrary")),
        name="lat_mix",
    )(p, *caches, *rope_tabs, cw, cb_full, sink, lq, lk, sg, wbd, ps)


def _out_proj_kernel(o_ref, x_ref, mod_ref, wo_ref, g_ref, b_ref, wq_ref, x1_ref, h2_ref, q_ref):
    m = mod_ref[...]
    g1 = m[:, 2 * D_MODEL:3 * D_MODEL]
    sh2 = m[:, 3 * D_MODEL:4 * D_MODEL]
    sc2 = m[:, 4 * D_MODEL:5 * D_MODEL]
    y = jnp.dot(o_ref[...].astype(BF16), wo_ref[...], preferred_element_type=F32)
    x1 = _ln(ALPHA * x_ref[...] + g1 * y) * g_ref[0:1, :] + b_ref[0:1, :]
    x1_ref[...] = x1
    h2 = _ln(x1) * (1.0 + sc2) + sh2
    h2_ref[...] = h2
    q_ref[...] = jnp.dot(h2.astype(BF16), wq_ref[...], preferred_element_type=F32)


def _out_proj(o, x, mods, w_out_bf16, ln_g, ln_b, wq_bf16, layer):
    row_spec = pl.BlockSpec((ROW_BLOCK, D_MODEL), lambda i: (i, 0))
    mat_spec = pl.BlockSpec((None, D_MODEL, D_MODEL), lambda i: (layer, 0, 0))
    vec_spec = pl.BlockSpec((None, 2, D_MODEL), lambda i: (layer, 0, 0))
    shape = jax.ShapeDtypeStruct((N_TOK, D_MODEL), F32)
    return pl.pallas_call(
        _out_proj_kernel,
        grid=(N_ROW_BLOCKS,),
        in_specs=[row_spec, row_spec,
                  pl.BlockSpec((None, None, 1, 6 * D_MODEL), lambda i: (layer, _mod_group(i), 0, 0)),
                  mat_spec, vec_spec, vec_spec, mat_spec],
        out_specs=[row_spec, row_spec, row_spec],
        out_shape=[shape, shape, shape],
        compiler_params=pltpu.CompilerParams(vmem_limit_bytes=VMEM_LIMIT),
        name="out_proj",
    )(o, x, mods, w_out_bf16, ln_g, ln_b, wq_bf16)


def _top16_rows(sc, payload=None):
    r = sc.shape[0]
    row = lax.broadcasted_iota(jnp.int32, sc.shape, 0)
    out_row = lax.broadcasted_iota(jnp.int32, (PK_TOPK, sc.shape[1]), 0)
    vals = jnp.zeros((PK_TOPK, sc.shape[1]), F32)
    sel = jnp.zeros((PK_TOPK, sc.shape[1]), jnp.int32)
    for i in range(PK_TOPK):
        m = jnp.max(sc, axis=0, keepdims=True)
        am = jnp.min(jnp.where(sc == m, row, r), axis=0, keepdims=True)
        hit = row == am
        if payload is None:
            picked = am
        else:
            picked = jnp.sum(jnp.where(hit, payload, 0), axis=0, keepdims=True)
        vals = jnp.where(out_row == i, m, vals)
        sel = jnp.where(out_row == i, picked, sel)
        sc = jnp.where(hit, NEG, sc)
    return vals, sel


def _half_offsets(idx, half):
    local = idx - half * HALF_EXPERTS
    inside = (local >= 0) & (local < HALF_EXPERTS)
    return jnp.where(inside, local, 0) * SUBLANES


def _route_kernel(q_ref, keys_ref, loc0_ref, loc1_ref, idxt_ref, gatet_ref, count0_ref):
    idx_heads = []
    gate_heads = []
    for h in range(PK_HEADS):
        sv = []
        si = []
        for j in range(2):
            c0 = (h * 2 + j) * HEAD_D
            sc = lax.dot_general(keys_ref[j], q_ref[:, c0:c0 + HEAD_D], (((1,), (1,)), ((), ())),
                                 precision=HIGHEST, preferred_element_type=F32)
            v, i = _top16_rows(sc)
            sv.append(v)
            si.append(i)
        cand = jnp.concatenate([sv[0][a:a + 1, :] + sv[1] for a in range(PK_TOPK)], axis=0)
        cidx = jnp.concatenate([si[0][a:a + 1, :] * N_KEYS + si[1] for a in range(PK_TOPK)], axis=0)
        cv, ce = _top16_rows(cand, cidx)
        e = jnp.exp(cv - jnp.max(cv, axis=0, keepdims=True))
        gate_heads.append(e / jnp.sum(e, axis=0, keepdims=True))
        idx_heads.append(ce)
    idx_t = jnp.concatenate(idx_heads, axis=0)
    gate_t = jnp.concatenate(gate_heads, axis=0)
    n_tok = idx_t.shape[1]
    in0 = idx_t < HALF_EXPERTS
    before = (lax.broadcasted_iota(jnp.int32, (N_PICKS, N_PICKS), 1)
              < lax.broadcasted_iota(jnp.int32, (N_PICKS, N_PICKS), 0))
    rank0 = jnp.dot(jnp.where(before, 1.0, 0.0).astype(BF16), jnp.where(in0, 1.0, 0.0).astype(BF16),
                    preferred_element_type=F32).astype(jnp.int32)
    count0 = jnp.sum(jnp.where(in0, 1, 0), axis=0, keepdims=True)
    pick = lax.broadcasted_iota(jnp.int32, (N_PICKS, n_tok), 0)
    slot = jnp.where(in0, rank0, count0 + pick - rank0)
    for d in range(N_PICKS):
        hit = slot == d
        idxt_ref[d:d + 1, :] = jnp.sum(jnp.where(hit, idx_t, 0), axis=0, keepdims=True)
        gatet_ref[d:d + 1, :] = jnp.sum(jnp.where(hit, gate_t, 0.0), axis=0, keepdims=True)
    count0_ref[...] = count0
    idx = idxt_ref[...].astype(F32).T.astype(jnp.int32)
    loc0_ref[...] = _half_offsets(idx, 0)
    loc1_ref[...] = _half_offsets(idx, 1)


def _route(q, keys):
    row_spec = pl.BlockSpec((ROUTE_BLOCK, N_PICKS), lambda i: (i, 0))
    col_spec = pl.BlockSpec((N_PICKS, ROUTE_BLOCK), lambda i: (0, i))
    return pl.pallas_call(
        _route_kernel,
        grid=(N_TOK // ROUTE_BLOCK,),
        in_specs=[pl.BlockSpec((ROUTE_BLOCK, D_MODEL), lambda i: (i, 0)), _whole(keys.shape)],
        out_specs=[row_spec, row_spec, col_spec, col_spec,
                   pl.BlockSpec((1, ROUTE_BLOCK), lambda i: (0, i))],
        out_shape=[jax.ShapeDtypeStruct((N_TOK, N_PICKS), jnp.int32),
                   jax.ShapeDtypeStruct((N_TOK, N_PICKS), jnp.int32),
                   jax.ShapeDtypeStruct((N_PICKS, N_TOK), jnp.int32),
                   jax.ShapeDtypeStruct((N_PICKS, N_TOK), F32),
                   jax.ShapeDtypeStruct((1, N_TOK), jnp.int32)],
        compiler_params=pltpu.CompilerParams(vmem_limit_bytes=VMEM_LIMIT),
        name="route",
    )(q, keys)


N_GROUPS = N_PICKS // SUBLANES
WALK_GROUPS = 10


def _group_ranges(half, count0):
    if half == 0:
        return range(WALK_GROUPS), (WALK_GROUPS, (count0 + SUBLANES - 1) // SUBLANES)
    return range(N_GROUPS - WALK_GROUPS, N_GROUPS), (count0 // SUBLANES, N_GROUPS - WALK_GROUPS)


_MERGE_ORDER = (0, 4, 2, 6, 1, 5, 3, 7)


def _merge8(tiles, sub):
    m4 = sub < 4
    m2 = (sub & 3) < 2
    m1 = (sub & 1) == 0

    def l1(a, b):
        return jnp.where(m4, a, b) + pltpu.roll(jnp.where(m4, b, a), 4, 0)

    def l2(c, d):
        return jnp.where(m2, c, d) + jnp.where(m2, pltpu.roll(c, 6, 0), pltpu.roll(d, 2, 0))

    def l3(e, f):
        return jnp.where(m1, e, f) + jnp.where(m1, pltpu.roll(e, 7, 0), pltpu.roll(f, 1, 0))

    c = [l1(tiles[2 * i], tiles[2 * i + 1]) for i in range(4)]
    return l3(l2(c[0], c[1]), l2(c[2], c[3]))


def _peer_dots(half, cnt_ref, loc_ref, h_ref, tab_ref, actt_ref, gm_a, gm_b):
    sub = lax.broadcasted_iota(jnp.int32, (SUBLANES, LANES), 0)
    lane = lax.broadcasted_iota(jnp.int32, (N_PICKS, PEER_BLOCK), 1)
    tok0 = pl.program_id(0) * PEER_BLOCK
    actt_ref[...] = jnp.zeros((N_PICKS, PEER_BLOCK), F32)
    gm_a[...] = jnp.zeros((N_PICKS, LANES), F32)
    gm_b[...] = jnp.zeros((N_PICKS, LANES), F32)

    def partial_sums(t, gm):
        ht = h_ref[pl.ds(pl.multiple_of(t * SUBLANES, SUBLANES), SUBLANES), :]
        loc_row = loc_ref.at[t]

        def group(g):
            prods = []
            for s in range(SUBLANES):
                r = pl.multiple_of(loc_row[g * SUBLANES + _MERGE_ORDER[s]], SUBLANES)
                prods.append(tab_ref[pl.ds(r, SUBLANES), :] * ht)
            return _merge8(prods, sub)

        static_groups, (lo, hi) = _group_ranges(half, cnt_ref[tok0 + t])
        for g in static_groups:
            gm[g * SUBLANES:(g + 1) * SUBLANES, :] = group(g)

        def extra(g, carry):
            gm[pl.ds(pl.multiple_of(g * SUBLANES, SUBLANES), SUBLANES), :] = group(g)
            return carry

        lax.fori_loop(lo, hi, extra, 0)

    def reduce_into_column(t, gm):
        col = jnp.sum(gm[...], axis=1, keepdims=True)
        actt_ref[...] = jnp.where(lane == t, col, actt_ref[...])

    def token_pair(k, carry):
        t = 2 * k
        reduce_into_column(t - 1, gm_b)
        partial_sums(t, gm_a)
        reduce_into_column(t, gm_a)
        partial_sums(t + 1, gm_b)
        return carry

    lax.fori_loop(0, PEER_BLOCK // 2, token_pair, 0)
    reduce_into_column(PEER_BLOCK - 1, gm_b)


def _peer_u_first_kernel(cnt_ref, loc_ref, h_ref, tab_ref, actt_ref, gm_a, gm_b):
    _peer_dots(0, cnt_ref, loc_ref, h_ref, tab_ref, actt_ref, gm_a, gm_b)


def _peer_u_second_kernel(cnt_ref, loc_ref, h_ref, tab_ref, act0t_ref, idxt_ref, gatet_ref, w0_ref, w1_ref,
                          actt_s, gm_a, gm_b):
    _peer_dots(1, cnt_ref, loc_ref, h_ref, tab_ref, actt_s, gm_a, gm_b)
    in0 = idxt_ref[...] < HALF_EXPERTS
    w = gatet_ref[...] * _gelu(jnp.where(in0, act0t_ref[...], actt_s[...]))
    w0_ref[...] = jnp.where(in0, w, 0.0).T
    w1_ref[...] = jnp.where(in0, 0.0, w).T


def _peer_sum(half, cnt_ref, loc_ref, w_ref, tab_ref, init, f_ref, wb_a, wb_b):
    tok0 = pl.program_id(0) * PEER_BLOCK

    def splat_weights(t, wb):
        wb[...] = jnp.broadcast_to(w_ref[pl.ds(t, 1), :], (N_PICKS, N_PICKS)).T

    def weighted_sum(t, wb):
        rows = pl.ds(pl.multiple_of(t * SUBLANES, SUBLANES), SUBLANES)
        loc_row = loc_ref.at[t]

        def add_group(g, accs, weight_row):
            accs = list(accs)
            for s in range(SUBLANES):
                r = pl.multiple_of(loc_row[g * SUBLANES + s], SUBLANES)
                w = jnp.broadcast_to(weight_row(g * SUBLANES + s), (SUBLANES, LANES))
                accs[s % 4] = accs[s % 4] + tab_ref[pl.ds(r, SUBLANES), :] * w
            return tuple(accs)

        static_groups, (lo, hi) = _group_ranges(half, cnt_ref[tok0 + t])
        accs = (init(rows),) + (jnp.zeros((SUBLANES, LANES), F32),) * 3
        for g in static_groups:
            accs = add_group(g, accs, lambda e: wb[e:e + 1, :])
        accs = lax.fori_loop(lo, hi, lambda g, a: add_group(g, a, lambda e: wb[pl.ds(e, 1), :]), accs)
        f_ref[rows, :] = (accs[0] + accs[1]) + (accs[2] + accs[3])

    splat_weights(0, wb_a)

    def token_pair(k, carry):
        t = 2 * k
        splat_weights(t + 1, wb_b)
        weighted_sum(t, wb_a)
        splat_weights(jnp.minimum(t + 2, PEER_BLOCK - 1), wb_a)
        weighted_sum(t + 1, wb_b)
        return carry

    lax.fori_loop(0, PEER_BLOCK // 2, token_pair, 0)


def _peer_v_first_kernel(cnt_ref, loc_ref, w_ref, tab_ref, f_ref, wb_a, wb_b):
    _peer_sum(0, cnt_ref, loc_ref, w_ref, tab_ref, lambda rows: jnp.zeros((SUBLANES, LANES), F32), f_ref,
              wb_a, wb_b)


def _peer_v_second_kernel(cnt_ref, loc_ref, w_ref, tab_ref, f0_ref, f_ref, wb_a, wb_b):
    _peer_sum(1, cnt_ref, loc_ref, w_ref, tab_ref, lambda rows: f0_ref[rows, :], f_ref, wb_a, wb_b)


def _peer(h2, loc0, loc1, idx_t, gate_t, count0, u_tab, v_tab, layer):
    nblk = N_TOK // PEER_BLOCK
    h_tiles = h2.reshape(N_TOK * SUBLANES, LANES)
    cnt_spec = pl.BlockSpec(memory_space=pltpu.SMEM)
    smem_spec = pl.BlockSpec((PEER_BLOCK, N_PICKS), lambda i: (i, 0), memory_space=pltpu.SMEM)
    row_spec = pl.BlockSpec((PEER_BLOCK, N_PICKS), lambda i: (i, 0))
    col_spec = pl.BlockSpec((N_PICKS, PEER_BLOCK), lambda i: (0, i))
    tile_spec = pl.BlockSpec((PEER_BLOCK * SUBLANES, LANES), lambda i: (i, 0))
    half_spec = lambda half: pl.BlockSpec((None, HALF_EXPERTS * SUBLANES, LANES),
                                          lambda i: (layer, half, 0), pipeline_mode=pl.Buffered(1))
    row_shape = jax.ShapeDtypeStruct((N_TOK, N_PICKS), F32)
    col_shape = jax.ShapeDtypeStruct((N_PICKS, N_TOK), F32)
    tile_shape = jax.ShapeDtypeStruct((N_TOK * SUBLANES, LANES), F32)
    params = pltpu.CompilerParams(vmem_limit_bytes=VMEM_LIMIT)
    pair_scratch = [pltpu.VMEM((N_PICKS, LANES), F32)] * 2

    act0_t = pl.pallas_call(
        _peer_u_first_kernel, grid=(nblk,),
        in_specs=[cnt_spec, smem_spec, tile_spec, half_spec(0)],
        out_specs=col_spec, out_shape=col_shape, scratch_shapes=pair_scratch,
        compiler_params=params, name="peer_u0",
    )(count0, loc0, h_tiles, u_tab)
    w0, w1 = pl.pallas_call(
        _peer_u_second_kernel, grid=(nblk,),
        in_specs=[cnt_spec, smem_spec, tile_spec, half_spec(1), col_spec, col_spec, col_spec],
        out_specs=[row_spec, row_spec], out_shape=[row_shape, row_shape],
        scratch_shapes=[pltpu.VMEM((N_PICKS, PEER_BLOCK), F32)] + pair_scratch,
        compiler_params=params, name="peer_u1",
    )(count0, loc1, h_tiles, u_tab, act0_t, idx_t, gate_t)
    f0 = pl.pallas_call(
        _peer_v_first_kernel, grid=(nblk,),
        in_specs=[cnt_spec, smem_spec, row_spec, half_spec(0)],
        out_specs=tile_spec, out_shape=tile_shape, scratch_shapes=pair_scratch,
        compiler_params=params, name="peer_v0",
    )(count0, loc0, w0, v_tab)
    f = pl.pallas_call(
        _peer_v_second_kernel, grid=(nblk,),
        in_specs=[cnt_spec, smem_spec, row_spec, half_spec(1), tile_spec],
        out_specs=tile_spec, out_shape=tile_shape, scratch_shapes=pair_scratch,
        compiler_params=params, name="peer_v1",
    )(count0, loc1, w1, v_tab, f0)
    return f.reshape(N_TOK, D_MODEL)


def _close_kernel(x1_ref, f_ref, mod_ref, g_ref, b_ref, o_ref):
    g2 = mod_ref[...][:, 5 * D_MODEL:6 * D_MODEL]
    o_ref[...] = _ln(ALPHA * x1_ref[...] + g2 * f_ref[...]) * g_ref[1:2, :] + b_ref[1:2, :]


def _close(x1, f, mods, ln_g, ln_b, layer):
    row_spec = pl.BlockSpec((ROW_BLOCK, D_MODEL), lambda i: (i, 0))
    vec_spec = pl.BlockSpec((None, 2, D_MODEL), lambda i: (layer, 0, 0))
    return pl.pallas_call(
        _close_kernel,
        grid=(N_ROW_BLOCKS,),
        in_specs=[row_spec, row_spec,
                  pl.BlockSpec((None, None, 1, 6 * D_MODEL), lambda i: (layer, _mod_group(i), 0, 0)),
                  vec_spec, vec_spec],
        out_specs=row_spec,
        out_shape=jax.ShapeDtypeStruct((N_TOK, D_MODEL), F32),
        compiler_params=pltpu.CompilerParams(vmem_limit_bytes=VMEM_LIMIT),
        name="close",
    )(x1, f, mods, ln_g, ln_b)


def _rope_tables(dim, copies):
    rows = LAT_LEN // GRID_W
    row = jnp.repeat(jnp.arange(rows, dtype=F32), GRID_W)
    col = jnp.tile(jnp.arange(GRID_W, dtype=F32), rows)
    nf = dim // 4
    inv = ROPE_BASE ** (-jnp.arange(nf, dtype=F32) / nf)
    ar = row[:, None] * inv
    ac = col[:, None] * inv
    ang = jnp.concatenate([ar, ar, ac, ac], axis=-1)
    cos, sin = jnp.cos(ang), jnp.sin(ang)
    even_quarter = (jnp.arange(dim) // nf) % 2 == 0
    sin_up = jnp.where(even_quarter, -sin, 0.0)
    sin_dn = jnp.where(even_quarter, 0.0, sin)
    return tuple(jnp.tile(t, (1, copies)) for t in (cos, sin_up, sin_dn))


def kernel(x_prompt, x_sample, c, cache_win_k, cache_win_v, cache_diff_k, cache_diff_v, c_ctx, w_mod, b_mod, w_in, w_out, chunk_w, chunk_b, win_sink, diff_lam_q, diff_lam_k, diff_subln_g, pool_w, pool_scale, ln_g, ln_b, peer_wq, peer_keys, peer_u, peer_v):
    cvec = jnp.concatenate([c_ctx[None, :], c, jnp.zeros((8 - 1 - N_LAT_SEQ, D_MODEL), F32)], axis=0)
    mods = _modulation(cvec, w_mod, b_mod)[:, :1 + N_LAT_SEQ].reshape(DEPTH, 1 + N_LAT_SEQ, 1, 6 * D_MODEL)

    w_in_b = w_in.astype(BF16)
    w_out_b = w_out.astype(BF16)
    wq_b = peer_wq.astype(BF16)
    cw_b = chunk_w.astype(BF16)
    cb_full = jnp.repeat(jnp.swapaxes(chunk_b, 1, 2), HEAD_D, axis=2)
    eye = jnp.eye(4, dtype=F32)
    wbd = jnp.einsum('lgcd,gh->lgchd', pool_w, eye).reshape(DEPTH, GROUP_W, GROUP_W).astype(BF16)
    caches = (cache_win_k.reshape(N_LAT_SEQ, DEPTH, CTX_LEN, 128),
              cache_win_v.reshape(N_LAT_SEQ, DEPTH, CTX_LEN, 128),
              cache_diff_k.reshape(N_LAT_SEQ, DEPTH, CTX_LEN, GROUP_W),
              cache_diff_v.reshape(N_LAT_SEQ, DEPTH, CTX_LEN, GROUP_W))
    rope_tabs = _rope_tables(HEAD_D, GROUP_W // HEAD_D) + _rope_tables(C_SUB, GROUP_W // C_SUB)
    u_tiles = peer_u.reshape(DEPTH, N_EXPERTS * SUBLANES, LANES)
    v_tiles = peer_v.reshape(DEPTH, N_EXPERTS * SUBLANES, LANES)

    x = jnp.concatenate([x_prompt.reshape(N_CTX_TOK, D_MODEL), x_sample.reshape(N_LAT_TOK, D_MODEL)], axis=0)
    kbs, vbs, kcs, vcs = [], [], [], []
    for l in range(DEPTH):
        lam_init = 0.8 - 0.6 * math.exp(-0.3 * l)
        layer_w = (cw_b[l], cb_full[l], win_sink[l], diff_lam_q[l], diff_lam_k[l],
                   diff_subln_g[l].reshape(1, HEAD_D), wbd[l], pool_scale[l].reshape(1, GROUP_W))
        p = _in_proj(x, mods, w_in_b, l)
        pc = p[:N_CTX_TOK].reshape(N_CTX_SEQ, CTX_LEN, P_COLS)
        kbs.append(pc[..., COL_BK:COL_BV].reshape(N_CTX_SEQ, CTX_LEN, 2, HEAD_D))
        vbs.append(pc[..., COL_BV:COL_CQ].reshape(N_CTX_SEQ, CTX_LEN, 2, HEAD_D))
        kcs.append(pc[..., COL_CK:COL_CV].reshape(N_CTX_SEQ, CTX_LEN, 4, 2, C_SUB))
        vcs.append(pc[..., COL_CV:COL_DZ].reshape(N_CTX_SEQ, CTX_LEN, 4, HEAD_D))
        o = jnp.concatenate([_ctx_mix(p, layer_w, lam_init),
                             _lat_mix(p, caches, rope_tabs, layer_w, lam_init, l)], axis=0)
        x1, h2, q = _out_proj(o, x, mods, w_out_b, ln_g, ln_b, wq_b, l)
        loc0, loc1, idx_t, gate_t, count0 = _route(q, peer_keys[l])
        f = _peer(h2, loc0, loc1, idx_t, gate_t, count0.reshape(N_TOK), u_tiles, v_tiles, l)
        x = _close(x1, f, mods, ln_g, ln_b, l)
    return (x[:N_CTX_TOK].reshape(N_CTX_SEQ, CTX_LEN, D_MODEL),
            x[N_CTX_TOK:].reshape(N_LAT_SEQ, LAT_LEN, D_MODEL),
            jnp.stack(kbs, axis=1), jnp.stack(vbs, axis=1), jnp.stack(kcs, axis=1), jnp.stack(vcs, axis=1))
```

```python
import functools
import math

import jax
import jax.numpy as jnp
import numpy as np
from jax import lax
from jax.experimental import pallas as pl
from jax.experimental.pallas import tpu as pltpu

F32 = jnp.float32
BF16 = jnp.bfloat16
HIGHEST = lax.Precision.HIGHEST

D_MODEL = 1024
DEPTH = 4
N_CTX_SEQ = 16
CTX_LEN = 256
N_LAT_SEQ = 2
LAT_LEN = 1024
N_CTX_TOK = N_CTX_SEQ * CTX_LEN
N_LAT_TOK = N_LAT_SEQ * LAT_LEN
N_TOK = N_CTX_TOK + N_LAT_TOK
GRID_W = 64
BLOCK = 128
GROUP_W = 256
HEAD_D = 64
C_SUB = 32
WINDOW = 128
POOL_SIZES = (2, 4, 8, 16)
P_COLS = 2048
COL_AU, COL_AV, COL_BQ, COL_BK, COL_BV, COL_CQ, COL_CK, COL_CV, COL_DZ = (
    0, 256, 512, 768, 896, 1024, 1280, 1536, 1792)
PK_HEADS = 8
N_KEYS = 128
N_EXPERTS = N_KEYS * N_KEYS
PK_TOPK = 16
N_PICKS = PK_HEADS * PK_TOPK
ROPE_BASE = 10000.0
ALPHA = (2 * DEPTH) ** 0.25
LN_EPS = 1e-5
NEG = -1e30

ROW_BLOCK = 512
N_ROW_BLOCKS = N_TOK // ROW_BLOCK
N_CTX_ROW_BLOCKS = N_CTX_TOK // ROW_BLOCK
ROW_BLOCKS_PER_LAT_SEQ = LAT_LEN // ROW_BLOCK
ROUTE_BLOCK = 128
PEER_BLOCK = 128
SUBLANES = 8
LANES = 128
HALF_EXPERTS = N_EXPERTS // 2
VMEM_LIMIT = 56 * 1024 * 1024


def _mm(a, b):
    return jnp.dot(a.astype(BF16), b.astype(BF16), preferred_element_type=F32)


def _mm_nt(a, b):
    return lax.dot_general(a.astype(BF16), b.astype(BF16), (((1,), (1,)), ((), ())),
                           preferred_element_type=F32)


def _gelu(x):
    return x * (0.5 * (1.0 + jnp.tanh(0.7978845608028654 * (x + 0.044715 * (x * x * x)))))


def _ln(x):
    mu = jnp.mean(x, axis=-1, keepdims=True)
    xc = x - mu
    var = jnp.mean(xc * xc, axis=-1, keepdims=True)
    return xc * lax.rsqrt(var + LN_EPS)


def _mod_group(i):
    return jnp.where(i < N_CTX_ROW_BLOCKS, 0, 1 + (i - N_CTX_ROW_BLOCKS) // ROW_BLOCKS_PER_LAT_SEQ)


def _mod_kernel(c_ref, w_ref, b_ref, o_ref):
    c = c_ref[...]
    s = c * jax.nn.sigmoid(c)
    o_ref[...] = jnp.dot(s, w_ref[...], precision=HIGHEST, preferred_element_type=F32) + b_ref[...]


def _modulation(cvec, w_mod, b_mod):
    nj = 4
    cols = 6 * D_MODEL // nj
    return pl.pallas_call(
        _mod_kernel,
        grid=(DEPTH, nj),
        in_specs=[pl.BlockSpec((8, D_MODEL), lambda l, j: (0, 0)),
                  pl.BlockSpec((None, D_MODEL, cols), lambda l, j: (l, 0, j)),
                  pl.BlockSpec((None, 1, cols), lambda l, j: (l, 0, j))],
        out_specs=pl.BlockSpec((None, 8, cols), lambda l, j: (l, 0, j)),
        out_shape=jax.ShapeDtypeStruct((DEPTH, 8, 6 * D_MODEL), F32),
        compiler_params=pltpu.CompilerParams(vmem_limit_bytes=VMEM_LIMIT),
        name="modulation",
    )(cvec, w_mod, b_mod.reshape(DEPTH, 1, 6 * D_MODEL))


def _in_proj_kernel(x_ref, mod_ref, w_ref, o_ref):
    m = mod_ref[...]
    h = _ln(x_ref[...]) * (1.0 + m[:, D_MODEL:2 * D_MODEL]) + m[:, 0:D_MODEL]
    o_ref[...] = jnp.dot(h.astype(BF16), w_ref[...], preferred_element_type=F32)


def _in_proj(x, mods, w_in_bf16, layer):
    return pl.pallas_call(
        _in_proj_kernel,
        grid=(N_ROW_BLOCKS,),
        in_specs=[pl.BlockSpec((ROW_BLOCK, D_MODEL), lambda i: (i, 0)),
                  pl.BlockSpec((None, None, 1, 6 * D_MODEL), lambda i: (layer, _mod_group(i), 0, 0)),
                  pl.BlockSpec((None, D_MODEL, P_COLS), lambda i: (layer, 0, 0))],
        out_specs=pl.BlockSpec((ROW_BLOCK, P_COLS), lambda i: (i, 0)),
        out_shape=jax.ShapeDtypeStruct((N_TOK, P_COLS), F32),
        compiler_params=pltpu.CompilerParams(vmem_limit_bytes=VMEM_LIMIT),
        name="in_proj",
    )(x, mods, w_in_bf16)


def _chunk_gating(u_raw, v_raw, cw_ref, cb_full):
    u = _gelu(u_raw)
    v = _gelu(v_raw).astype(BF16)
    head = lax.broadcasted_iota(jnp.int32, (BLOCK, GROUP_W), 1) // HEAD_D
    mixed = cb_full
    for h in range(4):
        mixed = mixed + jnp.where(head == h, jnp.dot(cw_ref[h], v, preferred_element_type=F32), 0.0)
    return u * mixed


def _pool(z_rows, z_win, row0, key0, seq_len, wbd_ref, pool_scale):
    r, k = z_rows.shape[0], z_win.shape[0]
    t = row0 + lax.broadcasted_iota(jnp.int32, (r, k), 0)
    pos = key0 + lax.broadcasted_iota(jnp.int32, (r, k), 1)
    t1 = row0 + lax.broadcasted_iota(jnp.int32, (r, 1), 0)
    z_hi = z_win.astype(BF16)
    z_lo = (z_win - z_hi.astype(F32)).astype(BF16)
    group = lax.broadcasted_iota(jnp.int32, (r, GROUP_W), 1) // HEAD_D
    mean = jnp.zeros((r, GROUP_W), F32)
    for g, w in enumerate(POOL_SIZES):
        lo = jnp.maximum(t - w // 2, 0)
        hi = jnp.minimum(t + w // 2, seq_len)
        band = jnp.where((pos >= lo) & (pos < hi), 1.0, 0.0).astype(BF16)
        tot = (jnp.dot(band, z_hi, preferred_element_type=F32)
               + jnp.dot(band, z_lo, preferred_element_type=F32))
        cnt = (jnp.minimum(t1 + w // 2, seq_len) - jnp.maximum(t1 - w // 2, 0)).astype(F32)
        mean = mean + jnp.where(group == g, tot / cnt, 0.0)
    pooled = mean - z_rows
    return jnp.dot(pooled.astype(BF16), wbd_ref[...], preferred_element_type=F32) * pool_scale


def _diff_lambda(lq_ref, lk_ref, lam_init):
    lq = lq_ref[...]
    lk = lk_ref[...]
    s0 = jnp.sum(lq[0:1] * lk[0:1], axis=-1, keepdims=True)
    s1 = jnp.sum(lq[1:2] * lk[1:2], axis=-1, keepdims=True)
    return jnp.exp(s0) - jnp.exp(s1) + lam_init


def _sub_norm(o, g, lam_init):
    ms = jnp.mean(o * o, axis=-1, keepdims=True)
    return o * lax.rsqrt(ms + LN_EPS) * g * (1.0 - lam_init)


def _ctx_mix_kernel(lam_init, p_ref, cw_ref, cb_ref, sink_ref, lq_ref, lk_ref, sg_ref, wbd_ref,
                    ps_ref, o_ref):
    s_len = CTX_LEN
    for n in range(s_len // BLOCK):
        rows = slice(n * BLOCK, (n + 1) * BLOCK)
        o_ref[rows, 0:GROUP_W] = _chunk_gating(p_ref[rows, COL_AU:COL_AU + GROUP_W],
                                               p_ref[rows, COL_AV:COL_AV + GROUP_W], cw_ref, cb_ref[...])
    for kv in range(2):
        kk = p_ref[:, COL_BK + kv * HEAD_D:COL_BK + (kv + 1) * HEAD_D].astype(BF16)
        vv = p_ref[:, COL_BV + kv * HEAD_D:COL_BV + (kv + 1) * HEAD_D].astype(BF16)
        for g in range(2):
            h = kv * 2 + g
            q = p_ref[:, COL_BQ + h * HEAD_D:COL_BQ + (h + 1) * HEAD_D] * (HEAD_D ** -0.5)
            s = _mm_nt(q, kk)
            sk = sink_ref[h]
            m = jnp.maximum(jnp.max(s, axis=-1, keepdims=True), sk)
            e = jnp.exp(s - m)
            den = jnp.sum(e, axis=-1, keepdims=True) + jnp.exp(sk - m)
            o_ref[:, GROUP_W + h * HEAD_D:GROUP_W + (h + 1) * HEAD_D] = _mm(e, vv) / den
    lam = _diff_lambda(lq_ref, lk_ref, lam_init)
    scale = C_SUB ** -0.5
    for h in range(4):
        vv = p_ref[:, COL_CV + h * HEAD_D:COL_CV + (h + 1) * HEAD_D].astype(BF16)
        parts = []
        for j in range(2):
            c0 = h * HEAD_D + j * C_SUB
            s = _mm_nt(p_ref[:, COL_CQ + c0:COL_CQ + c0 + C_SUB],
                       p_ref[:, COL_CK + c0:COL_CK + c0 + C_SUB]) * scale
            e = jnp.exp(s - jnp.max(s, axis=-1, keepdims=True))
            parts.append(_mm(e, vv) / jnp.sum(e, axis=-1, keepdims=True))
        o = parts[0] - lam * parts[1]
        o_ref[:, 2 * GROUP_W + h * HEAD_D:2 * GROUP_W + (h + 1) * HEAD_D] = _sub_norm(o, sg_ref[...], lam_init)
    z = p_ref[:, COL_DZ:COL_DZ + GROUP_W]
    o_ref[:, 3 * GROUP_W:4 * GROUP_W] = _pool(z, z, 0, 0, s_len, wbd_ref, ps_ref[...])


def _whole(shape):
    nd = len(shape)
    return pl.BlockSpec(shape, lambda *_: (0,) * nd)


def _ctx_mix(p, layer_w, lam_init):
    cw, cb_full, sink, lq, lk, sg, wbd, ps = layer_w
    return pl.pallas_call(
        functools.partial(_ctx_mix_kernel, lam_init),
        grid=(N_CTX_SEQ,),
        in_specs=[pl.BlockSpec((CTX_LEN, P_COLS), lambda b: (b, 0)),
                  _whole(cw.shape), _whole(cb_full.shape),
                  pl.BlockSpec(memory_space=pltpu.SMEM),
                  _whole(lq.shape), _whole(lk.shape), _whole(sg.shape), _whole(wbd.shape),
                  _whole(ps.shape)],
        out_specs=pl.BlockSpec((CTX_LEN, D_MODEL), lambda b: (b, 0)),
        out_shape=jax.ShapeDtypeStruct((N_CTX_TOK, D_MODEL), F32),
        compiler_params=pltpu.CompilerParams(vmem_limit_bytes=VMEM_LIMIT),
        name="ctx_mix",
    )(p, cw, cb_full, sink, lq, lk, sg, wbd, ps)


def _rope(x, cos, sin_up, sin_dn, shift):
    outs = []
    for c0 in range(0, x.shape[1], LANES):
        xs = x[:, c0:c0 + LANES]
        up = pltpu.roll(xs, LANES - shift, 1)
        dn = pltpu.roll(xs, shift, 1)
        outs.append(xs * cos[:, c0:c0 + LANES] + up * sin_up[:, c0:c0 + LANES]
                    + dn * sin_dn[:, c0:c0 + LANES])
    return outs[0] if len(outs) == 1 else jnp.concatenate(outs, axis=1)


def _lat_mix_kernel(lam_init, p_ref, wk_ref, wv_ref, dk_ref, dv_ref,
                    cosb_ref, sbu_ref, sbd_ref, cosc_ref, scu_ref, scd_ref,
                    cw_ref, cb_ref, sink_ref, lq_ref, lk_ref, sg_ref, wbd_ref, ps_ref,
                    o_ref, kb_s, kc_s):
    n = pl.program_id(1)
    s_len = LAT_LEN

    @pl.when(n == 0)
    def _():
        kb_s[...] = _rope(p_ref[:, COL_BK:COL_BK + 128], cosb_ref[:, 0:128], sbu_ref[:, 0:128],
                          sbd_ref[:, 0:128], HEAD_D // 4).astype(BF16)
        kc_s[...] = _rope(p_ref[:, COL_CK:COL_CK + GROUP_W], cosc_ref[...], scu_ref[...],
                          scd_ref[...], C_SUB // 4).astype(BF16)

    r0 = pl.multiple_of(n * BLOCK, BLOCK)
    rows = pl.ds(r0, BLOCK)
    o_ref[:, 0:GROUP_W] = _chunk_gating(p_ref[rows, COL_AU:COL_AU + GROUP_W],
                                        p_ref[rows, COL_AV:COL_AV + GROUP_W], cw_ref, cb_ref[...])
    w0 = pl.multiple_of(jnp.clip((n - 1) * BLOCK, 0, s_len - 3 * BLOCK), BLOCK)
    win = pl.ds(w0, 3 * BLOCK)
    qb = _rope(p_ref[rows, COL_BQ:COL_BQ + GROUP_W], cosb_ref[rows, :], sbu_ref[rows, :],
               sbd_ref[rows, :], HEAD_D // 4) * (HEAD_D ** -0.5)
    qpos = r0 + lax.broadcasted_iota(jnp.int32, (BLOCK, 3 * BLOCK), 0)
    kpos = w0 + lax.broadcasted_iota(jnp.int32, (BLOCK, 3 * BLOCK), 1)
    valid = jnp.abs(kpos - qpos) <= WINDOW
    for kv in range(2):
        cs = slice(kv * HEAD_D, (kv + 1) * HEAD_D)
        k_loc = kb_s[win, cs]
        v_loc = p_ref[win, COL_BV + kv * HEAD_D:COL_BV + (kv + 1) * HEAD_D].astype(BF16)
        k_ctx = wk_ref[:, cs].astype(BF16)
        v_ctx = wv_ref[:, cs].astype(BF16)
        for g in range(2):
            h = kv * 2 + g
            q = qb[:, h * HEAD_D:(h + 1) * HEAD_D]
            s_loc = jnp.where(valid, _mm_nt(q, k_loc), NEG)
            s_ctx = _mm_nt(q, k_ctx)
            sk = sink_ref[h]
            m = jnp.maximum(jnp.maximum(jnp.max(s_loc, axis=-1, keepdims=True),
                                        jnp.max(s_ctx, axis=-1, keepdims=True)), sk)
            e_loc = jnp.exp(s_loc - m)
            e_ctx = jnp.exp(s_ctx - m)
            den = (jnp.sum(e_loc, axis=-1, keepdims=True) + jnp.sum(e_ctx, axis=-1, keepdims=True)
                   + jnp.exp(sk - m))
            o_ref[:, GROUP_W + h * HEAD_D:GROUP_W + (h + 1) * HEAD_D] = (
                _mm(e_loc, v_loc) + _mm(e_ctx, v_ctx)) / den
    lam = _diff_lambda(lq_ref, lk_ref, lam_init)
    scale = C_SUB ** -0.5
    qc = _rope(p_ref[rows, COL_CQ:COL_CQ + GROUP_W], cosc_ref[rows, :], scu_ref[rows, :],
               scd_ref[rows, :], C_SUB // 4)
    for h in range(4):
        v_ctx = dv_ref[:, h * HEAD_D:(h + 1) * HEAD_D].astype(BF16)
        v_lat = p_ref[:, COL_CV + h * HEAD_D:COL_CV + (h + 1) * HEAD_D].astype(BF16)
        parts = []
        for j in range(2):
            cs = slice(h * HEAD_D + j * C_SUB, h * HEAD_D + (j + 1) * C_SUB)
            q = qc[:, cs]
            s_ctx = _mm_nt(q, dk_ref[:, cs]) * scale
            s_lat = _mm_nt(q, kc_s[:, cs]) * scale
            m = jnp.maximum(jnp.max(s_ctx, axis=-1, keepdims=True), jnp.max(s_lat, axis=-1, keepdims=True))
            e_ctx = jnp.exp(s_ctx - m)
            e_lat = jnp.exp(s_lat - m)
            den = jnp.sum(e_ctx, axis=-1, keepdims=True) + jnp.sum(e_lat, axis=-1, keepdims=True)
            parts.append((_mm(e_ctx, v_ctx) + _mm(e_lat, v_lat)) / den)
        o = parts[0] - lam * parts[1]
        o_ref[:, 2 * GROUP_W + h * HEAD_D:2 * GROUP_W + (h + 1) * HEAD_D] = _sub_norm(o, sg_ref[...], lam_init)
    o_ref[:, 3 * GROUP_W:4 * GROUP_W] = _pool(p_ref[rows, COL_DZ:COL_DZ + GROUP_W],
                                              p_ref[win, COL_DZ:COL_DZ + GROUP_W],
                                              r0, w0, s_len, wbd_ref, ps_ref[...])


def _lat_mix(p, caches, rope_tabs, layer_w, lam_init, layer):
    cw, cb_full, sink, lq, lk, sg, wbd, ps = layer_w
    nb = LAT_LEN // BLOCK
    ctx_blocks = N_CTX_TOK // LAT_LEN
    cache_spec = lambda width: pl.BlockSpec((None, None, CTX_LEN, width), lambda b, n: (b, layer, 0, 0))
    rope_spec = pl.BlockSpec((LAT_LEN, GROUP_W), lambda b, n: (0, 0))
    return pl.pallas_call(
        functools.partial(_lat_mix_kernel, lam_init),
        grid=(N_LAT_SEQ, nb),
        in_specs=[pl.BlockSpec((LAT_LEN, P_COLS), lambda b, n: (ctx_blocks + b, 0)),
                  cache_spec(128), cache_spec(128), cache_spec(256), cache_spec(256)]
                 + [rope_spec] * 6
                 + [_whole(cw.shape), _whole(cb_full.shape), pl.BlockSpec(memory_space=pltpu.SMEM),
                    _whole(lq.shape), _whole(lk.shape), _whole(sg.shape), _whole(wbd.shape),
                    _whole(ps.shape)],
        out_specs=pl.BlockSpec((BLOCK, D_MODEL), lambda b, n: (b * nb + n, 0)),
        out_shape=jax.ShapeDtypeStruct((N_LAT_TOK, D_MODEL), F32),
        scratch_shapes=[pltpu.VMEM((LAT_LEN, 128), BF16), pltpu.VMEM((LAT_LEN, GROUP_W), BF16)],
        compiler_params=pltpu.CompilerParams(vmem_limit_bytes=VMEM_LIMIT,
                                             dimension_semantics=("arbitrary", "arbitrary")),
        name="lat_mix",
    )(p, *caches, *rope_tabs, cw, cb_full, sink, lq, lk, sg, wbd, ps)


def _out_proj_kernel(o_ref, x_ref, mod_ref, wo_ref, g_ref, b_ref, wq_ref, x1_ref, h2_ref, q_ref):
    m = mod_ref[...]
    g1 = m[:, 2 * D_MODEL:3 * D_MODEL]
    sh2 = m[:, 3 * D_MODEL:4 * D_MODEL]
    sc2 = m[:, 4 * D_MODEL:5 * D_MODEL]
    y = jnp.dot(o_ref[...].astype(BF16), wo_ref[...], preferred_element_type=F32)
    x1 = _ln(ALPHA * x_ref[...] + g1 * y) * g_ref[0:1, :] + b_ref[0:1, :]
    x1_ref[...] = x1
    h2 = _ln(x1) * (1.0 + sc2) + sh2
    h2_ref[...] = h2
    q_ref[...] = jnp.dot(h2.astype(BF16), wq_ref[...], preferred_element_type=F32)


def _out_proj(o, x, mods, w_out_bf16, ln_g, ln_b, wq_bf16, layer):
    row_spec = pl.BlockSpec((ROW_BLOCK, D_MODEL), lambda i: (i, 0))
    mat_spec = pl.BlockSpec((None, D_MODEL, D_MODEL), lambda i: (layer, 0, 0))
    vec_spec = pl.BlockSpec((None, 2, D_MODEL), lambda i: (layer, 0, 0))
    shape = jax.ShapeDtypeStruct((N_TOK, D_MODEL), F32)
    return pl.pallas_call(
        _out_proj_kernel,
        grid=(N_ROW_BLOCKS,),
        in_specs=[row_spec, row_spec,
                  pl.BlockSpec((None, None, 1, 6 * D_MODEL), lambda i: (layer, _mod_group(i), 0, 0)),
                  mat_spec, vec_spec, vec_spec, mat_spec],
        out_specs=[row_spec, row_spec, row_spec],
        out_shape=[shape, shape, shape],
        compiler_params=pltpu.CompilerParams(vmem_limit_bytes=VMEM_LIMIT),
        name="out_proj",
    )(o, x, mods, w_out_bf16, ln_g, ln_b, wq_bf16)


def _top16_rows(sc, order, payload=None):
    out_row = lax.broadcasted_iota(jnp.int32, (PK_TOPK, sc.shape[1]), 0)
    vals = jnp.zeros((PK_TOPK, sc.shape[1]), F32)
    sel = jnp.zeros((PK_TOPK, sc.shape[1]), jnp.int32)
    for i in range(PK_TOPK):
        m = jnp.max(sc, axis=0, keepdims=True)
        am = jnp.min(jnp.where(sc == m, order, jnp.int32(2 ** 30)), axis=0, keepdims=True)
        hit = order == am
        if payload is None:
            picked = am
        else:
            picked = jnp.sum(jnp.where(hit, payload, 0), axis=0, keepdims=True)
        vals = jnp.where(out_row == i, m, vals)
        sel = jnp.where(out_row == i, picked, sel)
        sc = jnp.where(hit, NEG, sc)
    return vals, sel


def _pair_candidates(sv, si):
    n_tok = sv[0].shape[1]
    rows = lax.broadcasted_iota(jnp.int32, (SUBLANES, n_tok), 0)
    cands, flats, experts = [], [], []

    def piece(a, b0, valid):
        v = sv[0][a:a + 1, :] + sv[1][b0:b0 + SUBLANES, :]
        cands.append(v if valid >= SUBLANES else jnp.where(rows < valid, v, NEG))
        flats.append(a * PK_TOPK + b0 + rows)
        experts.append(si[0][a:a + 1, :] * N_KEYS + si[1][b0:b0 + SUBLANES, :])

    piece(0, 0, SUBLANES)
    piece(0, SUBLANES, SUBLANES)
    for a in range(1, SUBLANES):
        piece(a, 0, PK_TOPK // (a + 1))
    cands.append(sv[0][SUBLANES:, :] + sv[1][0:1, :])
    flats.append((SUBLANES + rows) * PK_TOPK)
    experts.append(si[0][SUBLANES:, :] * N_KEYS + si[1][0:1, :])
    return (jnp.concatenate(cands, axis=0), jnp.concatenate(flats, axis=0),
            jnp.concatenate(experts, axis=0))


def _half_offsets(idx, half):
    local = idx - half * HALF_EXPERTS
    inside = (local >= 0) & (local < HALF_EXPERTS)
    return jnp.where(inside, local, 0) * SUBLANES


def _route_kernel(q_ref, keys_ref, loc0_ref, loc1_ref, idxt_ref, gatet_ref, count0_ref,
                  slot_s, idx_s, gate_s):
    n_tok = q_ref.shape[0]
    key_row = lax.broadcasted_iota(jnp.int32, (N_KEYS, n_tok), 0)
    for h in range(PK_HEADS):
        sv = []
        si = []
        for j in range(2):
            c0 = (h * 2 + j) * HEAD_D
            sc = lax.dot_general(keys_ref[j], q_ref[:, c0:c0 + HEAD_D], (((1,), (1,)), ((), ())),
                                 precision=HIGHEST, preferred_element_type=F32)
            v, i = _top16_rows(sc, key_row)
            sv.append(v)
            si.append(i)
        cv, ce = _top16_rows(*_pair_candidates(sv, si))
        e = jnp.exp(cv - jnp.max(cv, axis=0, keepdims=True))
        gate_s[h * PK_TOPK:(h + 1) * PK_TOPK, :] = e / jnp.sum(e, axis=0, keepdims=True)
        idx_s[h * PK_TOPK:(h + 1) * PK_TOPK, :] = ce
    in0 = idx_s[...] < HALF_EXPERTS
    before = (lax.broadcasted_iota(jnp.int32, (N_PICKS, N_PICKS), 1)
              < lax.broadcasted_iota(jnp.int32, (N_PICKS, N_PICKS), 0))
    rank0 = jnp.dot(jnp.where(before, 1.0, 0.0).astype(BF16), jnp.where(in0, 1.0, 0.0).astype(BF16),
                    preferred_element_type=F32).astype(jnp.int32)
    count0 = jnp.sum(jnp.where(in0, 1, 0), axis=0, keepdims=True)
    pick = lax.broadcasted_iota(jnp.int32, (N_PICKS, n_tok), 0)
    slot_s[...] = jnp.where(in0, rank0, count0 + pick - rank0)
    idx_sorted = jnp.zeros((N_PICKS, n_tok), jnp.int32)
    gate_sorted = jnp.zeros((N_PICKS, n_tok), F32)
    for e in range(N_PICKS):
        hit = pick == slot_s[e:e + 1, :]
        idx_sorted = jnp.where(hit, idx_s[e:e + 1, :], idx_sorted)
        gate_sorted = jnp.where(hit, gate_s[e:e + 1, :], gate_sorted)
    idxt_ref[...] = idx_sorted
    gatet_ref[...] = gate_sorted
    count0_ref[...] = count0
    idx = idx_sorted.astype(F32).T.astype(jnp.int32)
    loc0_ref[...] = _half_offsets(idx, 0)
    loc1_ref[...] = _half_offsets(idx, 1)


def _route(q, keys):
    row_spec = pl.BlockSpec((ROUTE_BLOCK, N_PICKS), lambda i: (i, 0))
    col_spec = pl.BlockSpec((N_PICKS, ROUTE_BLOCK), lambda i: (0, i))
    return pl.pallas_call(
        _route_kernel,
        grid=(N_TOK // ROUTE_BLOCK,),
        in_specs=[pl.BlockSpec((ROUTE_BLOCK, D_MODEL), lambda i: (i, 0)), _whole(keys.shape)],
        out_specs=[row_spec, row_spec, col_spec, col_spec,
                   pl.BlockSpec((1, ROUTE_BLOCK), lambda i: (0, i))],
        out_shape=[jax.ShapeDtypeStruct((N_TOK, N_PICKS), jnp.int32),
                   jax.ShapeDtypeStruct((N_TOK, N_PICKS), jnp.int32),
                   jax.ShapeDtypeStruct((N_PICKS, N_TOK), jnp.int32),
                   jax.ShapeDtypeStruct((N_PICKS, N_TOK), F32),
                   jax.ShapeDtypeStruct((1, N_TOK), jnp.int32)],
        scratch_shapes=[pltpu.VMEM((N_PICKS, ROUTE_BLOCK), jnp.int32),
                        pltpu.VMEM((N_PICKS, ROUTE_BLOCK), jnp.int32),
                        pltpu.VMEM((N_PICKS, ROUTE_BLOCK), F32)],
        compiler_params=pltpu.CompilerParams(vmem_limit_bytes=VMEM_LIMIT),
        name="route",
    )(q, keys)


N_GROUPS = N_PICKS // SUBLANES
WALK_GROUPS = 10


def _group_ranges(half, count0):
    if half == 0:
        return range(WALK_GROUPS), (WALK_GROUPS, (count0 + SUBLANES - 1) // SUBLANES)
    return range(N_GROUPS - WALK_GROUPS, N_GROUPS), (count0 // SUBLANES, N_GROUPS - WALK_GROUPS)


_MERGE_ORDER = (0, 4, 2, 6, 1, 5, 3, 7)


def _merge8(tiles, sub):
    m4 = sub < 4
    m2 = (sub & 3) < 2
    m1 = (sub & 1) == 0

    def l1(a, b):
        return jnp.where(m4, a, b) + pltpu.roll(jnp.where(m4, b, a), 4, 0)

    def l2(c, d):
        return jnp.where(m2, c, d) + jnp.where(m2, pltpu.roll(c, 6, 0), pltpu.roll(d, 2, 0))

    def l3(e, f):
        return jnp.where(m1, e, f) + jnp.where(m1, pltpu.roll(e, 7, 0), pltpu.roll(f, 1, 0))

    c = [l1(tiles[2 * i], tiles[2 * i + 1]) for i in range(4)]
    return l3(l2(c[0], c[1]), l2(c[2], c[3]))


def _extra_groups(half, cnt_ref, tok0, t):
    _, (lo, hi) = _group_ranges(half, cnt_ref[tok0 + t])
    return lo, hi


def _for_tokens_with_extra_groups(half, cnt_ref, tok0, body):
    if half == 0:
        extreme = lax.fori_loop(0, PEER_BLOCK, lambda t, m: jnp.maximum(m, cnt_ref[tok0 + t]),
                                jnp.int32(0), unroll=8)
    else:
        extreme = lax.fori_loop(0, PEER_BLOCK, lambda t, m: jnp.minimum(m, cnt_ref[tok0 + t]),
                                jnp.int32(N_PICKS), unroll=8)
    lo, hi = _group_ranges(half, extreme)[1]

    @pl.when(hi > lo)
    def _():
        def token(t, carry):
            lo, hi = _extra_groups(half, cnt_ref, tok0, t)

            @pl.when(hi > lo)
            def _():
                body(t, lo, hi)

            return carry

        lax.fori_loop(0, PEER_BLOCK, token, 0)


def _peer_dots(half, cnt_ref, loc_ref, h_ref, tab_ref, actt_ref, gm):
    sub = lax.broadcasted_iota(jnp.int32, (SUBLANES, LANES), 0)
    lane = lax.broadcasted_iota(jnp.int32, (N_PICKS, PEER_BLOCK), 1)
    tok0 = pl.program_id(0) * PEER_BLOCK
    actt_ref[...] = jnp.zeros((N_PICKS, PEER_BLOCK), F32)
    for buf in gm:
        buf[...] = jnp.zeros((N_PICKS, LANES), F32)

    def group_sums(t, g):
        ht = h_ref[pl.ds(pl.multiple_of(t * SUBLANES, SUBLANES), SUBLANES), :]
        loc_row = loc_ref.at[t]
        prods = []
        for s in range(SUBLANES):
            r = pl.multiple_of(loc_row[g * SUBLANES + _MERGE_ORDER[s]], SUBLANES)
            prods.append(tab_ref[pl.ds(r, SUBLANES), :] * ht)
        return _merge8(prods, sub)

    def partial_sums(t, buf):
        for g in _group_ranges(half, 0)[0]:
            buf[g * SUBLANES:(g + 1) * SUBLANES, :] = group_sums(t, g)

    def reduce_into_column(t, buf):
        col = jnp.sum(buf[...], axis=1, keepdims=True)
        actt_ref[...] = jnp.where(lane == t, col, actt_ref[...])

    def four_tokens(k, carry):
        t = 4 * k
        reduce_into_column(t - 2, gm[2])
        reduce_into_column(t - 1, gm[3])
        partial_sums(t, gm[0])
        partial_sums(t + 1, gm[1])
        reduce_into_column(t, gm[0])
        reduce_into_column(t + 1, gm[1])
        partial_sums(t + 2, gm[2])
        partial_sums(t + 3, gm[3])
        return carry

    lax.fori_loop(0, PEER_BLOCK // 4, four_tokens, 0)
    reduce_into_column(PEER_BLOCK - 2, gm[2])
    reduce_into_column(PEER_BLOCK - 1, gm[3])

    lane8 = lax.broadcasted_iota(jnp.int32, (SUBLANES, PEER_BLOCK), 1)

    def extra_groups(t, lo, hi):
        def one(g, carry):
            rows = pl.ds(pl.multiple_of(g * SUBLANES, SUBLANES), SUBLANES)
            col = jnp.sum(group_sums(t, g), axis=1, keepdims=True)
            actt_ref[rows, :] = jnp.where(lane8 == t, col, actt_ref[rows, :])
            return carry

        lax.fori_loop(lo, hi, one, 0)

    _for_tokens_with_extra_groups(half, cnt_ref, tok0, extra_groups)


def _peer_u_first_kernel(cnt_ref, loc_ref, h_ref, tab_ref, actt_ref, *gm):
    _peer_dots(0, cnt_ref, loc_ref, h_ref, tab_ref, actt_ref, gm)


def _peer_u_second_kernel(cnt_ref, loc_ref, h_ref, tab_ref, act0t_ref, idxt_ref, gatet_ref, w0_ref, w1_ref,
                          actt_s, *gm):
    _peer_dots(1, cnt_ref, loc_ref, h_ref, tab_ref, actt_s, gm)
    in0 = idxt_ref[...] < HALF_EXPERTS
    w = gatet_ref[...] * _gelu(jnp.where(in0, act0t_ref[...], actt_s[...]))
    w0_ref[...] = jnp.where(in0, w, 0.0).T
    w1_ref[...] = jnp.where(in0, 0.0, w).T


def _peer_sum(half, cnt_ref, loc_ref, w_ref, tab_ref, init, f_ref, wb):
    tok0 = pl.program_id(0) * PEER_BLOCK

    def splat_weights(t, buf):
        buf[...] = jnp.broadcast_to(w_ref[pl.ds(t, 1), :], (N_PICKS, N_PICKS)).T

    def add_group(t, g, accs, weight_row):
        loc_row = loc_ref.at[t]
        accs = list(accs)
        for s in range(SUBLANES):
            r = pl.multiple_of(loc_row[g * SUBLANES + s], SUBLANES)
            w = jnp.broadcast_to(weight_row(g * SUBLANES + s), (SUBLANES, LANES))
            accs[s % 4] = accs[s % 4] + tab_ref[pl.ds(r, SUBLANES), :] * w
        return tuple(accs)

    def token_rows(t):
        return pl.ds(pl.multiple_of(t * SUBLANES, SUBLANES), SUBLANES)

    def weighted_sum(t, buf):
        accs = (init(token_rows(t)),) + (jnp.zeros((SUBLANES, LANES), F32),) * 3
        for g in _group_ranges(half, 0)[0]:
            accs = add_group(t, g, accs, lambda e: buf[e:e + 1, :])
        f_ref[token_rows(t), :] = (accs[0] + accs[1]) + (accs[2] + accs[3])

    splat_weights(0, wb[0])
    splat_weights(1, wb[1])

    def four_tokens(k, carry):
        t = 4 * k
        splat_weights(t + 2, wb[2])
        splat_weights(t + 3, wb[3])
        weighted_sum(t, wb[0])
        weighted_sum(t + 1, wb[1])
        splat_weights(jnp.minimum(t + 4, PEER_BLOCK - 1), wb[0])
        splat_weights(jnp.minimum(t + 5, PEER_BLOCK - 1), wb[1])
        weighted_sum(t + 2, wb[2])
        weighted_sum(t + 3, wb[3])
        return carry

    lax.fori_loop(0, PEER_BLOCK // 4, four_tokens, 0)

    def extra_groups(t, lo, hi):
        splat_weights(t, wb[0])
        accs = (f_ref[token_rows(t), :],) + (jnp.zeros((SUBLANES, LANES), F32),) * 3
        accs = lax.fori_loop(lo, hi, lambda g, a: add_group(t, g, a, lambda e: wb[0][pl.ds(e, 1), :]), accs)
        f_ref[token_rows(t), :] = (accs[0] + accs[1]) + (accs[2] + accs[3])

    _for_tokens_with_extra_groups(half, cnt_ref, tok0, extra_groups)


def _peer_v_first_kernel(cnt_ref, loc_ref, w_ref, tab_ref, f_ref, *wb):
    _peer_sum(0, cnt_ref, loc_ref, w_ref, tab_ref, lambda rows: jnp.zeros((SUBLANES, LANES), F32), f_ref, wb)


def _peer_v_second_kernel(cnt_ref, loc_ref, w_ref, tab_ref, f0_ref, f_ref, *wb):
    _peer_sum(1, cnt_ref, loc_ref, w_ref, tab_ref, lambda rows: f0_ref[rows, :], f_ref, wb)


def _peer(h2, loc0, loc1, idx_t, gate_t, count0, u_tab, v_tab, layer):
    nblk = N_TOK // PEER_BLOCK
    h_tiles = h2.reshape(N_TOK * SUBLANES, LANES)
    cnt_spec = pl.BlockSpec(memory_space=pltpu.SMEM)
    smem_spec = pl.BlockSpec((PEER_BLOCK, N_PICKS), lambda i: (i, 0), memory_space=pltpu.SMEM)
    row_spec = pl.BlockSpec((PEER_BLOCK, N_PICKS), lambda i: (i, 0))
    col_spec = pl.BlockSpec((N_PICKS, PEER_BLOCK), lambda i: (0, i))
    tile_spec = pl.BlockSpec((PEER_BLOCK * SUBLANES, LANES), lambda i: (i, 0))
    half_spec = lambda half: pl.BlockSpec((None, HALF_EXPERTS * SUBLANES, LANES),
                                          lambda i: (layer, half, 0), pipeline_mode=pl.Buffered(1))
    row_shape = jax.ShapeDtypeStruct((N_TOK, N_PICKS), F32)
    col_shape = jax.ShapeDtypeStruct((N_PICKS, N_TOK), F32)
    tile_shape = jax.ShapeDtypeStruct((N_TOK * SUBLANES, LANES), F32)
    params = pltpu.CompilerParams(vmem_limit_bytes=VMEM_LIMIT)
    pair_scratch = [pltpu.VMEM((N_PICKS, LANES), F32)] * 4

    act0_t = pl.pallas_call(
        _peer_u_first_kernel, grid=(nblk,),
        in_specs=[cnt_spec, smem_spec, tile_spec, half_spec(0)],
        out_specs=col_spec, out_shape=col_shape, scratch_shapes=pair_scratch,
        compiler_params=params, name="peer_u0",
    )(count0, loc0, h_tiles, u_tab)
    w0, w1 = pl.pallas_call(
        _peer_u_second_kernel, grid=(nblk,),
        in_specs=[cnt_spec, smem_spec, tile_spec, half_spec(1), col_spec, col_spec, col_spec],
        out_specs=[row_spec, row_spec], out_shape=[row_shape, row_shape],
        scratch_shapes=[pltpu.VMEM((N_PICKS, PEER_BLOCK), F32)] + pair_scratch,
        compiler_params=params, name="peer_u1",
    )(count0, loc1, h_tiles, u_tab, act0_t, idx_t, gate_t)
    f0 = pl.pallas_call(
        _peer_v_first_kernel, grid=(nblk,),
        in_specs=[cnt_spec, smem_spec, row_spec, half_spec(0)],
        out_specs=tile_spec, out_shape=tile_shape, scratch_shapes=pair_scratch,
        compiler_params=params, name="peer_v0",
    )(count0, loc0, w0, v_tab)
    f = pl.pallas_call(
        _peer_v_second_kernel, grid=(nblk,),
        in_specs=[cnt_spec, smem_spec, row_spec, half_spec(1), tile_spec],
        out_specs=tile_spec, out_shape=tile_shape, scratch_shapes=pair_scratch,
        compiler_params=params, name="peer_v1",
    )(count0, loc1, w1, v_tab, f0)
    return f.reshape(N_TOK, D_MODEL)


def _close_kernel(x1_ref, f_ref, mod_ref, g_ref, b_ref, o_ref):
    g2 = mod_ref[...][:, 5 * D_MODEL:6 * D_MODEL]
    o_ref[...] = _ln(ALPHA * x1_ref[...] + g2 * f_ref[...]) * g_ref[1:2, :] + b_ref[1:2, :]


def _close(x1, f, mods, ln_g, ln_b, layer):
    row_spec = pl.BlockSpec((ROW_BLOCK, D_MODEL), lambda i: (i, 0))
    vec_spec = pl.BlockSpec((None, 2, D_MODEL), lambda i: (layer, 0, 0))
    return pl.pallas_call(
        _close_kernel,
        grid=(N_ROW_BLOCKS,),
        in_specs=[row_spec, row_spec,
                  pl.BlockSpec((None, None, 1, 6 * D_MODEL), lambda i: (layer, _mod_group(i), 0, 0)),
                  vec_spec, vec_spec],
        out_specs=row_spec,
        out_shape=jax.ShapeDtypeStruct((N_TOK, D_MODEL), F32),
        compiler_params=pltpu.CompilerParams(vmem_limit_bytes=VMEM_LIMIT),
        name="close",
    )(x1, f, mods, ln_g, ln_b)


def _rope_tables(dim, copies):
    rows = LAT_LEN // GRID_W
    row = jnp.repeat(jnp.arange(rows, dtype=F32), GRID_W)
    col = jnp.tile(jnp.arange(GRID_W, dtype=F32), rows)
    nf = dim // 4
    inv = ROPE_BASE ** (-jnp.arange(nf, dtype=F32) / nf)
    ar = row[:, None] * inv
    ac = col[:, None] * inv
    ang = jnp.concatenate([ar, ar, ac, ac], axis=-1)
    cos, sin = jnp.cos(ang), jnp.sin(ang)
    even_quarter = (jnp.arange(dim) // nf) % 2 == 0
    sin_up = jnp.where(even_quarter, -sin, 0.0)
    sin_dn = jnp.where(even_quarter, 0.0, sin)
    return tuple(jnp.tile(t, (1, copies)) for t in (cos, sin_up, sin_dn))


def kernel(x_prompt, x_sample, c, cache_win_k, cache_win_v, cache_diff_k, cache_diff_v, c_ctx, w_mod, b_mod, w_in, w_out, chunk_w, chunk_b, win_sink, diff_lam_q, diff_lam_k, diff_subln_g, pool_w, pool_scale, ln_g, ln_b, peer_wq, peer_keys, peer_u, peer_v):
    cvec = jnp.concatenate([c_ctx[None, :], c, jnp.zeros((8 - 1 - N_LAT_SEQ, D_MODEL), F32)], axis=0)
    mods = _modulation(cvec, w_mod, b_mod)[:, :1 + N_LAT_SEQ].reshape(DEPTH, 1 + N_LAT_SEQ, 1, 6 * D_MODEL)

    w_in_b = w_in.astype(BF16)
    w_out_b = w_out.astype(BF16)
    wq_b = peer_wq.astype(BF16)
    cw_b = chunk_w.astype(BF16)
    cb_full = jnp.repeat(jnp.swapaxes(chunk_b, 1, 2), HEAD_D, axis=2)
    eye = jnp.eye(4, dtype=F32)
    wbd = jnp.einsum('lgcd,gh->lgchd', pool_w, eye).reshape(DEPTH, GROUP_W, GROUP_W).astype(BF16)
    caches = (cache_win_k.reshape(N_LAT_SEQ, DEPTH, CTX_LEN, 128),
              cache_win_v.reshape(N_LAT_SEQ, DEPTH, CTX_LEN, 128),
              cache_diff_k.reshape(N_LAT_SEQ, DEPTH, CTX_LEN, GROUP_W),
              cache_diff_v.reshape(N_LAT_SEQ, DEPTH, CTX_LEN, GROUP_W))
    rope_tabs = _rope_tables(HEAD_D, GROUP_W // HEAD_D) + _rope_tables(C_SUB, GROUP_W // C_SUB)
    u_tiles = peer_u.reshape(DEPTH, N_EXPERTS * SUBLANES, LANES)
    v_tiles = peer_v.reshape(DEPTH, N_EXPERTS * SUBLANES, LANES)

    x = jnp.concatenate([x_prompt.reshape(N_CTX_TOK, D_MODEL), x_sample.reshape(N_LAT_TOK, D_MODEL)], axis=0)
    kbs, vbs, kcs, vcs = [], [], [], []
    for l in range(DEPTH):
        lam_init = 0.8 - 0.6 * math.exp(-0.3 * l)
        layer_w = (cw_b[l], cb_full[l], win_sink[l], diff_lam_q[l], diff_lam_k[l],
                   diff_subln_g[l].reshape(1, HEAD_D), wbd[l], pool_scale[l].reshape(1, GROUP_W))
        p = _in_proj(x, mods, w_in_b, l)
        pc = p[:N_CTX_TOK].reshape(N_CTX_SEQ, CTX_LEN, P_COLS)
        kbs.append(pc[..., COL_BK:COL_BV].reshape(N_CTX_SEQ, CTX_LEN, 2, HEAD_D))
        vbs.append(pc[..., COL_BV:COL_CQ].reshape(N_CTX_SEQ, CTX_LEN, 2, HEAD_D))
        kcs.append(pc[..., COL_CK:COL_CV].reshape(N_CTX_SEQ, CTX_LEN, 4, 2, C_SUB))
        vcs.append(pc[..., COL_CV:COL_DZ].reshape(N_CTX_SEQ, CTX_LEN, 4, HEAD_D))
        o = jnp.concatenate([_ctx_mix(p, layer_w, lam_init),
                             _lat_mix(p, caches, rope_tabs, layer_w, lam_init, l)], axis=0)
        x1, h2, q = _out_proj(o, x, mods, w_out_b, ln_g, ln_b, wq_b, l)
        loc0, loc1, idx_t, gate_t, count0 = _route(q, peer_keys[l])
        f = _peer(h2, loc0, loc1, idx_t, gate_t, count0.reshape(N_TOK), u_tiles, v_tiles, l)
        x = _close(x1, f, mods, ln_g, ln_b, l)
    return (x[:N_CTX_TOK].reshape(N_CTX_SEQ, CTX_LEN, D_MODEL),
            x[N_CTX_TOK:].reshape(N_LAT_SEQ, LAT_LEN, D_MODEL),
            jnp.stack(kbs, axis=1), jnp.stack(vbs, axis=1), jnp.stack(kcs, axis=1), jnp.stack(vcs, axis=1))
```

```python
import functools
import math

import jax
import jax.numpy as jnp
import numpy as np
from jax import lax
from jax.experimental import pallas as pl
from jax.experimental.pallas import tpu as pltpu

F32 = jnp.float32
BF16 = jnp.bfloat16
HIGHEST = lax.Precision.HIGHEST

D_MODEL = 1024
DEPTH = 4
N_CTX_SEQ = 16
CTX_LEN = 256
N_LAT_SEQ = 2
LAT_LEN = 1024
N_CTX_TOK = N_CTX_SEQ * CTX_LEN
N_LAT_TOK = N_LAT_SEQ * LAT_LEN
N_TOK = N_CTX_TOK + N_LAT_TOK
GRID_W = 64
BLOCK = 128
GROUP_W = 256
HEAD_D = 64
C_SUB = 32
WINDOW = 128
POOL_SIZES = (2, 4, 8, 16)
P_COLS = 2048
COL_AU, COL_AV, COL_BQ, COL_BK, COL_BV, COL_CQ, COL_CK, COL_CV, COL_DZ = (
    0, 256, 512, 768, 896, 1024, 1280, 1536, 1792)
PK_HEADS = 8
N_KEYS = 128
N_EXPERTS = N_KEYS * N_KEYS
PK_TOPK = 16
N_PICKS = PK_HEADS * PK_TOPK
ROPE_BASE = 10000.0
ALPHA = (2 * DEPTH) ** 0.25
LN_EPS = 1e-5
NEG = -1e30

ROW_BLOCK = 512
N_ROW_BLOCKS = N_TOK // ROW_BLOCK
N_CTX_ROW_BLOCKS = N_CTX_TOK // ROW_BLOCK
ROW_BLOCKS_PER_LAT_SEQ = LAT_LEN // ROW_BLOCK
ROUTE_BLOCK = 128
PEER_BLOCK = 128
SUBLANES = 8
LANES = 128
HALF_EXPERTS = N_EXPERTS // 2
VMEM_LIMIT = 56 * 1024 * 1024


def _mm(a, b):
    return jnp.dot(a.astype(BF16), b.astype(BF16), preferred_element_type=F32)


def _mm_nt(a, b):
    return lax.dot_general(a.astype(BF16), b.astype(BF16), (((1,), (1,)), ((), ())),
                           preferred_element_type=F32)


def _gelu(x):
    return x * (0.5 * (1.0 + jnp.tanh(0.7978845608028654 * (x + 0.044715 * (x * x * x)))))


def _ln(x):
    mu = jnp.mean(x, axis=-1, keepdims=True)
    xc = x - mu
    var = jnp.mean(xc * xc, axis=-1, keepdims=True)
    return xc * lax.rsqrt(var + LN_EPS)


def _mod_group(i):
    return jnp.where(i < N_CTX_ROW_BLOCKS, 0, 1 + (i - N_CTX_ROW_BLOCKS) // ROW_BLOCKS_PER_LAT_SEQ)


def _mod_kernel(c_ref, w_ref, b_ref, o_ref):
    c = c_ref[...]
    s = c * jax.nn.sigmoid(c)
    o_ref[...] = jnp.dot(s, w_ref[...], precision=HIGHEST, preferred_element_type=F32) + b_ref[...]


def _modulation(cvec, w_mod, b_mod):
    nj = 4
    cols = 6 * D_MODEL // nj
    return pl.pallas_call(
        _mod_kernel,
        grid=(DEPTH, nj),
        in_specs=[pl.BlockSpec((8, D_MODEL), lambda l, j: (0, 0)),
                  pl.BlockSpec((None, D_MODEL, cols), lambda l, j: (l, 0, j)),
                  pl.BlockSpec((None, 1, cols), lambda l, j: (l, 0, j))],
        out_specs=pl.BlockSpec((None, 8, cols), lambda l, j: (l, 0, j)),
        out_shape=jax.ShapeDtypeStruct((DEPTH, 8, 6 * D_MODEL), F32),
        compiler_params=pltpu.CompilerParams(vmem_limit_bytes=VMEM_LIMIT),
        name="modulation",
    )(cvec, w_mod, b_mod.reshape(DEPTH, 1, 6 * D_MODEL))


def _in_proj_kernel(x_ref, mod_ref, w_ref, o_ref):
    m = mod_ref[...]
    h = _ln(x_ref[...]) * (1.0 + m[:, D_MODEL:2 * D_MODEL]) + m[:, 0:D_MODEL]
    o_ref[...] = jnp.dot(h.astype(BF16), w_ref[...], preferred_element_type=F32)


def _in_proj(x, mods, w_in_bf16, layer):
    return pl.pallas_call(
        _in_proj_kernel,
        grid=(N_ROW_BLOCKS,),
        in_specs=[pl.BlockSpec((ROW_BLOCK, D_MODEL), lambda i: (i, 0)),
                  pl.BlockSpec((None, None, 1, 6 * D_MODEL), lambda i: (layer, _mod_group(i), 0, 0)),
                  pl.BlockSpec((None, D_MODEL, P_COLS), lambda i: (layer, 0, 0))],
        out_specs=pl.BlockSpec((ROW_BLOCK, P_COLS), lambda i: (i, 0)),
        out_shape=jax.ShapeDtypeStruct((N_TOK, P_COLS), F32),
        compiler_params=pltpu.CompilerParams(vmem_limit_bytes=VMEM_LIMIT),
        name="in_proj",
    )(x, mods, w_in_bf16)


def _chunk_gating(u_raw, v_raw, cw_ref, cb_full):
    u = _gelu(u_raw)
    v = _gelu(v_raw).astype(BF16)
    head = lax.broadcasted_iota(jnp.int32, (BLOCK, GROUP_W), 1) // HEAD_D
    mixed = cb_full
    for h in range(4):
        mixed = mixed + jnp.where(head == h, jnp.dot(cw_ref[h], v, preferred_element_type=F32), 0.0)
    return u * mixed


def _pool(z_rows, z_win, row0, key0, seq_len, wbd_ref, pool_scale):
    r, k = z_rows.shape[0], z_win.shape[0]
    t = row0 + lax.broadcasted_iota(jnp.int32, (r, k), 0)
    pos = key0 + lax.broadcasted_iota(jnp.int32, (r, k), 1)
    t1 = row0 + lax.broadcasted_iota(jnp.int32, (r, 1), 0)
    z_hi = z_win.astype(BF16)
    z_lo = (z_win - z_hi.astype(F32)).astype(BF16)
    group = lax.broadcasted_iota(jnp.int32, (r, GROUP_W), 1) // HEAD_D
    mean = jnp.zeros((r, GROUP_W), F32)
    for g, w in enumerate(POOL_SIZES):
        lo = jnp.maximum(t - w // 2, 0)
        hi = jnp.minimum(t + w // 2, seq_len)
        band = jnp.where((pos >= lo) & (pos < hi), 1.0, 0.0).astype(BF16)
        tot = (jnp.dot(band, z_hi, preferred_element_type=F32)
               + jnp.dot(band, z_lo, preferred_element_type=F32))
        cnt = (jnp.minimum(t1 + w // 2, seq_len) - jnp.maximum(t1 - w // 2, 0)).astype(F32)
        mean = mean + jnp.where(group == g, tot / cnt, 0.0)
    pooled = mean - z_rows
    return jnp.dot(pooled.astype(BF16), wbd_ref[...], preferred_element_type=F32) * pool_scale


def _diff_lambda(lq_ref, lk_ref, lam_init):
    lq = lq_ref[...]
    lk = lk_ref[...]
    s0 = jnp.sum(lq[0:1] * lk[0:1], axis=-1, keepdims=True)
    s1 = jnp.sum(lq[1:2] * lk[1:2], axis=-1, keepdims=True)
    return jnp.exp(s0) - jnp.exp(s1) + lam_init


def _sub_norm(o, g, lam_init):
    ms = jnp.mean(o * o, axis=-1, keepdims=True)
    return o * lax.rsqrt(ms + LN_EPS) * g * (1.0 - lam_init)


def _ctx_mix_kernel(lam_init, p_ref, cw_ref, cb_ref, sink_ref, lq_ref, lk_ref, sg_ref, wbd_ref,
                    ps_ref, o_ref):
    s_len = CTX_LEN
    for n in range(s_len // BLOCK):
        rows = slice(n * BLOCK, (n + 1) * BLOCK)
        o_ref[rows, 0:GROUP_W] = _chunk_gating(p_ref[rows, COL_AU:COL_AU + GROUP_W],
                                               p_ref[rows, COL_AV:COL_AV + GROUP_W], cw_ref, cb_ref[...])
    for kv in range(2):
        kk = p_ref[:, COL_BK + kv * HEAD_D:COL_BK + (kv + 1) * HEAD_D].astype(BF16)
        vv = p_ref[:, COL_BV + kv * HEAD_D:COL_BV + (kv + 1) * HEAD_D].astype(BF16)
        for g in range(2):
            h = kv * 2 + g
            q = p_ref[:, COL_BQ + h * HEAD_D:COL_BQ + (h + 1) * HEAD_D] * (HEAD_D ** -0.5)
            s = _mm_nt(q, kk)
            sk = sink_ref[h]
            m = jnp.maximum(jnp.max(s, axis=-1, keepdims=True), sk)
            e = jnp.exp(s - m)
            den = jnp.sum(e, axis=-1, keepdims=True) + jnp.exp(sk - m)
            o_ref[:, GROUP_W + h * HEAD_D:GROUP_W + (h + 1) * HEAD_D] = _mm(e, vv) / den
    lam = _diff_lambda(lq_ref, lk_ref, lam_init)
    scale = C_SUB ** -0.5
    for h in range(4):
        vv = p_ref[:, COL_CV + h * HEAD_D:COL_CV + (h + 1) * HEAD_D].astype(BF16)
        parts = []
        for j in range(2):
            c0 = h * HEAD_D + j * C_SUB
            s = _mm_nt(p_ref[:, COL_CQ + c0:COL_CQ + c0 + C_SUB],
                       p_ref[:, COL_CK + c0:COL_CK + c0 + C_SUB]) * scale
            e = jnp.exp(s - jnp.max(s, axis=-1, keepdims=True))
            parts.append(_mm(e, vv) / jnp.sum(e, axis=-1, keepdims=True))
        o = parts[0] - lam * parts[1]
        o_ref[:, 2 * GROUP_W + h * HEAD_D:2 * GROUP_W + (h + 1) * HEAD_D] = _sub_norm(o, sg_ref[...], lam_init)
    z = p_ref[:, COL_DZ:COL_DZ + GROUP_W]
    o_ref[:, 3 * GROUP_W:4 * GROUP_W] = _pool(z, z, 0, 0, s_len, wbd_ref, ps_ref[...])


def _whole(shape):
    nd = len(shape)
    return pl.BlockSpec(shape, lambda *_: (0,) * nd)


def _ctx_mix(p, layer_w, lam_init):
    cw, cb_full, sink, lq, lk, sg, wbd, ps = layer_w
    return pl.pallas_call(
        functools.partial(_ctx_mix_kernel, lam_init),
        grid=(N_CTX_SEQ,),
        in_specs=[pl.BlockSpec((CTX_LEN, P_COLS), lambda b: (b, 0)),
                  _whole(cw.shape), _whole(cb_full.shape),
                  pl.BlockSpec(memory_space=pltpu.SMEM),
                  _whole(lq.shape), _whole(lk.shape), _whole(sg.shape), _whole(wbd.shape),
                  _whole(ps.shape)],
        out_specs=pl.BlockSpec((CTX_LEN, D_MODEL), lambda b: (b, 0)),
        out_shape=jax.ShapeDtypeStruct((N_CTX_TOK, D_MODEL), F32),
        compiler_params=pltpu.CompilerParams(vmem_limit_bytes=VMEM_LIMIT),
        name="ctx_mix",
    )(p, cw, cb_full, sink, lq, lk, sg, wbd, ps)


def _rope(x, cos, sin_up, sin_dn, shift):
    outs = []
    for c0 in range(0, x.shape[1], LANES):
        xs = x[:, c0:c0 + LANES]
        up = pltpu.roll(xs, LANES - shift, 1)
        dn = pltpu.roll(xs, shift, 1)
        outs.append(xs * cos[:, c0:c0 + LANES] + up * sin_up[:, c0:c0 + LANES]
                    + dn * sin_dn[:, c0:c0 + LANES])
    return outs[0] if len(outs) == 1 else jnp.concatenate(outs, axis=1)


def _lat_mix_kernel(lam_init, p_ref, wk_ref, wv_ref, dk_ref, dv_ref,
                    cosb_ref, sbu_ref, sbd_ref, cosc_ref, scu_ref, scd_ref,
                    cw_ref, cb_ref, sink_ref, lq_ref, lk_ref, sg_ref, wbd_ref, ps_ref,
                    o_ref, kb_s, kc_s):
    n = pl.program_id(1)
    s_len = LAT_LEN

    @pl.when(n == 0)
    def _():
        kb_s[...] = _rope(p_ref[:, COL_BK:COL_BK + 128], cosb_ref[:, 0:128], sbu_ref[:, 0:128],
                          sbd_ref[:, 0:128], HEAD_D // 4).astype(BF16)
        kc_s[...] = _rope(p_ref[:, COL_CK:COL_CK + GROUP_W], cosc_ref[...], scu_ref[...],
                          scd_ref[...], C_SUB // 4).astype(BF16)

    r0 = pl.multiple_of(n * BLOCK, BLOCK)
    rows = pl.ds(r0, BLOCK)
    o_ref[:, 0:GROUP_W] = _chunk_gating(p_ref[rows, COL_AU:COL_AU + GROUP_W],
                                        p_ref[rows, COL_AV:COL_AV + GROUP_W], cw_ref, cb_ref[...])
    w0 = pl.multiple_of(jnp.clip((n - 1) * BLOCK, 0, s_len - 3 * BLOCK), BLOCK)
    win = pl.ds(w0, 3 * BLOCK)
    qb = _rope(p_ref[rows, COL_BQ:COL_BQ + GROUP_W], cosb_ref[rows, :], sbu_ref[rows, :],
               sbd_ref[rows, :], HEAD_D // 4) * (HEAD_D ** -0.5)
    qpos = r0 + lax.broadcasted_iota(jnp.int32, (BLOCK, 3 * BLOCK), 0)
    kpos = w0 + lax.broadcasted_iota(jnp.int32, (BLOCK, 3 * BLOCK), 1)
    valid = jnp.abs(kpos - qpos) <= WINDOW
    for kv in range(2):
        cs = slice(kv * HEAD_D, (kv + 1) * HEAD_D)
        k_loc = kb_s[win, cs]
        v_loc = p_ref[win, COL_BV + kv * HEAD_D:COL_BV + (kv + 1) * HEAD_D].astype(BF16)
        k_ctx = wk_ref[:, cs].astype(BF16)
        v_ctx = wv_ref[:, cs].astype(BF16)
        for g in range(2):
            h = kv * 2 + g
            q = qb[:, h * HEAD_D:(h + 1) * HEAD_D]
            s_loc = jnp.where(valid, _mm_nt(q, k_loc), NEG)
            s_ctx = _mm_nt(q, k_ctx)
            sk = sink_ref[h]
            m = jnp.maximum(jnp.maximum(jnp.max(s_loc, axis=-1, keepdims=True),
                                        jnp.max(s_ctx, axis=-1, keepdims=True)), sk)
            e_loc = jnp.exp(s_loc - m)
            e_ctx = jnp.exp(s_ctx - m)
            den = (jnp.sum(e_loc, axis=-1, keepdims=True) + jnp.sum(e_ctx, axis=-1, keepdims=True)
                   + jnp.exp(sk - m))
            o_ref[:, GROUP_W + h * HEAD_D:GROUP_W + (h + 1) * HEAD_D] = (
                _mm(e_loc, v_loc) + _mm(e_ctx, v_ctx)) / den
    lam = _diff_lambda(lq_ref, lk_ref, lam_init)
    scale = C_SUB ** -0.5
    qc = _rope(p_ref[rows, COL_CQ:COL_CQ + GROUP_W], cosc_ref[rows, :], scu_ref[rows, :],
               scd_ref[rows, :], C_SUB // 4)
    for h in range(4):
        v_ctx = dv_ref[:, h * HEAD_D:(h + 1) * HEAD_D].astype(BF16)
        v_lat = p_ref[:, COL_CV + h * HEAD_D:COL_CV + (h + 1) * HEAD_D].astype(BF16)
        parts = []
        for j in range(2):
            cs = slice(h * HEAD_D + j * C_SUB, h * HEAD_D + (j + 1) * C_SUB)
            q = qc[:, cs]
            s_ctx = _mm_nt(q, dk_ref[:, cs]) * scale
            s_lat = _mm_nt(q, kc_s[:, cs]) * scale
            m = jnp.maximum(jnp.max(s_ctx, axis=-1, keepdims=True), jnp.max(s_lat, axis=-1, keepdims=True))
            e_ctx = jnp.exp(s_ctx - m)
            e_lat = jnp.exp(s_lat - m)
            den = jnp.sum(e_ctx, axis=-1, keepdims=True) + jnp.sum(e_lat, axis=-1, keepdims=True)
            parts.append((_mm(e_ctx, v_ctx) + _mm(e_lat, v_lat)) / den)
        o = parts[0] - lam * parts[1]
        o_ref[:, 2 * GROUP_W + h * HEAD_D:2 * GROUP_W + (h + 1) * HEAD_D] = _sub_norm(o, sg_ref[...], lam_init)
    o_ref[:, 3 * GROUP_W:4 * GROUP_W] = _pool(p_ref[rows, COL_DZ:COL_DZ + GROUP_W],
                                              p_ref[win, COL_DZ:COL_DZ + GROUP_W],
                                              r0, w0, s_len, wbd_ref, ps_ref[...])


def _lat_mix(p, caches, rope_tabs, layer_w, lam_init, layer):
    cw, cb_full, sink, lq, lk, sg, wbd, ps = layer_w
    nb = LAT_LEN // BLOCK
    ctx_blocks = N_CTX_TOK // LAT_LEN
    cache_spec = lambda width: pl.BlockSpec((None, None, CTX_LEN, width), lambda b, n: (b, layer, 0, 0))
    rope_spec = pl.BlockSpec((LAT_LEN, GROUP_W), lambda b, n: (0, 0))
    return pl.pallas_call(
        functools.partial(_lat_mix_kernel, lam_init),
        grid=(N_LAT_SEQ, nb),
        in_specs=[pl.BlockSpec((LAT_LEN, P_COLS), lambda b, n: (ctx_blocks + b, 0)),
                  cache_spec(128), cache_spec(128), cache_spec(256), cache_spec(256)]
                 + [rope_spec] * 6
                 + [_whole(cw.shape), _whole(cb_full.shape), pl.BlockSpec(memory_space=pltpu.SMEM),
                    _whole(lq.shape), _whole(lk.shape), _whole(sg.shape), _whole(wbd.shape),
                    _whole(ps.shape)],
        out_specs=pl.BlockSpec((BLOCK, D_MODEL), lambda b, n: (b * nb + n, 0)),
        out_shape=jax.ShapeDtypeStruct((N_LAT_TOK, D_MODEL), F32),
        scratch_shapes=[pltpu.VMEM((LAT_LEN, 128), BF16), pltpu.VMEM((LAT_LEN, GROUP_W), BF16)],
        compiler_params=pltpu.CompilerParams(vmem_limit_bytes=VMEM_LIMIT,
                                             dimension_semantics=("arbitrary", "arbitrary")),
        name="lat_mix",
    )(p, *caches, *rope_tabs, cw, cb_full, sink, lq, lk, sg, wbd, ps)


def _out_proj_kernel(o_ref, x_ref, mod_ref, wo_ref, g_ref, b_ref, wq_ref, x1_ref, h2_ref, q_ref):
    m = mod_ref[...]
    g1 = m[:, 2 * D_MODEL:3 * D_MODEL]
    sh2 = m[:, 3 * D_MODEL:4 * D_MODEL]
    sc2 = m[:, 4 * D_MODEL:5 * D_MODEL]
    y = jnp.dot(o_ref[...].astype(BF16), wo_ref[...], preferred_element_type=F32)
    x1 = _ln(ALPHA * x_ref[...] + g1 * y) * g_ref[0:1, :] + b_ref[0:1, :]
    x1_ref[...] = x1
    h2 = _ln(x1) * (1.0 + sc2) + sh2
    h2_ref[...] = h2
    q_ref[...] = jnp.dot(h2.astype(BF16), wq_ref[...], preferred_element_type=F32)


def _out_proj(o, x, mods, w_out_bf16, ln_g, ln_b, wq_bf16, layer):
    row_spec = pl.BlockSpec((ROW_BLOCK, D_MODEL), lambda i: (i, 0))
    mat_spec = pl.BlockSpec((None, D_MODEL, D_MODEL), lambda i: (layer, 0, 0))
    vec_spec = pl.BlockSpec((None, 2, D_MODEL), lambda i: (layer, 0, 0))
    shape = jax.ShapeDtypeStruct((N_TOK, D_MODEL), F32)
    return pl.pallas_call(
        _out_proj_kernel,
        grid=(N_ROW_BLOCKS,),
        in_specs=[row_spec, row_spec,
                  pl.BlockSpec((None, None, 1, 6 * D_MODEL), lambda i: (layer, _mod_group(i), 0, 0)),
                  mat_spec, vec_spec, vec_spec, mat_spec],
        out_specs=[row_spec, row_spec, row_spec],
        out_shape=[shape, shape, shape],
        compiler_params=pltpu.CompilerParams(vmem_limit_bytes=VMEM_LIMIT),
        name="out_proj",
    )(o, x, mods, w_out_bf16, ln_g, ln_b, wq_bf16)


def _top16_rows(sc, order, payload=None):
    out_row = lax.broadcasted_iota(jnp.int32, (PK_TOPK, sc.shape[1]), 0)
    vals = jnp.zeros((PK_TOPK, sc.shape[1]), F32)
    sel = jnp.zeros((PK_TOPK, sc.shape[1]), jnp.int32)
    for i in range(PK_TOPK):
        m = jnp.max(sc, axis=0, keepdims=True)
        am = jnp.min(jnp.where(sc == m, order, jnp.int32(2 ** 30)), axis=0, keepdims=True)
        hit = order == am
        if payload is None:
            picked = am
        else:
            picked = jnp.sum(jnp.where(hit, payload, 0), axis=0, keepdims=True)
        vals = jnp.where(out_row == i, m, vals)
        sel = jnp.where(out_row == i, picked, sel)
        sc = jnp.where(hit, NEG, sc)
    return vals, sel


def _pair_candidates(sv, si):
    n_tok = sv[0].shape[1]
    rows = lax.broadcasted_iota(jnp.int32, (SUBLANES, n_tok), 0)
    cands, flats, experts = [], [], []

    def piece(a, b0, valid):
        v = sv[0][a:a + 1, :] + sv[1][b0:b0 + SUBLANES, :]
        cands.append(v if valid >= SUBLANES else jnp.where(rows < valid, v, NEG))
        flats.append(a * PK_TOPK + b0 + rows)
        experts.append(si[0][a:a + 1, :] * N_KEYS + si[1][b0:b0 + SUBLANES, :])

    piece(0, 0, SUBLANES)
    piece(0, SUBLANES, SUBLANES)
    for a in range(1, SUBLANES):
        piece(a, 0, PK_TOPK // (a + 1))
    cands.append(sv[0][SUBLANES:, :] + sv[1][0:1, :])
    flats.append((SUBLANES + rows) * PK_TOPK)
    experts.append(si[0][SUBLANES:, :] * N_KEYS + si[1][0:1, :])
    return (jnp.concatenate(cands, axis=0), jnp.concatenate(flats, axis=0),
            jnp.concatenate(experts, axis=0))


def _half_offsets(idx, half):
    local = idx - half * HALF_EXPERTS
    inside = (local >= 0) & (local < HALF_EXPERTS)
    return jnp.where(inside, local, 0) * SUBLANES


def _route_kernel(q_ref, keys_ref, loc0_ref, loc1_ref, idxt_ref, gatet_ref, count0_ref,
                  slot_s, idx_s, gate_s):
    n_tok = q_ref.shape[0]
    key_row = lax.broadcasted_iota(jnp.int32, (N_KEYS, n_tok), 0)
    for h in range(PK_HEADS):
        sv = []
        si = []
        for j in range(2):
            c0 = (h * 2 + j) * HEAD_D
            sc = lax.dot_general(keys_ref[j], q_ref[:, c0:c0 + HEAD_D], (((1,), (1,)), ((), ())),
                                 precision=HIGHEST, preferred_element_type=F32)
            v, i = _top16_rows(sc, key_row)
            sv.append(v)
            si.append(i)
        cv, ce = _top16_rows(*_pair_candidates(sv, si))
        e = jnp.exp(cv - jnp.max(cv, axis=0, keepdims=True))
        gate_s[h * PK_TOPK:(h + 1) * PK_TOPK, :] = e / jnp.sum(e, axis=0, keepdims=True)
        idx_s[h * PK_TOPK:(h + 1) * PK_TOPK, :] = ce
    in0 = idx_s[...] < HALF_EXPERTS
    before = (lax.broadcasted_iota(jnp.int32, (N_PICKS, N_PICKS), 1)
              < lax.broadcasted_iota(jnp.int32, (N_PICKS, N_PICKS), 0))
    rank0 = jnp.dot(jnp.where(before, 1.0, 0.0).astype(BF16), jnp.where(in0, 1.0, 0.0).astype(BF16),
                    preferred_element_type=F32).astype(jnp.int32)
    count0 = jnp.sum(jnp.where(in0, 1, 0), axis=0, keepdims=True)
    pick = lax.broadcasted_iota(jnp.int32, (N_PICKS, n_tok), 0)
    slot_s[...] = jnp.where(in0, rank0, count0 + pick - rank0)
    idx_sorted = jnp.zeros((N_PICKS, n_tok), jnp.int32)
    gate_sorted = jnp.zeros((N_PICKS, n_tok), F32)
    for e in range(N_PICKS):
        hit = pick == slot_s[e:e + 1, :]
        idx_sorted = jnp.where(hit, idx_s[e:e + 1, :], idx_sorted)
        gate_sorted = jnp.where(hit, gate_s[e:e + 1, :], gate_sorted)
    idxt_ref[...] = idx_sorted
    gatet_ref[...] = gate_sorted
    count0_ref[...] = count0
    idx = idx_sorted.astype(F32).T.astype(jnp.int32)
    loc0_ref[...] = _half_offsets(idx, 0)
    loc1_ref[...] = _half_offsets(idx, 1)


def _route(q, keys):
    row_spec = pl.BlockSpec((ROUTE_BLOCK, N_PICKS), lambda i: (i, 0))
    col_spec = pl.BlockSpec((N_PICKS, ROUTE_BLOCK), lambda i: (0, i))
    return pl.pallas_call(
        _route_kernel,
        grid=(N_TOK // ROUTE_BLOCK,),
        in_specs=[pl.BlockSpec((ROUTE_BLOCK, D_MODEL), lambda i: (i, 0)), _whole(keys.shape)],
        out_specs=[row_spec, row_spec, col_spec, col_spec,
                   pl.BlockSpec((1, ROUTE_BLOCK), lambda i: (0, i))],
        out_shape=[jax.ShapeDtypeStruct((N_TOK, N_PICKS), jnp.int32),
                   jax.ShapeDtypeStruct((N_TOK, N_PICKS), jnp.int32),
                   jax.ShapeDtypeStruct((N_PICKS, N_TOK), jnp.int32),
                   jax.ShapeDtypeStruct((N_PICKS, N_TOK), F32),
                   jax.ShapeDtypeStruct((1, N_TOK), jnp.int32)],
        scratch_shapes=[pltpu.VMEM((N_PICKS, ROUTE_BLOCK), jnp.int32),
                        pltpu.VMEM((N_PICKS, ROUTE_BLOCK), jnp.int32),
                        pltpu.VMEM((N_PICKS, ROUTE_BLOCK), F32)],
        compiler_params=pltpu.CompilerParams(vmem_limit_bytes=VMEM_LIMIT),
        name="route",
    )(q, keys)


N_GROUPS = N_PICKS // SUBLANES
WALK_GROUPS = 10


def _group_ranges(half, count0):
    if half == 0:
        return range(WALK_GROUPS), (WALK_GROUPS, (count0 + SUBLANES - 1) // SUBLANES)
    return range(N_GROUPS - WALK_GROUPS, N_GROUPS), (count0 // SUBLANES, N_GROUPS - WALK_GROUPS)


_MERGE_ORDER = (0, 4, 2, 6, 1, 5, 3, 7)


def _merge8(tiles, sub):
    m4 = sub < 4
    m2 = (sub & 3) < 2
    m1 = (sub & 1) == 0

    def l1(a, b):
        return jnp.where(m4, a, b) + pltpu.roll(jnp.where(m4, b, a), 4, 0)

    def l2(c, d):
        return jnp.where(m2, c, d) + jnp.where(m2, pltpu.roll(c, 6, 0), pltpu.roll(d, 2, 0))

    def l3(e, f):
        return jnp.where(m1, e, f) + jnp.where(m1, pltpu.roll(e, 7, 0), pltpu.roll(f, 1, 0))

    c = [l1(tiles[2 * i], tiles[2 * i + 1]) for i in range(4)]
    return l3(l2(c[0], c[1]), l2(c[2], c[3]))


def _peer_dots(half, cnt_ref, loc_ref, h_ref, tab_ref, actt_ref, gm):
    sub = lax.broadcasted_iota(jnp.int32, (SUBLANES, LANES), 0)
    lane = lax.broadcasted_iota(jnp.int32, (N_PICKS, PEER_BLOCK), 1)
    tok0 = pl.program_id(0) * PEER_BLOCK
    actt_ref[...] = jnp.zeros((N_PICKS, PEER_BLOCK), F32)
    for buf in gm:
        buf[...] = jnp.zeros((N_PICKS, LANES), F32)

    def group_sums(t, g):
        ht = h_ref[pl.ds(pl.multiple_of(t * SUBLANES, SUBLANES), SUBLANES), :]
        loc_row = loc_ref.at[t]
        prods = []
        for s in range(SUBLANES):
            r = pl.multiple_of(loc_row[g * SUBLANES + _MERGE_ORDER[s]], SUBLANES)
            prods.append(tab_ref[pl.ds(r, SUBLANES), :] * ht)
        return _merge8(prods, sub)

    def partial_sums(t, buf):
        for g in _group_ranges(half, 0)[0]:
            buf[g * SUBLANES:(g + 1) * SUBLANES, :] = group_sums(t, g)

    def extra_partial_sums(t, buf):
        def one(g, carry):
            buf[pl.ds(pl.multiple_of(g * SUBLANES, SUBLANES), SUBLANES), :] = group_sums(t, g)
            return carry

        lax.fori_loop(*_group_ranges(half, cnt_ref[tok0 + t])[1], one, 0)

    def reduce_into_column(t, buf):
        col = jnp.sum(buf[...], axis=1, keepdims=True)
        actt_ref[...] = jnp.where(lane == t, col, actt_ref[...])

    def four_tokens(k, carry):
        t = 4 * k
        reduce_into_column(t - 2, gm[2])
        reduce_into_column(t - 1, gm[3])
        partial_sums(t, gm[0])
        partial_sums(t + 1, gm[1])
        extra_partial_sums(t, gm[0])
        extra_partial_sums(t + 1, gm[1])
        reduce_into_column(t, gm[0])
        reduce_into_column(t + 1, gm[1])
        partial_sums(t + 2, gm[2])
        partial_sums(t + 3, gm[3])
        extra_partial_sums(t + 2, gm[2])
        extra_partial_sums(t + 3, gm[3])
        return carry

    lax.fori_loop(0, PEER_BLOCK // 4, four_tokens, 0)
    reduce_into_column(PEER_BLOCK - 2, gm[2])
    reduce_into_column(PEER_BLOCK - 1, gm[3])


def _peer_u_first_kernel(cnt_ref, loc_ref, h_ref, tab_ref, actt_ref, *gm):
    _peer_dots(0, cnt_ref, loc_ref, h_ref, tab_ref, actt_ref, gm)


def _peer_u_second_kernel(cnt_ref, loc_ref, h_ref, tab_ref, act0t_ref, idxt_ref, gatet_ref, w0_ref, w1_ref,
                          actt_s, *gm):
    _peer_dots(1, cnt_ref, loc_ref, h_ref, tab_ref, actt_s, gm)
    in0 = idxt_ref[...] < HALF_EXPERTS
    w = gatet_ref[...] * _gelu(jnp.where(in0, act0t_ref[...], actt_s[...]))
    w0_ref[...] = jnp.where(in0, w, 0.0)
    w1_ref[...] = jnp.where(in0, 0.0, w)


def _peer_sum(half, cnt_ref, loc_ref, w_ref, tab_ref, init, f_ref, wb):
    tok0 = pl.program_id(0) * PEER_BLOCK

    def splat_weights(t, buf):
        buf[...] = jnp.take_along_axis(w_ref[...], jnp.full((N_PICKS, PEER_BLOCK), t, jnp.int32), axis=1)

    def add_group(t, g, accs, weight_row):
        loc_row = loc_ref.at[t]
        accs = list(accs)
        for s in range(SUBLANES):
            r = pl.multiple_of(loc_row[g * SUBLANES + s], SUBLANES)
            w = jnp.broadcast_to(weight_row(g * SUBLANES + s), (SUBLANES, LANES))
            accs[s % 4] = accs[s % 4] + tab_ref[pl.ds(r, SUBLANES), :] * w
        return tuple(accs)

    def token_rows(t):
        return pl.ds(pl.multiple_of(t * SUBLANES, SUBLANES), SUBLANES)

    def weighted_sum(t, buf):
        accs = (init(token_rows(t)),) + (jnp.zeros((SUBLANES, LANES), F32),) * 3
        for g in _group_ranges(half, 0)[0]:
            accs = add_group(t, g, accs, lambda e: buf[e:e + 1, :])
        return accs

    def finish(t, buf, accs):
        accs = lax.fori_loop(*_group_ranges(half, cnt_ref[tok0 + t])[1],
                             lambda g, a: add_group(t, g, a, lambda e: buf[pl.ds(e, 1), :]), accs)
        f_ref[token_rows(t), :] = (accs[0] + accs[1]) + (accs[2] + accs[3])

    def token_pair(t, bufs, next_t, next_bufs):
        splat_weights(jnp.minimum(next_t, PEER_BLOCK - 1), next_bufs[0])
        splat_weights(jnp.minimum(next_t + 1, PEER_BLOCK - 1), next_bufs[1])
        first = weighted_sum(t, bufs[0])
        second = weighted_sum(t + 1, bufs[1])
        finish(t, bufs[0], first)
        finish(t + 1, bufs[1], second)

    splat_weights(0, wb[0])
    splat_weights(1, wb[1])

    def four_tokens(k, carry):
        t = 4 * k
        token_pair(t, wb[0:2], t + 2, wb[2:4])
        token_pair(t + 2, wb[2:4], t + 4, wb[0:2])
        return carry

    lax.fori_loop(0, PEER_BLOCK // 4, four_tokens, 0)


def _peer_v_first_kernel(cnt_ref, loc_ref, w_ref, tab_ref, f_ref, *wb):
    _peer_sum(0, cnt_ref, loc_ref, w_ref, tab_ref, lambda rows: jnp.zeros((SUBLANES, LANES), F32), f_ref, wb)


def _peer_v_second_kernel(cnt_ref, loc_ref, w_ref, tab_ref, f0_ref, f_ref, *wb):
    _peer_sum(1, cnt_ref, loc_ref, w_ref, tab_ref, lambda rows: f0_ref[rows, :], f_ref, wb)


def _peer(h2, loc0, loc1, idx_t, gate_t, count0, u_tab, v_tab, layer):
    nblk = N_TOK // PEER_BLOCK
    h_tiles = h2.reshape(N_TOK * SUBLANES, LANES)
    cnt_spec = pl.BlockSpec(memory_space=pltpu.SMEM)
    smem_spec = pl.BlockSpec((PEER_BLOCK, N_PICKS), lambda i: (i, 0), memory_space=pltpu.SMEM)
    row_spec = pl.BlockSpec((PEER_BLOCK, N_PICKS), lambda i: (i, 0))
    col_spec = pl.BlockSpec((N_PICKS, PEER_BLOCK), lambda i: (0, i))
    tile_spec = pl.BlockSpec((PEER_BLOCK * SUBLANES, LANES), lambda i: (i, 0))
    half_spec = lambda half: pl.BlockSpec((None, HALF_EXPERTS * SUBLANES, LANES),
                                          lambda i: (layer, half, 0), pipeline_mode=pl.Buffered(1))
    row_shape = jax.ShapeDtypeStruct((N_TOK, N_PICKS), F32)
    col_shape = jax.ShapeDtypeStruct((N_PICKS, N_TOK), F32)
    tile_shape = jax.ShapeDtypeStruct((N_TOK * SUBLANES, LANES), F32)
    params = pltpu.CompilerParams(vmem_limit_bytes=VMEM_LIMIT)
    pair_scratch = [pltpu.VMEM((N_PICKS, LANES), F32)] * 4

    act0_t = pl.pallas_call(
        _peer_u_first_kernel, grid=(nblk,),
        in_specs=[cnt_spec, smem_spec, tile_spec, half_spec(0)],
        out_specs=col_spec, out_shape=col_shape, scratch_shapes=pair_scratch,
        compiler_params=params, name="peer_u0",
    )(count0, loc0, h_tiles, u_tab)
    w0, w1 = pl.pallas_call(
        _peer_u_second_kernel, grid=(nblk,),
        in_specs=[cnt_spec, smem_spec, tile_spec, half_spec(1), col_spec, col_spec, col_spec],
        out_specs=[col_spec, col_spec], out_shape=[col_shape, col_shape],
        scratch_shapes=[pltpu.VMEM((N_PICKS, PEER_BLOCK), F32)] + pair_scratch,
        compiler_params=params, name="peer_u1",
    )(count0, loc1, h_tiles, u_tab, act0_t, idx_t, gate_t)
    f0 = pl.pallas_call(
        _peer_v_first_kernel, grid=(nblk,),
        in_specs=[cnt_spec, smem_spec, col_spec, half_spec(0)],
        out_specs=tile_spec, out_shape=tile_shape, scratch_shapes=pair_scratch,
        compiler_params=params, name="peer_v0",
    )(count0, loc0, w0, v_tab)
    f = pl.pallas_call(
        _peer_v_second_kernel, grid=(nblk,),
        in_specs=[cnt_spec, smem_spec, col_spec, half_spec(1), tile_spec],
        out_specs=tile_spec, out_shape=tile_shape, scratch_shapes=pair_scratch,
        compiler_params=params, name="peer_v1",
    )(count0, loc1, w1, v_tab, f0)
    return f.reshape(N_TOK, D_MODEL)


def _close_kernel(x1_ref, f_ref, mod_ref, g_ref, b_ref, o_ref):
    g2 = mod_ref[...][:, 5 * D_MODEL:6 * D_MODEL]
    o_ref[...] = _ln(ALPHA * x1_ref[...] + g2 * f_ref[...]) * g_ref[1:2, :] + b_ref[1:2, :]


def _close(x1, f, mods, ln_g, ln_b, layer):
    row_spec = pl.BlockSpec((ROW_BLOCK, D_MODEL), lambda i: (i, 0))
    vec_spec = pl.BlockSpec((None, 2, D_MODEL), lambda i: (layer, 0, 0))
    return pl.pallas_call(
        _close_kernel,
        grid=(N_ROW_BLOCKS,),
        in_specs=[row_spec, row_spec,
                  pl.BlockSpec((None, None, 1, 6 * D_MODEL), lambda i: (layer, _mod_group(i), 0, 0)),
                  vec_spec, vec_spec],
        out_specs=row_spec,
        out_shape=jax.ShapeDtypeStruct((N_TOK, D_MODEL), F32),
        compiler_params=pltpu.CompilerParams(vmem_limit_bytes=VMEM_LIMIT),
        name="close",
    )(x1, f, mods, ln_g, ln_b)


def _rope_tables(dim, copies):
    rows = LAT_LEN // GRID_W
    row = jnp.repeat(jnp.arange(rows, dtype=F32), GRID_W)
    col = jnp.tile(jnp.arange(GRID_W, dtype=F32), rows)
    nf = dim // 4
    inv = ROPE_BASE ** (-jnp.arange(nf, dtype=F32) / nf)
    ar = row[:, None] * inv
    ac = col[:, None] * inv
    ang = jnp.concatenate([ar, ar, ac, ac], axis=-1)
    cos, sin = jnp.cos(ang), jnp.sin(ang)
    even_quarter = (jnp.arange(dim) // nf) % 2 == 0
    sin_up = jnp.where(even_quarter, -sin, 0.0)
    sin_dn = jnp.where(even_quarter, 0.0, sin)
    return tuple(jnp.tile(t, (1, copies)) for t in (cos, sin_up, sin_dn))


def kernel(x_prompt, x_sample, c, cache_win_k, cache_win_v, cache_diff_k, cache_diff_v, c_ctx, w_mod, b_mod, w_in, w_out, chunk_w, chunk_b, win_sink, diff_lam_q, diff_lam_k, diff_subln_g, pool_w, pool_scale, ln_g, ln_b, peer_wq, peer_keys, peer_u, peer_v):
    cvec = jnp.concatenate([c_ctx[None, :], c, jnp.zeros((8 - 1 - N_LAT_SEQ, D_MODEL), F32)], axis=0)
    mods = _modulation(cvec, w_mod, b_mod)[:, :1 + N_LAT_SEQ].reshape(DEPTH, 1 + N_LAT_SEQ, 1, 6 * D_MODEL)

    w_in_b = w_in.astype(BF16)
    w_out_b = w_out.astype(BF16)
    wq_b = peer_wq.astype(BF16)
    cw_b = chunk_w.astype(BF16)
    cb_full = jnp.repeat(jnp.swapaxes(chunk_b, 1, 2), HEAD_D, axis=2)
    eye = jnp.eye(4, dtype=F32)
    wbd = jnp.einsum('lgcd,gh->lgchd', pool_w, eye).reshape(DEPTH, GROUP_W, GROUP_W).astype(BF16)
    caches = (cache_win_k.reshape(N_LAT_SEQ, DEPTH, CTX_LEN, 128),
              cache_win_v.reshape(N_LAT_SEQ, DEPTH, CTX_LEN, 128),
              cache_diff_k.reshape(N_LAT_SEQ, DEPTH, CTX_LEN, GROUP_W),
              cache_diff_v.reshape(N_LAT_SEQ, DEPTH, CTX_LEN, GROUP_W))
    rope_tabs = _rope_tables(HEAD_D, GROUP_W // HEAD_D) + _rope_tables(C_SUB, GROUP_W // C_SUB)
    u_tiles = peer_u.reshape(DEPTH, N_EXPERTS * SUBLANES, LANES)
    v_tiles = peer_v.reshape(DEPTH, N_EXPERTS * SUBLANES, LANES)

    x = jnp.concatenate([x_prompt.reshape(N_CTX_TOK, D_MODEL), x_sample.reshape(N_LAT_TOK, D_MODEL)], axis=0)
    kbs, vbs, kcs, vcs = [], [], [], []
    for l in range(DEPTH):
        lam_init = 0.8 - 0.6 * math.exp(-0.3 * l)
        layer_w = (cw_b[l], cb_full[l], win_sink[l], diff_lam_q[l], diff_lam_k[l],
                   diff_subln_g[l].reshape(1, HEAD_D), wbd[l], pool_scale[l].reshape(1, GROUP_W))
        p = _in_proj(x, mods, w_in_b, l)
        pc = p[:N_CTX_TOK].reshape(N_CTX_SEQ, CTX_LEN, P_COLS)
        kbs.append(pc[..., COL_BK:COL_BV].reshape(N_CTX_SEQ, CTX_LEN, 2, HEAD_D))
        vbs.append(pc[..., COL_BV:COL_CQ].reshape(N_CTX_SEQ, CTX_LEN, 2, HEAD_D))
        kcs.append(pc[..., COL_CK:COL_CV].reshape(N_CTX_SEQ, CTX_LEN, 4, 2, C_SUB))
        vcs.append(pc[..., COL_CV:COL_DZ].reshape(N_CTX_SEQ, CTX_LEN, 4, HEAD_D))
        o = jnp.concatenate([_ctx_mix(p, layer_w, lam_init),
                             _lat_mix(p, caches, rope_tabs, layer_w, lam_init, l)], axis=0)
        x1, h2, q = _out_proj(o, x, mods, w_out_b, ln_g, ln_b, wq_b, l)
        loc0, loc1, idx_t, gate_t, count0 = _route(q, peer_keys[l])
        f = _peer(h2, loc0, loc1, idx_t, gate_t, count0.reshape(N_TOK), u_tiles, v_tiles, l)
        x = _close(x1, f, mods, ln_g, ln_b, l)
    return (x[:N_CTX_TOK].reshape(N_CTX_SEQ, CTX_LEN, D_MODEL),
            x[N_CTX_TOK:].reshape(N_LAT_SEQ, LAT_LEN, D_MODEL),
            jnp.stack(kbs, axis=1), jnp.stack(vbs, axis=1), jnp.stack(kcs, axis=1), jnp.stack(vcs, axis=1))
```

```python
import functools
import math

import jax
import jax.numpy as jnp
import numpy as np
from jax import lax
from jax.experimental import pallas as pl
from jax.experimental.pallas import tpu as pltpu

F32 = jnp.float32
BF16 = jnp.bfloat16
HIGHEST = lax.Precision.HIGHEST

D_MODEL = 1024
DEPTH = 4
N_CTX_SEQ = 16
CTX_LEN = 256
N_LAT_SEQ = 2
LAT_LEN = 1024
N_CTX_TOK = N_CTX_SEQ * CTX_LEN
N_LAT_TOK = N_LAT_SEQ * LAT_LEN
N_TOK = N_CTX_TOK + N_LAT_TOK
GRID_W = 64
BLOCK = 128
GROUP_W = 256
HEAD_D = 64
C_SUB = 32
WINDOW = 128
POOL_SIZES = (2, 4, 8, 16)
P_COLS = 2048
COL_AU, COL_AV, COL_BQ, COL_BK, COL_BV, COL_CQ, COL_CK, COL_CV, COL_DZ = (
    0, 256, 512, 768, 896, 1024, 1280, 1536, 1792)
PK_HEADS = 8
N_KEYS = 128
N_EXPERTS = N_KEYS * N_KEYS
PK_TOPK = 16
N_PICKS = PK_HEADS * PK_TOPK
ROPE_BASE = 10000.0
ALPHA = (2 * DEPTH) ** 0.25
LN_EPS = 1e-5
NEG = -1e30

ROW_BLOCK = 512
N_ROW_BLOCKS = N_TOK // ROW_BLOCK
N_CTX_ROW_BLOCKS = N_CTX_TOK // ROW_BLOCK
ROW_BLOCKS_PER_LAT_SEQ = LAT_LEN // ROW_BLOCK
ROUTE_BLOCK = 128
PEER_BLOCK = 128
SUBLANES = 8
LANES = 128
HALF_EXPERTS = N_EXPERTS // 2
VMEM_LIMIT = 56 * 1024 * 1024


def _mm(a, b):
    return jnp.dot(a.astype(BF16), b.astype(BF16), preferred_element_type=F32)


def _mm_nt(a, b):
    return lax.dot_general(a.astype(BF16), b.astype(BF16), (((1,), (1,)), ((), ())),
                           preferred_element_type=F32)


def _gelu(x):
    return x * (0.5 * (1.0 + jnp.tanh(0.7978845608028654 * (x + 0.044715 * (x * x * x)))))


def _ln(x):
    mu = jnp.mean(x, axis=-1, keepdims=True)
    xc = x - mu
    var = jnp.mean(xc * xc, axis=-1, keepdims=True)
    return xc * lax.rsqrt(var + LN_EPS)


def _mod_group(i):
    return jnp.where(i < N_CTX_ROW_BLOCKS, 0, 1 + (i - N_CTX_ROW_BLOCKS) // ROW_BLOCKS_PER_LAT_SEQ)


def _mod_kernel(c_ref, w_ref, b_ref, o_ref):
    c = c_ref[...]
    s = c * jax.nn.sigmoid(c)
    o_ref[...] = jnp.dot(s, w_ref[...], precision=HIGHEST, preferred_element_type=F32) + b_ref[...]


def _modulation(cvec, w_mod, b_mod):
    nj = 4
    cols = 6 * D_MODEL // nj
    return pl.pallas_call(
        _mod_kernel,
        grid=(DEPTH, nj),
        in_specs=[pl.BlockSpec((8, D_MODEL), lambda l, j: (0, 0)),
                  pl.BlockSpec((None, D_MODEL, cols), lambda l, j: (l, 0, j)),
                  pl.BlockSpec((None, 1, cols), lambda l, j: (l, 0, j))],
        out_specs=pl.BlockSpec((None, 8, cols), lambda l, j: (l, 0, j)),
        out_shape=jax.ShapeDtypeStruct((DEPTH, 8, 6 * D_MODEL), F32),
        compiler_params=pltpu.CompilerParams(vmem_limit_bytes=VMEM_LIMIT),
        name="modulation",
    )(cvec, w_mod, b_mod.reshape(DEPTH, 1, 6 * D_MODEL))


def _in_proj_kernel(x_ref, mod_ref, w_ref, o_ref):
    m = mod_ref[...]
    h = _ln(x_ref[...]) * (1.0 + m[:, D_MODEL:2 * D_MODEL]) + m[:, 0:D_MODEL]
    o_ref[...] = jnp.dot(h.astype(BF16), w_ref[...], preferred_element_type=F32)


def _in_proj(x, mods, w_in_bf16, layer):
    return pl.pallas_call(
        _in_proj_kernel,
        grid=(N_ROW_BLOCKS,),
        in_specs=[pl.BlockSpec((ROW_BLOCK, D_MODEL), lambda i: (i, 0)),
                  pl.BlockSpec((None, None, 1, 6 * D_MODEL), lambda i: (layer, _mod_group(i), 0, 0)),
                  pl.BlockSpec((None, D_MODEL, P_COLS), lambda i: (layer, 0, 0))],
        out_specs=pl.BlockSpec((ROW_BLOCK, P_COLS), lambda i: (i, 0)),
        out_shape=jax.ShapeDtypeStruct((N_TOK, P_COLS), F32),
        compiler_params=pltpu.CompilerParams(vmem_limit_bytes=VMEM_LIMIT),
        name="in_proj",
    )(x, mods, w_in_bf16)


def _chunk_gating(u_raw, v_raw, cw_ref, cb_full):
    u = _gelu(u_raw)
    v = _gelu(v_raw).astype(BF16)
    head = lax.broadcasted_iota(jnp.int32, (BLOCK, GROUP_W), 1) // HEAD_D
    mixed = cb_full
    for h in range(4):
        mixed = mixed + jnp.where(head == h, jnp.dot(cw_ref[h], v, preferred_element_type=F32), 0.0)
    return u * mixed


def _pool(z_rows, z_win, row0, key0, seq_len, wbd_ref, pool_scale):
    r, k = z_rows.shape[0], z_win.shape[0]
    t = row0 + lax.broadcasted_iota(jnp.int32, (r, k), 0)
    pos = key0 + lax.broadcasted_iota(jnp.int32, (r, k), 1)
    t1 = row0 + lax.broadcasted_iota(jnp.int32, (r, 1), 0)
    z_hi = z_win.astype(BF16)
    z_lo = (z_win - z_hi.astype(F32)).astype(BF16)
    group = lax.broadcasted_iota(jnp.int32, (r, GROUP_W), 1) // HEAD_D
    mean = jnp.zeros((r, GROUP_W), F32)
    for g, w in enumerate(POOL_SIZES):
        lo = jnp.maximum(t - w // 2, 0)
        hi = jnp.minimum(t + w // 2, seq_len)
        band = jnp.where((pos >= lo) & (pos < hi), 1.0, 0.0).astype(BF16)
        tot = (jnp.dot(band, z_hi, preferred_element_type=F32)
               + jnp.dot(band, z_lo, preferred_element_type=F32))
        cnt = (jnp.minimum(t1 + w // 2, seq_len) - jnp.maximum(t1 - w // 2, 0)).astype(F32)
        mean = mean + jnp.where(group == g, tot / cnt, 0.0)
    pooled = mean - z_rows
    return jnp.dot(pooled.astype(BF16), wbd_ref[...], preferred_element_type=F32) * pool_scale


def _diff_lambda(lq_ref, lk_ref, lam_init):
    lq = lq_ref[...]
    lk = lk_ref[...]
    s0 = jnp.sum(lq[0:1] * lk[0:1], axis=-1, keepdims=True)
    s1 = jnp.sum(lq[1:2] * lk[1:2], axis=-1, keepdims=True)
    return jnp.exp(s0) - jnp.exp(s1) + lam_init


def _sub_norm(o, g, lam_init):
    ms = jnp.mean(o * o, axis=-1, keepdims=True)
    return o * lax.rsqrt(ms + LN_EPS) * g * (1.0 - lam_init)


def _ctx_mix_kernel(lam_init, p_ref, cw_ref, cb_ref, sink_ref, lq_ref, lk_ref, sg_ref, wbd_ref,
                    ps_ref, o_ref):
    s_len = CTX_LEN
    for n in range(s_len // BLOCK):
        rows = slice(n * BLOCK, (n + 1) * BLOCK)
        o_ref[rows, 0:GROUP_W] = _chunk_gating(p_ref[rows, COL_AU:COL_AU + GROUP_W],
                                               p_ref[rows, COL_AV:COL_AV + GROUP_W], cw_ref, cb_ref[...])
    for kv in range(2):
        kk = p_ref[:, COL_BK + kv * HEAD_D:COL_BK + (kv + 1) * HEAD_D].astype(BF16)
        vv = p_ref[:, COL_BV + kv * HEAD_D:COL_BV + (kv + 1) * HEAD_D].astype(BF16)
        for g in range(2):
            h = kv * 2 + g
            q = p_ref[:, COL_BQ + h * HEAD_D:COL_BQ + (h + 1) * HEAD_D] * (HEAD_D ** -0.5)
            s = _mm_nt(q, kk)
            sk = sink_ref[h]
            m = jnp.maximum(jnp.max(s, axis=-1, keepdims=True), sk)
            e = jnp.exp(s - m)
            den = jnp.sum(e, axis=-1, keepdims=True) + jnp.exp(sk - m)
            o_ref[:, GROUP_W + h * HEAD_D:GROUP_W + (h + 1) * HEAD_D] = _mm(e, vv) / den
    lam = _diff_lambda(lq_ref, lk_ref, lam_init)
    scale = C_SUB ** -0.5
    for h in range(4):
        vv = p_ref[:, COL_CV + h * HEAD_D:COL_CV + (h + 1) * HEAD_D].astype(BF16)
        parts = []
        for j in range(2):
            c0 = h * HEAD_D + j * C_SUB
            s = _mm_nt(p_ref[:, COL_CQ + c0:COL_CQ + c0 + C_SUB],
                       p_ref[:, COL_CK + c0:COL_CK + c0 + C_SUB]) * scale
            e = jnp.exp(s - jnp.max(s, axis=-1, keepdims=True))
            parts.append(_mm(e, vv) / jnp.sum(e, axis=-1, keepdims=True))
        o = parts[0] - lam * parts[1]
        o_ref[:, 2 * GROUP_W + h * HEAD_D:2 * GROUP_W + (h + 1) * HEAD_D] = _sub_norm(o, sg_ref[...], lam_init)
    z = p_ref[:, COL_DZ:COL_DZ + GROUP_W]
    o_ref[:, 3 * GROUP_W:4 * GROUP_W] = _pool(z, z, 0, 0, s_len, wbd_ref, ps_ref[...])


def _whole(shape):
    nd = len(shape)
    return pl.BlockSpec(shape, lambda *_: (0,) * nd)


def _ctx_mix(p, layer_w, lam_init):
    cw, cb_full, sink, lq, lk, sg, wbd, ps = layer_w
    return pl.pallas_call(
        functools.partial(_ctx_mix_kernel, lam_init),
        grid=(N_CTX_SEQ,),
        in_specs=[pl.BlockSpec((CTX_LEN, P_COLS), lambda b: (b, 0)),
                  _whole(cw.shape), _whole(cb_full.shape),
                  pl.BlockSpec(memory_space=pltpu.SMEM),
                  _whole(lq.shape), _whole(lk.shape), _whole(sg.shape), _whole(wbd.shape),
                  _whole(ps.shape)],
        out_specs=pl.BlockSpec((CTX_LEN, D_MODEL), lambda b: (b, 0)),
        out_shape=jax.ShapeDtypeStruct((N_CTX_TOK, D_MODEL), F32),
        compiler_params=pltpu.CompilerParams(vmem_limit_bytes=VMEM_LIMIT),
        name="ctx_mix",
    )(p, cw, cb_full, sink, lq, lk, sg, wbd, ps)


def _rope(x, cos, sin_up, sin_dn, shift):
    outs = []
    for c0 in range(0, x.shape[1], LANES):
        xs = x[:, c0:c0 + LANES]
        up = pltpu.roll(xs, LANES - shift, 1)
        dn = pltpu.roll(xs, shift, 1)
        outs.append(xs * cos[:, c0:c0 + LANES] + up * sin_up[:, c0:c0 + LANES]
                    + dn * sin_dn[:, c0:c0 + LANES])
    return outs[0] if len(outs) == 1 else jnp.concatenate(outs, axis=1)


def _lat_mix_kernel(lam_init, p_ref, wk_ref, wv_ref, dk_ref, dv_ref,
                    cosb_ref, sbu_ref, sbd_ref, cosc_ref, scu_ref, scd_ref,
                    cw_ref, cb_ref, sink_ref, lq_ref, lk_ref, sg_ref, wbd_ref, ps_ref,
                    o_ref, kb_s, kc_s):
    n = pl.program_id(1)
    s_len = LAT_LEN

    @pl.when(n == 0)
    def _():
        kb_s[...] = _rope(p_ref[:, COL_BK:COL_BK + 128], cosb_ref[:, 0:128], sbu_ref[:, 0:128],
                          sbd_ref[:, 0:128], HEAD_D // 4).astype(BF16)
        kc_s[...] = _rope(p_ref[:, COL_CK:COL_CK + GROUP_W], cosc_ref[...], scu_ref[...],
                          scd_ref[...], C_SUB // 4).astype(BF16)

    r0 = pl.multiple_of(n * BLOCK, BLOCK)
    rows = pl.ds(r0, BLOCK)
    o_ref[:, 0:GROUP_W] = _chunk_gating(p_ref[rows, COL_AU:COL_AU + GROUP_W],
                                        p_ref[rows, COL_AV:COL_AV + GROUP_W], cw_ref, cb_ref[...])
    w0 = pl.multiple_of(jnp.clip((n - 1) * BLOCK, 0, s_len - 3 * BLOCK), BLOCK)
    win = pl.ds(w0, 3 * BLOCK)
    qb = _rope(p_ref[rows, COL_BQ:COL_BQ + GROUP_W], cosb_ref[rows, :], sbu_ref[rows, :],
               sbd_ref[rows, :], HEAD_D // 4) * (HEAD_D ** -0.5)
    qpos = r0 + lax.broadcasted_iota(jnp.int32, (BLOCK, 3 * BLOCK), 0)
    kpos = w0 + lax.broadcasted_iota(jnp.int32, (BLOCK, 3 * BLOCK), 1)
    valid = jnp.abs(kpos - qpos) <= WINDOW
    for kv in range(2):
        cs = slice(kv * HEAD_D, (kv + 1) * HEAD_D)
        k_loc = kb_s[win, cs]
        v_loc = p_ref[win, COL_BV + kv * HEAD_D:COL_BV + (kv + 1) * HEAD_D].astype(BF16)
        k_ctx = wk_ref[:, cs].astype(BF16)
        v_ctx = wv_ref[:, cs].astype(BF16)
        for g in range(2):
            h = kv * 2 + g
            q = qb[:, h * HEAD_D:(h + 1) * HEAD_D]
            s_loc = jnp.where(valid, _mm_nt(q, k_loc), NEG)
            s_ctx = _mm_nt(q, k_ctx)
            sk = sink_ref[h]
            m = jnp.maximum(jnp.maximum(jnp.max(s_loc, axis=-1, keepdims=True),
                                        jnp.max(s_ctx, axis=-1, keepdims=True)), sk)
            e_loc = jnp.exp(s_loc - m)
            e_ctx = jnp.exp(s_ctx - m)
            den = (jnp.sum(e_loc, axis=-1, keepdims=True) + jnp.sum(e_ctx, axis=-1, keepdims=True)
                   + jnp.exp(sk - m))
            o_ref[:, GROUP_W + h * HEAD_D:GROUP_W + (h + 1) * HEAD_D] = (
                _mm(e_loc, v_loc) + _mm(e_ctx, v_ctx)) / den
    lam = _diff_lambda(lq_ref, lk_ref, lam_init)
    scale = C_SUB ** -0.5
    qc = _rope(p_ref[rows, COL_CQ:COL_CQ + GROUP_W], cosc_ref[rows, :], scu_ref[rows, :],
               scd_ref[rows, :], C_SUB // 4)
    for h in range(4):
        v_ctx = dv_ref[:, h * HEAD_D:(h + 1) * HEAD_D].astype(BF16)
        v_lat = p_ref[:, COL_CV + h * HEAD_D:COL_CV + (h + 1) * HEAD_D].astype(BF16)
        parts = []
        for j in range(2):
            cs = slice(h * HEAD_D + j * C_SUB, h * HEAD_D + (j + 1) * C_SUB)
            q = qc[:, cs]
            s_ctx = _mm_nt(q, dk_ref[:, cs]) * scale
            s_lat = _mm_nt(q, kc_s[:, cs]) * scale
            m = jnp.maximum(jnp.max(s_ctx, axis=-1, keepdims=True), jnp.max(s_lat, axis=-1, keepdims=True))
            e_ctx = jnp.exp(s_ctx - m)
            e_lat = jnp.exp(s_lat - m)
            den = jnp.sum(e_ctx, axis=-1, keepdims=True) + jnp.sum(e_lat, axis=-1, keepdims=True)
            parts.append((_mm(e_ctx, v_ctx) + _mm(e_lat, v_lat)) / den)
        o = parts[0] - lam * parts[1]
        o_ref[:, 2 * GROUP_W + h * HEAD_D:2 * GROUP_W + (h + 1) * HEAD_D] = _sub_norm(o, sg_ref[...], lam_init)
    o_ref[:, 3 * GROUP_W:4 * GROUP_W] = _pool(p_ref[rows, COL_DZ:COL_DZ + GROUP_W],
                                              p_ref[win, COL_DZ:COL_DZ + GROUP_W],
                                              r0, w0, s_len, wbd_ref, ps_ref[...])


def _lat_mix(p, caches, rope_tabs, layer_w, lam_init, layer):
    cw, cb_full, sink, lq, lk, sg, wbd, ps = layer_w
    nb = LAT_LEN // BLOCK
    ctx_blocks = N_CTX_TOK // LAT_LEN
    cache_spec = lambda width: pl.BlockSpec((None, None, CTX_LEN, width), lambda b, n: (b, layer, 0, 0))
    rope_spec = pl.BlockSpec((LAT_LEN, GROUP_W), lambda b, n: (0, 0))
    return pl.pallas_call(
        functools.partial(_lat_mix_kernel, lam_init),
        grid=(N_LAT_SEQ, nb),
        in_specs=[pl.BlockSpec((LAT_LEN, P_COLS), lambda b, n: (ctx_blocks + b, 0)),
                  cache_spec(128), cache_spec(128), cache_spec(256), cache_spec(256)]
                 + [rope_spec] * 6
                 + [_whole(cw.shape), _whole(cb_full.shape), pl.BlockSpec(memory_space=pltpu.SMEM),
                    _whole(lq.shape), _whole(lk.shape), _whole(sg.shape), _whole(wbd.shape),
                    _whole(ps.shape)],
        out_specs=pl.BlockSpec((BLOCK, D_MODEL), lambda b, n: (b * nb + n, 0)),
        out_shape=jax.ShapeDtypeStruct((N_LAT_TOK, D_MODEL), F32),
        scratch_shapes=[pltpu.VMEM((LAT_LEN, 128), BF16), pltpu.VMEM((LAT_LEN, GROUP_W), BF16)],
        compiler_params=pltpu.CompilerParams(vmem_limit_bytes=VMEM_LIMIT,
                                             dimension_semantics=("arbitrary", "arbitrary")),
        name="lat_mix",
    )(p, *caches, *rope_tabs, cw, cb_full, sink, lq, lk, sg, wbd, ps)


def _out_proj_kernel(o_ref, x_ref, mod_ref, wo_ref, g_ref, b_ref, wq_ref, x1_ref, h2_ref, q_ref):
    m = mod_ref[...]
    g1 = m[:, 2 * D_MODEL:3 * D_MODEL]
    sh2 = m[:, 3 * D_MODEL:4 * D_MODEL]
    sc2 = m[:, 4 * D_MODEL:5 * D_MODEL]
    y = jnp.dot(o_ref[...].astype(BF16), wo_ref[...], preferred_element_type=F32)
    x1 = _ln(ALPHA * x_ref[...] + g1 * y) * g_ref[0:1, :] + b_ref[0:1, :]
    x1_ref[...] = x1
    h2 = _ln(x1) * (1.0 + sc2) + sh2
    h2_ref[...] = h2
    q_ref[...] = jnp.dot(h2.astype(BF16), wq_ref[...], preferred_element_type=F32)


def _out_proj(o, x, mods, w_out_bf16, ln_g, ln_b, wq_bf16, layer):
    row_spec = pl.BlockSpec((ROW_BLOCK, D_MODEL), lambda i: (i, 0))
    mat_spec = pl.BlockSpec((None, D_MODEL, D_MODEL), lambda i: (layer, 0, 0))
    vec_spec = pl.BlockSpec((None, 2, D_MODEL), lambda i: (layer, 0, 0))
    shape = jax.ShapeDtypeStruct((N_TOK, D_MODEL), F32)
    return pl.pallas_call(
        _out_proj_kernel,
        grid=(N_ROW_BLOCKS,),
        in_specs=[row_spec, row_spec,
                  pl.BlockSpec((None, None, 1, 6 * D_MODEL), lambda i: (layer, _mod_group(i), 0, 0)),
                  mat_spec, vec_spec, vec_spec, mat_spec],
        out_specs=[row_spec, row_spec, row_spec],
        out_shape=[shape, shape, shape],
        compiler_params=pltpu.CompilerParams(vmem_limit_bytes=VMEM_LIMIT),
        name="out_proj",
    )(o, x, mods, w_out_bf16, ln_g, ln_b, wq_bf16)


def _top16_rows(sc, order):
    out_row = lax.broadcasted_iota(jnp.int32, (PK_TOPK, sc.shape[1]), 0)
    vals = jnp.zeros((PK_TOPK, sc.shape[1]), F32)
    sel = jnp.zeros((PK_TOPK, sc.shape[1]), jnp.int32)
    for i in range(PK_TOPK):
        m = jnp.max(sc, axis=0, keepdims=True)
        am = jnp.min(jnp.where(sc == m, order, jnp.int32(2 ** 30)), axis=0, keepdims=True)
        vals = jnp.where(out_row == i, m, vals)
        sel = jnp.where(out_row == i, am, sel)
        sc = jnp.where(order == am, NEG, sc)
    return vals, sel


def _pair_candidates(sv, si):
    n_tok = sv[0].shape[1]
    rows = lax.broadcasted_iota(jnp.int32, (SUBLANES, n_tok), 0)
    cands, flats, experts = [], [], []

    def piece(a, b0, valid):
        v = sv[0][a:a + 1, :] + sv[1][b0:b0 + SUBLANES, :]
        cands.append(v if valid >= SUBLANES else jnp.where(rows < valid, v, NEG))
        flats.append(a * PK_TOPK + b0 + rows)
        experts.append(si[0][a:a + 1, :] * N_KEYS + si[1][b0:b0 + SUBLANES, :])

    piece(0, 0, SUBLANES)
    piece(0, SUBLANES, SUBLANES)
    for a in range(1, SUBLANES):
        piece(a, 0, PK_TOPK // (a + 1))
    cands.append(sv[0][SUBLANES:, :] + sv[1][0:1, :])
    flats.append((SUBLANES + rows) * PK_TOPK)
    experts.append(si[0][SUBLANES:, :] * N_KEYS + si[1][0:1, :])
    order = jnp.concatenate(flats, axis=0) * N_EXPERTS + jnp.concatenate(experts, axis=0)
    return jnp.concatenate(cands, axis=0), order


def _half_offsets(idx, half):
    local = idx - half * HALF_EXPERTS
    inside = (local >= 0) & (local < HALF_EXPERTS)
    return jnp.where(inside, local, 0) * SUBLANES


def _route_kernel(q_ref, keys_ref, loc0_ref, loc1_ref, idxt_ref, gatet_ref, count0_ref,
                  slot_s, idx_s, gate_s):
    n_tok = q_ref.shape[0]
    key_row = lax.broadcasted_iota(jnp.int32, (N_KEYS, n_tok), 0)
    for h in range(PK_HEADS):
        sv = []
        si = []
        for j in range(2):
            c0 = (h * 2 + j) * HEAD_D
            sc = lax.dot_general(keys_ref[j], q_ref[:, c0:c0 + HEAD_D], (((1,), (1,)), ((), ())),
                                 precision=HIGHEST, preferred_element_type=F32)
            v, i = _top16_rows(sc, key_row)
            sv.append(v)
            si.append(i)
        cv, winners = _top16_rows(*_pair_candidates(sv, si))
        ce = winners & (N_EXPERTS - 1)
        e = jnp.exp(cv - jnp.max(cv, axis=0, keepdims=True))
        gate_s[h * PK_TOPK:(h + 1) * PK_TOPK, :] = e / jnp.sum(e, axis=0, keepdims=True)
        idx_s[h * PK_TOPK:(h + 1) * PK_TOPK, :] = ce
    in0 = idx_s[...] < HALF_EXPERTS
    before = (lax.broadcasted_iota(jnp.int32, (N_PICKS, N_PICKS), 1)
              < lax.broadcasted_iota(jnp.int32, (N_PICKS, N_PICKS), 0))
    rank0 = jnp.dot(jnp.where(before, 1.0, 0.0).astype(BF16), jnp.where(in0, 1.0, 0.0).astype(BF16),
                    preferred_element_type=F32).astype(jnp.int32)
    count0 = jnp.sum(jnp.where(in0, 1, 0), axis=0, keepdims=True)
    pick = lax.broadcasted_iota(jnp.int32, (N_PICKS, n_tok), 0)
    slot_s[...] = jnp.where(in0, rank0, count0 + pick - rank0)
    idx_sorted = jnp.zeros((N_PICKS, n_tok), jnp.int32)
    gate_sorted = jnp.zeros((N_PICKS, n_tok), F32)
    for e in range(N_PICKS):
        hit = pick == slot_s[e:e + 1, :]
        idx_sorted = jnp.where(hit, idx_s[e:e + 1, :], idx_sorted)
        gate_sorted = jnp.where(hit, gate_s[e:e + 1, :], gate_sorted)
    idxt_ref[...] = idx_sorted
    gatet_ref[...] = gate_sorted
    count0_ref[...] = count0
    idx = idx_sorted.astype(F32).T.astype(jnp.int32)
    loc0_ref[...] = _half_offsets(idx, 0)
    loc1_ref[...] = _half_offsets(idx, 1)


def _route(q, keys):
    row_spec = pl.BlockSpec((ROUTE_BLOCK, N_PICKS), lambda i: (i, 0))
    col_spec = pl.BlockSpec((N_PICKS, ROUTE_BLOCK), lambda i: (0, i))
    return pl.pallas_call(
        _route_kernel,
        grid=(N_TOK // ROUTE_BLOCK,),
        in_specs=[pl.BlockSpec((ROUTE_BLOCK, D_MODEL), lambda i: (i, 0)), _whole(keys.shape)],
        out_specs=[row_spec, row_spec, col_spec, col_spec,
                   pl.BlockSpec((1, ROUTE_BLOCK), lambda i: (0, i))],
        out_shape=[jax.ShapeDtypeStruct((N_TOK, N_PICKS), jnp.int32),
                   jax.ShapeDtypeStruct((N_TOK, N_PICKS), jnp.int32),
                   jax.ShapeDtypeStruct((N_PICKS, N_TOK), jnp.int32),
                   jax.ShapeDtypeStruct((N_PICKS, N_TOK), F32),
                   jax.ShapeDtypeStruct((1, N_TOK), jnp.int32)],
        scratch_shapes=[pltpu.VMEM((N_PICKS, ROUTE_BLOCK), jnp.int32),
                        pltpu.VMEM((N_PICKS, ROUTE_BLOCK), jnp.int32),
                        pltpu.VMEM((N_PICKS, ROUTE_BLOCK), F32)],
        compiler_params=pltpu.CompilerParams(vmem_limit_bytes=VMEM_LIMIT),
        name="route",
    )(q, keys)


N_GROUPS = N_PICKS // SUBLANES
WALK_GROUPS = 10


def _group_ranges(half, count0):
    if half == 0:
        return range(WALK_GROUPS), (WALK_GROUPS, (count0 + SUBLANES - 1) // SUBLANES)
    return range(N_GROUPS - WALK_GROUPS, N_GROUPS), (count0 // SUBLANES, N_GROUPS - WALK_GROUPS)


_MERGE_ORDER = (0, 4, 2, 6, 1, 5, 3, 7)


def _merge8(tiles, sub):
    m4 = sub < 4
    m2 = (sub & 3) < 2
    m1 = (sub & 1) == 0

    def l1(a, b):
        return jnp.where(m4, a, b) + pltpu.roll(jnp.where(m4, b, a), 4, 0)

    def l2(c, d):
        return jnp.where(m2, c, d) + jnp.where(m2, pltpu.roll(c, 6, 0), pltpu.roll(d, 2, 0))

    def l3(e, f):
        return jnp.where(m1, e, f) + jnp.where(m1, pltpu.roll(e, 7, 0), pltpu.roll(f, 1, 0))

    c = [l1(tiles[2 * i], tiles[2 * i + 1]) for i in range(4)]
    return l3(l2(c[0], c[1]), l2(c[2], c[3]))


def _peer_dots(half, cnt_ref, loc_ref, h_ref, tab_ref, actt_ref, gm):
    sub = lax.broadcasted_iota(jnp.int32, (SUBLANES, LANES), 0)
    lane = lax.broadcasted_iota(jnp.int32, (N_PICKS, PEER_BLOCK), 1)
    tok0 = pl.program_id(0) * PEER_BLOCK
    actt_ref[...] = jnp.zeros((N_PICKS, PEER_BLOCK), F32)
    for buf in gm:
        buf[...] = jnp.zeros((N_PICKS, LANES), F32)

    def group_sums(t, g):
        ht = h_ref[pl.ds(pl.multiple_of(t * SUBLANES, SUBLANES), SUBLANES), :]
        loc_row = loc_ref.at[t]
        prods = []
        for s in range(SUBLANES):
            r = pl.multiple_of(loc_row[g * SUBLANES + _MERGE_ORDER[s]], SUBLANES)
            prods.append(tab_ref[pl.ds(r, SUBLANES), :] * ht)
        return _merge8(prods, sub)

    def partial_sums(t, buf):
        for g in _group_ranges(half, 0)[0]:
            buf[g * SUBLANES:(g + 1) * SUBLANES, :] = group_sums(t, g)

    def extra_partial_sums(t, bufs):
        bounds = [_group_ranges(half, cnt_ref[tok0 + t + i])[1] for i in range(2)]

        @pl.when((bounds[0][1] > bounds[0][0]) | (bounds[1][1] > bounds[1][0]))
        def _():
            for i in range(2):
                def one(g, carry, i=i):
                    bufs[i][pl.ds(pl.multiple_of(g * SUBLANES, SUBLANES), SUBLANES), :] = group_sums(t + i, g)
                    return carry

                lax.fori_loop(*bounds[i], one, 0)

    def reduce_into_column(t, buf):
        col = jnp.sum(buf[...], axis=1, keepdims=True)
        actt_ref[...] = jnp.where(lane == t, col, actt_ref[...])

    def four_tokens(k, carry):
        t = 4 * k
        reduce_into_column(t - 2, gm[2])
        reduce_into_column(t - 1, gm[3])
        partial_sums(t, gm[0])
        partial_sums(t + 1, gm[1])
        extra_partial_sums(t, gm[0:2])
        reduce_into_column(t, gm[0])
        reduce_into_column(t + 1, gm[1])
        partial_sums(t + 2, gm[2])
        partial_sums(t + 3, gm[3])
        extra_partial_sums(t + 2, gm[2:4])
        return carry

    lax.fori_loop(0, PEER_BLOCK // 4, four_tokens, 0)
    reduce_into_column(PEER_BLOCK - 2, gm[2])
    reduce_into_column(PEER_BLOCK - 1, gm[3])


def _peer_u_first_kernel(cnt_ref, loc_ref, h_ref, tab_ref, actt_ref, *gm):
    _peer_dots(0, cnt_ref, loc_ref, h_ref, tab_ref, actt_ref, gm)


def _peer_u_second_kernel(cnt_ref, loc_ref, h_ref, tab_ref, act0t_ref, idxt_ref, gatet_ref, w0_ref, w1_ref,
                          actt_s, *gm):
    _peer_dots(1, cnt_ref, loc_ref, h_ref, tab_ref, actt_s, gm)
    in0 = idxt_ref[...] < HALF_EXPERTS
    w = gatet_ref[...] * _gelu(jnp.where(in0, act0t_ref[...], actt_s[...]))
    w0_ref[...] = jnp.where(in0, w, 0.0)
    w1_ref[...] = jnp.where(in0, 0.0, w)


def _peer_sum(half, cnt_ref, loc_ref, w_ref, tab_ref, init, f_ref, wb):
    tok0 = pl.program_id(0) * PEER_BLOCK

    def splat_weights(t, buf):
        buf[...] = jnp.take_along_axis(w_ref[...], jnp.full((N_PICKS, PEER_BLOCK), t, jnp.int32), axis=1)

    def add_group(t, g, accs, weight_row):
        loc_row = loc_ref.at[t]
        accs = list(accs)
        for s in range(SUBLANES):
            r = pl.multiple_of(loc_row[g * SUBLANES + s], SUBLANES)
            w = jnp.broadcast_to(weight_row(g * SUBLANES + s), (SUBLANES, LANES))
            accs[s % 4] = accs[s % 4] + tab_ref[pl.ds(r, SUBLANES), :] * w
        return tuple(accs)

    def token_rows(t):
        return pl.ds(pl.multiple_of(t * SUBLANES, SUBLANES), SUBLANES)

    def weighted_sum(t, buf):
        accs = (init(token_rows(t)),) + (jnp.zeros((SUBLANES, LANES), F32),) * 3
        for g in _group_ranges(half, 0)[0]:
            accs = add_group(t, g, accs, lambda e: buf[e:e + 1, :])
        return accs

    def total(accs):
        return (accs[0] + accs[1]) + (accs[2] + accs[3])

    def extra_weighted_sums(t, bufs):
        bounds = [_group_ranges(half, cnt_ref[tok0 + t + i])[1] for i in range(2)]

        @pl.when((bounds[0][1] > bounds[0][0]) | (bounds[1][1] > bounds[1][0]))
        def _():
            for i in range(2):
                accs = (f_ref[token_rows(t + i), :],) + (jnp.zeros((SUBLANES, LANES), F32),) * 3
                accs = lax.fori_loop(
                    *bounds[i],
                    lambda g, a, i=i: add_group(t + i, g, a, lambda e: bufs[i][pl.ds(e, 1), :]), accs)
                f_ref[token_rows(t + i), :] = total(accs)

    def token_pair(t, bufs, next_t, next_bufs):
        splat_weights(jnp.minimum(next_t, PEER_BLOCK - 1), next_bufs[0])
        splat_weights(jnp.minimum(next_t + 1, PEER_BLOCK - 1), next_bufs[1])
        f_ref[token_rows(t), :] = total(weighted_sum(t, bufs[0]))
        f_ref[token_rows(t + 1), :] = total(weighted_sum(t + 1, bufs[1]))
        extra_weighted_sums(t, bufs)

    splat_weights(0, wb[0])
    splat_weights(1, wb[1])

    def four_tokens(k, carry):
        t = 4 * k
        token_pair(t, wb[0:2], t + 2, wb[2:4])
        token_pair(t + 2, wb[2:4], t + 4, wb[0:2])
        return carry

    lax.fori_loop(0, PEER_BLOCK // 4, four_tokens, 0)


def _peer_v_first_kernel(cnt_ref, loc_ref, w_ref, tab_ref, f_ref, *wb):
    _peer_sum(0, cnt_ref, loc_ref, w_ref, tab_ref, lambda rows: jnp.zeros((SUBLANES, LANES), F32), f_ref, wb)


def _peer_v_second_kernel(cnt_ref, loc_ref, w_ref, tab_ref, f0_ref, f_ref, *wb):
    _peer_sum(1, cnt_ref, loc_ref, w_ref, tab_ref, lambda rows: f0_ref[rows, :], f_ref, wb)


def _peer(h2, loc0, loc1, idx_t, gate_t, count0, u_tab, v_tab, layer):
    nblk = N_TOK // PEER_BLOCK
    h_tiles = h2.reshape(N_TOK * SUBLANES, LANES)
    cnt_spec = pl.BlockSpec(memory_space=pltpu.SMEM)
    smem_spec = pl.BlockSpec((PEER_BLOCK, N_PICKS), lambda i: (i, 0), memory_space=pltpu.SMEM)
    row_spec = pl.BlockSpec((PEER_BLOCK, N_PICKS), lambda i: (i, 0))
    col_spec = pl.BlockSpec((N_PICKS, PEER_BLOCK), lambda i: (0, i))
    tile_spec = pl.BlockSpec((PEER_BLOCK * SUBLANES, LANES), lambda i: (i, 0))
    half_spec = lambda half: pl.BlockSpec((None, HALF_EXPERTS * SUBLANES, LANES),
                                          lambda i: (layer, half, 0), pipeline_mode=pl.Buffered(1))
    row_shape = jax.ShapeDtypeStruct((N_TOK, N_PICKS), F32)
    col_shape = jax.ShapeDtypeStruct((N_PICKS, N_TOK), F32)
    tile_shape = jax.ShapeDtypeStruct((N_TOK * SUBLANES, LANES), F32)
    params = pltpu.CompilerParams(vmem_limit_bytes=VMEM_LIMIT)
    pair_scratch = [pltpu.VMEM((N_PICKS, LANES), F32)] * 4

    act0_t = pl.pallas_call(
        _peer_u_first_kernel, grid=(nblk,),
        in_specs=[cnt_spec, smem_spec, tile_spec, half_spec(0)],
        out_specs=col_spec, out_shape=col_shape, scratch_shapes=pair_scratch,
        compiler_params=params, name="peer_u0",
    )(count0, loc0, h_tiles, u_tab)
    w0, w1 = pl.pallas_call(
        _peer_u_second_kernel, grid=(nblk,),
        in_specs=[cnt_spec, smem_spec, tile_spec, half_spec(1), col_spec, col_spec, col_spec],
        out_specs=[col_spec, col_spec], out_shape=[col_shape, col_shape],
        scratch_shapes=[pltpu.VMEM((N_PICKS, PEER_BLOCK), F32)] + pair_scratch,
        compiler_params=params, name="peer_u1",
    )(count0, loc1, h_tiles, u_tab, act0_t, idx_t, gate_t)
    f0 = pl.pallas_call(
        _peer_v_first_kernel, grid=(nblk,),
        in_specs=[cnt_spec, smem_spec, col_spec, half_spec(0)],
        out_specs=tile_spec, out_shape=tile_shape, scratch_shapes=pair_scratch,
        compiler_params=params, name="peer_v0",
    )(count0, loc0, w0, v_tab)
    f = pl.pallas_call(
        _peer_v_second_kernel, grid=(nblk,),
        in_specs=[cnt_spec, smem_spec, col_spec, half_spec(1), tile_spec],
        out_specs=tile_spec, out_shape=tile_shape, scratch_shapes=pair_scratch,
        compiler_params=params, name="peer_v1",
    )(count0, loc1, w1, v_tab, f0)
    return f.reshape(N_TOK, D_MODEL)


def _close_kernel(x1_ref, f_ref, mod_ref, g_ref, b_ref, o_ref):
    g2 = mod_ref[...][:, 5 * D_MODEL:6 * D_MODEL]
    o_ref[...] = _ln(ALPHA * x1_ref[...] + g2 * f_ref[...]) * g_ref[1:2, :] + b_ref[1:2, :]


def _close(x1, f, mods, ln_g, ln_b, layer):
    row_spec = pl.BlockSpec((ROW_BLOCK, D_MODEL), lambda i: (i, 0))
    vec_spec = pl.BlockSpec((None, 2, D_MODEL), lambda i: (layer, 0, 0))
    return pl.pallas_call(
        _close_kernel,
        grid=(N_ROW_BLOCKS,),
        in_specs=[row_spec, row_spec,
                  pl.BlockSpec((None, None, 1, 6 * D_MODEL), lambda i: (layer, _mod_group(i), 0, 0)),
                  vec_spec, vec_spec],
        out_specs=row_spec,
        out_shape=jax.ShapeDtypeStruct((N_TOK, D_MODEL), F32),
        compiler_params=pltpu.CompilerParams(vmem_limit_bytes=VMEM_LIMIT),
        name="close",
    )(x1, f, mods, ln_g, ln_b)


def _rope_tables(dim, copies):
    rows = LAT_LEN // GRID_W
    row = jnp.repeat(jnp.arange(rows, dtype=F32), GRID_W)
    col = jnp.tile(jnp.arange(GRID_W, dtype=F32), rows)
    nf = dim // 4
    inv = ROPE_BASE ** (-jnp.arange(nf, dtype=F32) / nf)
    ar = row[:, None] * inv
    ac = col[:, None] * inv
    ang = jnp.concatenate([ar, ar, ac, ac], axis=-1)
    cos, sin = jnp.cos(ang), jnp.sin(ang)
    even_quarter = (jnp.arange(dim) // nf) % 2 == 0
    sin_up = jnp.where(even_quarter, -sin, 0.0)
    sin_dn = jnp.where(even_quarter, 0.0, sin)
    return tuple(jnp.tile(t, (1, copies)) for t in (cos, sin_up, sin_dn))


def kernel(x_prompt, x_sample, c, cache_win_k, cache_win_v, cache_diff_k, cache_diff_v, c_ctx, w_mod, b_mod, w_in, w_out, chunk_w, chunk_b, win_sink, diff_lam_q, diff_lam_k, diff_subln_g, pool_w, pool_scale, ln_g, ln_b, peer_wq, peer_keys, peer_u, peer_v):
    cvec = jnp.concatenate([c_ctx[None, :], c, jnp.zeros((8 - 1 - N_LAT_SEQ, D_MODEL), F32)], axis=0)
    mods = _modulation(cvec, w_mod, b_mod)[:, :1 + N_LAT_SEQ].reshape(DEPTH, 1 + N_LAT_SEQ, 1, 6 * D_MODEL)

    w_in_b = w_in.astype(BF16)
    w_out_b = w_out.astype(BF16)
    wq_b = peer_wq.astype(BF16)
    cw_b = chunk_w.astype(BF16)
    cb_full = jnp.repeat(jnp.swapaxes(chunk_b, 1, 2), HEAD_D, axis=2)
    eye = jnp.eye(4, dtype=F32)
    wbd = jnp.einsum('lgcd,gh->lgchd', pool_w, eye).reshape(DEPTH, GROUP_W, GROUP_W).astype(BF16)
    caches = (cache_win_k.reshape(N_LAT_SEQ, DEPTH, CTX_LEN, 128),
              cache_win_v.reshape(N_LAT_SEQ, DEPTH, CTX_LEN, 128),
              cache_diff_k.reshape(N_LAT_SEQ, DEPTH, CTX_LEN, GROUP_W),
              cache_diff_v.reshape(N_LAT_SEQ, DEPTH, CTX_LEN, GROUP_W))
    rope_tabs = _rope_tables(HEAD_D, GROUP_W // HEAD_D) + _rope_tables(C_SUB, GROUP_W // C_SUB)
    u_tiles = peer_u.reshape(DEPTH, N_EXPERTS * SUBLANES, LANES)
    v_tiles = peer_v.reshape(DEPTH, N_EXPERTS * SUBLANES, LANES)

    x = jnp.concatenate([x_prompt.reshape(N_CTX_TOK, D_MODEL), x_sample.reshape(N_LAT_TOK, D_MODEL)], axis=0)
    kbs, vbs, kcs, vcs = [], [], [], []
    for l in range(DEPTH):
        lam_init = 0.8 - 0.6 * math.exp(-0.3 * l)
        layer_w = (cw_b[l], cb_full[l], win_sink[l], diff_lam_q[l], diff_lam_k[l],
                   diff_subln_g[l].reshape(1, HEAD_D), wbd[l], pool_scale[l].reshape(1, GROUP_W))
        p = _in_proj(x, mods, w_in_b, l)
        pc = p[:N_CTX_TOK].reshape(N_CTX_SEQ, CTX_LEN, P_COLS)
        kbs.append(pc[..., COL_BK:COL_BV].reshape(N_CTX_SEQ, CTX_LEN, 2, HEAD_D))
        vbs.append(pc[..., COL_BV:COL_CQ].reshape(N_CTX_SEQ, CTX_LEN, 2, HEAD_D))
        kcs.append(pc[..., COL_CK:COL_CV].reshape(N_CTX_SEQ, CTX_LEN, 4, 2, C_SUB))
        vcs.append(pc[..., COL_CV:COL_DZ].reshape(N_CTX_SEQ, CTX_LEN, 4, HEAD_D))
        o = jnp.concatenate([_ctx_mix(p, layer_w, lam_init),
                             _lat_mix(p, caches, rope_tabs, layer_w, lam_init, l)], axis=0)
        x1, h2, q = _out_proj(o, x, mods, w_out_b, ln_g, ln_b, wq_b, l)
        loc0, loc1, idx_t, gate_t, count0 = _route(q, peer_keys[l])
        f = _peer(h2, loc0, loc1, idx_t, gate_t, count0.reshape(N_TOK), u_tiles, v_tiles, l)
        x = _close(x1, f, mods, ln_g, ln_b, l)
    return (x[:N_CTX_TOK].reshape(N_CTX_SEQ, CTX_LEN, D_MODEL),
            x[N_CTX_TOK:].reshape(N_LAT_SEQ, LAT_LEN, D_MODEL),
            jnp.stack(kbs, axis=1), jnp.stack(vbs, axis=1), jnp.stack(kcs, axis=1), jnp.stack(vcs, axis=1))
```

```python
import functools
import math

import jax
import jax.numpy as jnp
import numpy as np
from jax import lax
from jax.experimental import pallas as pl
from jax.experimental.pallas import tpu as pltpu

F32 = jnp.float32
BF16 = jnp.bfloat16
HIGHEST = lax.Precision.HIGHEST

D_MODEL = 1024
DEPTH = 4
N_CTX_SEQ = 16
CTX_LEN = 256
N_LAT_SEQ = 2
LAT_LEN = 1024
N_CTX_TOK = N_CTX_SEQ * CTX_LEN
N_LAT_TOK = N_LAT_SEQ * LAT_LEN
N_TOK = N_CTX_TOK + N_LAT_TOK
GRID_W = 64
BLOCK = 128
GROUP_W = 256
HEAD_D = 64
C_SUB = 32
WINDOW = 128
POOL_SIZES = (2, 4, 8, 16)
P_COLS = 2048
COL_AU, COL_AV, COL_BQ, COL_BK, COL_BV, COL_CQ, COL_CK, COL_CV, COL_DZ = (
    0, 256, 512, 768, 896, 1024, 1280, 1536, 1792)
PK_HEADS = 8
N_KEYS = 128
N_EXPERTS = N_KEYS * N_KEYS
PK_TOPK = 16
N_PICKS = PK_HEADS * PK_TOPK
ROPE_BASE = 10000.0
ALPHA = (2 * DEPTH) ** 0.25
LN_EPS = 1e-5
NEG = -1e30

ROW_BLOCK = 512
N_ROW_BLOCKS = N_TOK // ROW_BLOCK
N_CTX_ROW_BLOCKS = N_CTX_TOK // ROW_BLOCK
ROW_BLOCKS_PER_LAT_SEQ = LAT_LEN // ROW_BLOCK
ROUTE_BLOCK = 128
PEER_BLOCK = 128
SUBLANES = 8
LANES = 128
HALF_EXPERTS = N_EXPERTS // 2
VMEM_LIMIT = 56 * 1024 * 1024


def _mm(a, b):
    return jnp.dot(a.astype(BF16), b.astype(BF16), preferred_element_type=F32)


def _mm_nt(a, b):
    return lax.dot_general(a.astype(BF16), b.astype(BF16), (((1,), (1,)), ((), ())),
                           preferred_element_type=F32)


def _gelu(x):
    return x * (0.5 * (1.0 + jnp.tanh(0.7978845608028654 * (x + 0.044715 * (x * x * x)))))


def _ln(x):
    mu = jnp.mean(x, axis=-1, keepdims=True)
    xc = x - mu
    var = jnp.mean(xc * xc, axis=-1, keepdims=True)
    return xc * lax.rsqrt(var + LN_EPS)


def _mod_group(i):
    return jnp.where(i < N_CTX_ROW_BLOCKS, 0, 1 + (i - N_CTX_ROW_BLOCKS) // ROW_BLOCKS_PER_LAT_SEQ)


def _mod_kernel(c_ref, w_ref, b_ref, o_ref):
    c = c_ref[...]
    s = c * jax.nn.sigmoid(c)
    o_ref[...] = jnp.dot(s, w_ref[...], precision=HIGHEST, preferred_element_type=F32) + b_ref[...]


def _modulation(cvec, w_mod, b_mod):
    nj = 4
    cols = 6 * D_MODEL // nj
    return pl.pallas_call(
        _mod_kernel,
        grid=(DEPTH, nj),
        in_specs=[pl.BlockSpec((8, D_MODEL), lambda l, j: (0, 0)),
                  pl.BlockSpec((None, D_MODEL, cols), lambda l, j: (l, 0, j)),
                  pl.BlockSpec((None, 1, cols), lambda l, j: (l, 0, j))],
        out_specs=pl.BlockSpec((None, 8, cols), lambda l, j: (l, 0, j)),
        out_shape=jax.ShapeDtypeStruct((DEPTH, 8, 6 * D_MODEL), F32),
        compiler_params=pltpu.CompilerParams(vmem_limit_bytes=VMEM_LIMIT),
        name="modulation",
    )(cvec, w_mod, b_mod.reshape(DEPTH, 1, 6 * D_MODEL))


def _in_proj_kernel(x_ref, mod_ref, w_ref, o_ref):
    m = mod_ref[...]
    h = _ln(x_ref[...]) * (1.0 + m[:, D_MODEL:2 * D_MODEL]) + m[:, 0:D_MODEL]
    o_ref[...] = jnp.dot(h.astype(BF16), w_ref[...], preferred_element_type=F32)


def _in_proj(x, mods, w_in_bf16, layer):
    return pl.pallas_call(
        _in_proj_kernel,
        grid=(N_ROW_BLOCKS,),
        in_specs=[pl.BlockSpec((ROW_BLOCK, D_MODEL), lambda i: (i, 0)),
                  pl.BlockSpec((None, None, 1, 6 * D_MODEL), lambda i: (layer, _mod_group(i), 0, 0)),
                  pl.BlockSpec((None, D_MODEL, P_COLS), lambda i: (layer, 0, 0))],
        out_specs=pl.BlockSpec((ROW_BLOCK, P_COLS), lambda i: (i, 0)),
        out_shape=jax.ShapeDtypeStruct((N_TOK, P_COLS), F32),
        compiler_params=pltpu.CompilerParams(vmem_limit_bytes=VMEM_LIMIT),
        name="in_proj",
    )(x, mods, w_in_bf16)


def _chunk_gating(u_raw, v_raw, cw_ref, cb_full):
    u = _gelu(u_raw)
    v = _gelu(v_raw).astype(BF16)
    head = lax.broadcasted_iota(jnp.int32, (BLOCK, GROUP_W), 1) // HEAD_D
    mixed = cb_full
    for h in range(4):
        mixed = mixed + jnp.where(head == h, jnp.dot(cw_ref[h], v, preferred_element_type=F32), 0.0)
    return u * mixed


def _pool(z_rows, z_win, row0, key0, seq_len, wbd_ref, pool_scale):
    r, k = z_rows.shape[0], z_win.shape[0]
    t = row0 + lax.broadcasted_iota(jnp.int32, (r, k), 0)
    pos = key0 + lax.broadcasted_iota(jnp.int32, (r, k), 1)
    t1 = row0 + lax.broadcasted_iota(jnp.int32, (r, 1), 0)
    z_hi = z_win.astype(BF16)
    z_lo = (z_win - z_hi.astype(F32)).astype(BF16)
    group = lax.broadcasted_iota(jnp.int32, (r, GROUP_W), 1) // HEAD_D
    mean = jnp.zeros((r, GROUP_W), F32)
    for g, w in enumerate(POOL_SIZES):
        lo = jnp.maximum(t - w // 2, 0)
        hi = jnp.minimum(t + w // 2, seq_len)
        band = jnp.where((pos >= lo) & (pos < hi), 1.0, 0.0).astype(BF16)
        tot = (jnp.dot(band, z_hi, preferred_element_type=F32)
               + jnp.dot(band, z_lo, preferred_element_type=F32))
        cnt = (jnp.minimum(t1 + w // 2, seq_len) - jnp.maximum(t1 - w // 2, 0)).astype(F32)
        mean = mean + jnp.where(group == g, tot / cnt, 0.0)
    pooled = mean - z_rows
    return jnp.dot(pooled.astype(BF16), wbd_ref[...], preferred_element_type=F32) * pool_scale


def _diff_lambda(lq_ref, lk_ref, lam_init):
    lq = lq_ref[...]
    lk = lk_ref[...]
    s0 = jnp.sum(lq[0:1] * lk[0:1], axis=-1, keepdims=True)
    s1 = jnp.sum(lq[1:2] * lk[1:2], axis=-1, keepdims=True)
    return jnp.exp(s0) - jnp.exp(s1) + lam_init


def _sub_norm(o, g, lam_init):
    ms = jnp.mean(o * o, axis=-1, keepdims=True)
    return o * lax.rsqrt(ms + LN_EPS) * g * (1.0 - lam_init)


def _ctx_mix_kernel(lam_init, p_ref, cw_ref, cb_ref, sink_ref, lq_ref, lk_ref, sg_ref, wbd_ref,
                    ps_ref, o_ref):
    s_len = CTX_LEN
    for n in range(s_len // BLOCK):
        rows = slice(n * BLOCK, (n + 1) * BLOCK)
        o_ref[rows, 0:GROUP_W] = _chunk_gating(p_ref[rows, COL_AU:COL_AU + GROUP_W],
                                               p_ref[rows, COL_AV:COL_AV + GROUP_W], cw_ref, cb_ref[...])
    for kv in range(2):
        kk = p_ref[:, COL_BK + kv * HEAD_D:COL_BK + (kv + 1) * HEAD_D].astype(BF16)
        vv = p_ref[:, COL_BV + kv * HEAD_D:COL_BV + (kv + 1) * HEAD_D].astype(BF16)
        for g in range(2):
            h = kv * 2 + g
            q = p_ref[:, COL_BQ + h * HEAD_D:COL_BQ + (h + 1) * HEAD_D] * (HEAD_D ** -0.5)
            s = _mm_nt(q, kk)
            sk = sink_ref[h]
            m = jnp.maximum(jnp.max(s, axis=-1, keepdims=True), sk)
            e = jnp.exp(s - m)
            den = jnp.sum(e, axis=-1, keepdims=True) + jnp.exp(sk - m)
            o_ref[:, GROUP_W + h * HEAD_D:GROUP_W + (h + 1) * HEAD_D] = _mm(e, vv) / den
    lam = _diff_lambda(lq_ref, lk_ref, lam_init)
    scale = C_SUB ** -0.5
    for h in range(4):
        vv = p_ref[:, COL_CV + h * HEAD_D:COL_CV + (h + 1) * HEAD_D].astype(BF16)
        parts = []
        for j in range(2):
            c0 = h * HEAD_D + j * C_SUB
            s = _mm_nt(p_ref[:, COL_CQ + c0:COL_CQ + c0 + C_SUB],
                       p_ref[:, COL_CK + c0:COL_CK + c0 + C_SUB]) * scale
            e = jnp.exp(s - jnp.max(s, axis=-1, keepdims=True))
            parts.append(_mm(e, vv) / jnp.sum(e, axis=-1, keepdims=True))
        o = parts[0] - lam * parts[1]
        o_ref[:, 2 * GROUP_W + h * HEAD_D:2 * GROUP_W + (h + 1) * HEAD_D] = _sub_norm(o, sg_ref[...], lam_init)
    z = p_ref[:, COL_DZ:COL_DZ + GROUP_W]
    o_ref[:, 3 * GROUP_W:4 * GROUP_W] = _pool(z, z, 0, 0, s_len, wbd_ref, ps_ref[...])


def _whole(shape):
    nd = len(shape)
    return pl.BlockSpec(shape, lambda *_: (0,) * nd)


def _ctx_mix(p, layer_w, lam_init):
    cw, cb_full, sink, lq, lk, sg, wbd, ps = layer_w
    return pl.pallas_call(
        functools.partial(_ctx_mix_kernel, lam_init),
        grid=(N_CTX_SEQ,),
        in_specs=[pl.BlockSpec((CTX_LEN, P_COLS), lambda b: (b, 0)),
                  _whole(cw.shape), _whole(cb_full.shape),
                  pl.BlockSpec(memory_space=pltpu.SMEM),
                  _whole(lq.shape), _whole(lk.shape), _whole(sg.shape), _whole(wbd.shape),
                  _whole(ps.shape)],
        out_specs=pl.BlockSpec((CTX_LEN, D_MODEL), lambda b: (b, 0)),
        out_shape=jax.ShapeDtypeStruct((N_CTX_TOK, D_MODEL), F32),
        compiler_params=pltpu.CompilerParams(vmem_limit_bytes=VMEM_LIMIT),
        name="ctx_mix",
    )(p, cw, cb_full, sink, lq, lk, sg, wbd, ps)


def _rope(x, cos, sin_up, sin_dn, shift):
    outs = []
    for c0 in range(0, x.shape[1], LANES):
        xs = x[:, c0:c0 + LANES]
        up = pltpu.roll(xs, LANES - shift, 1)
        dn = pltpu.roll(xs, shift, 1)
        outs.append(xs * cos[:, c0:c0 + LANES] + up * sin_up[:, c0:c0 + LANES]
                    + dn * sin_dn[:, c0:c0 + LANES])
    return outs[0] if len(outs) == 1 else jnp.concatenate(outs, axis=1)


def _lat_mix_kernel(lam_init, p_ref, wk_ref, wv_ref, dk_ref, dv_ref,
                    cosb_ref, sbu_ref, sbd_ref, cosc_ref, scu_ref, scd_ref,
                    cw_ref, cb_ref, sink_ref, lq_ref, lk_ref, sg_ref, wbd_ref, ps_ref,
                    o_ref, kb_s, kc_s):
    n = pl.program_id(1)
    s_len = LAT_LEN

    @pl.when(n == 0)
    def _():
        kb_s[...] = _rope(p_ref[:, COL_BK:COL_BK + 128], cosb_ref[:, 0:128], sbu_ref[:, 0:128],
                          sbd_ref[:, 0:128], HEAD_D // 4).astype(BF16)
        kc_s[...] = _rope(p_ref[:, COL_CK:COL_CK + GROUP_W], cosc_ref[...], scu_ref[...],
                          scd_ref[...], C_SUB // 4).astype(BF16)

    r0 = pl.multiple_of(n * BLOCK, BLOCK)
    rows = pl.ds(r0, BLOCK)
    o_ref[:, 0:GROUP_W] = _chunk_gating(p_ref[rows, COL_AU:COL_AU + GROUP_W],
                                        p_ref[rows, COL_AV:COL_AV + GROUP_W], cw_ref, cb_ref[...])
    w0 = pl.multiple_of(jnp.clip((n - 1) * BLOCK, 0, s_len - 3 * BLOCK), BLOCK)
    win = pl.ds(w0, 3 * BLOCK)
    qb = _rope(p_ref[rows, COL_BQ:COL_BQ + GROUP_W], cosb_ref[rows, :], sbu_ref[rows, :],
               sbd_ref[rows, :], HEAD_D // 4) * (HEAD_D ** -0.5)
    qpos = r0 + lax.broadcasted_iota(jnp.int32, (BLOCK, 3 * BLOCK), 0)
    kpos = w0 + lax.broadcasted_iota(jnp.int32, (BLOCK, 3 * BLOCK), 1)
    valid = jnp.abs(kpos - qpos) <= WINDOW
    for kv in range(2):
        cs = slice(kv * HEAD_D, (kv + 1) * HEAD_D)
        k_loc = kb_s[win, cs]
        v_loc = p_ref[win, COL_BV + kv * HEAD_D:COL_BV + (kv + 1) * HEAD_D].astype(BF16)
        k_ctx = wk_ref[:, cs].astype(BF16)
        v_ctx = wv_ref[:, cs].astype(BF16)
        for g in range(2):
            h = kv * 2 + g
            q = qb[:, h * HEAD_D:(h + 1) * HEAD_D]
            s_loc = jnp.where(valid, _mm_nt(q, k_loc), NEG)
            s_ctx = _mm_nt(q, k_ctx)
            sk = sink_ref[h]
            m = jnp.maximum(jnp.maximum(jnp.max(s_loc, axis=-1, keepdims=True),
                                        jnp.max(s_ctx, axis=-1, keepdims=True)), sk)
            e_loc = jnp.exp(s_loc - m)
            e_ctx = jnp.exp(s_ctx - m)
            den = (jnp.sum(e_loc, axis=-1, keepdims=True) + jnp.sum(e_ctx, axis=-1, keepdims=True)
                   + jnp.exp(sk - m))
            o_ref[:, GROUP_W + h * HEAD_D:GROUP_W + (h + 1) * HEAD_D] = (
                _mm(e_loc, v_loc) + _mm(e_ctx, v_ctx)) / den
    lam = _diff_lambda(lq_ref, lk_ref, lam_init)
    scale = C_SUB ** -0.5
    qc = _rope(p_ref[rows, COL_CQ:COL_CQ + GROUP_W], cosc_ref[rows, :], scu_ref[rows, :],
               scd_ref[rows, :], C_SUB // 4)
    for h in range(4):
        v_ctx = dv_ref[:, h * HEAD_D:(h + 1) * HEAD_D].astype(BF16)
        v_lat = p_ref[:, COL_CV + h * HEAD_D:COL_CV + (h + 1) * HEAD_D].astype(BF16)
        parts = []
        for j in range(2):
            cs = slice(h * HEAD_D + j * C_SUB, h * HEAD_D + (j + 1) * C_SUB)
            q = qc[:, cs]
            s_ctx = _mm_nt(q, dk_ref[:, cs]) * scale
            s_lat = _mm_nt(q, kc_s[:, cs]) * scale
            m = jnp.maximum(jnp.max(s_ctx, axis=-1, keepdims=True), jnp.max(s_lat, axis=-1, keepdims=True))
            e_ctx = jnp.exp(s_ctx - m)
            e_lat = jnp.exp(s_lat - m)
            den = jnp.sum(e_ctx, axis=-1, keepdims=True) + jnp.sum(e_lat, axis=-1, keepdims=True)
            parts.append((_mm(e_ctx, v_ctx) + _mm(e_lat, v_lat)) / den)
        o = parts[0] - lam * parts[1]
        o_ref[:, 2 * GROUP_W + h * HEAD_D:2 * GROUP_W + (h + 1) * HEAD_D] = _sub_norm(o, sg_ref[...], lam_init)
    o_ref[:, 3 * GROUP_W:4 * GROUP_W] = _pool(p_ref[rows, COL_DZ:COL_DZ + GROUP_W],
                                              p_ref[win, COL_DZ:COL_DZ + GROUP_W],
                                              r0, w0, s_len, wbd_ref, ps_ref[...])


def _lat_mix(p, caches, rope_tabs, layer_w, lam_init, layer):
    cw, cb_full, sink, lq, lk, sg, wbd, ps = layer_w
    nb = LAT_LEN // BLOCK
    ctx_blocks = N_CTX_TOK // LAT_LEN
    cache_spec = lambda width: pl.BlockSpec((None, None, CTX_LEN, width), lambda b, n: (b, layer, 0, 0))
    rope_spec = pl.BlockSpec((LAT_LEN, GROUP_W), lambda b, n: (0, 0))
    return pl.pallas_call(
        functools.partial(_lat_mix_kernel, lam_init),
        grid=(N_LAT_SEQ, nb),
        in_specs=[pl.BlockSpec((LAT_LEN, P_COLS), lambda b, n: (ctx_blocks + b, 0)),
                  cache_spec(128), cache_spec(128), cache_spec(256), cache_spec(256)]
                 + [rope_spec] * 6
                 + [_whole(cw.shape), _whole(cb_full.shape), pl.BlockSpec(memory_space=pltpu.SMEM),
                    _whole(lq.shape), _whole(lk.shape), _whole(sg.shape), _whole(wbd.shape),
                    _whole(ps.shape)],
        out_specs=pl.BlockSpec((BLOCK, D_MODEL), lambda b, n: (b * nb + n, 0)),
        out_shape=jax.ShapeDtypeStruct((N_LAT_TOK, D_MODEL), F32),
        scratch_shapes=[pltpu.VMEM((LAT_LEN, 128), BF16), pltpu.VMEM((LAT_LEN, GROUP_W), BF16)],
        compiler_params=pltpu.CompilerParams(vmem_limit_bytes=VMEM_LIMIT,
                                             dimension_semantics=("arbitrary", "arbitrary")),
        name="lat_mix",
    )(p, *caches, *rope_tabs, cw, cb_full, sink, lq, lk, sg, wbd, ps)


def _out_proj_kernel(o_ref, x_ref, mod_ref, wo_ref, g_ref, b_ref, wq_ref, x1_ref, h2_ref, q_ref):
    m = mod_ref[...]
    g1 = m[:, 2 * D_MODEL:3 * D_MODEL]
    sh2 = m[:, 3 * D_MODEL:4 * D_MODEL]
    sc2 = m[:, 4 * D_MODEL:5 * D_MODEL]
    y = jnp.dot(o_ref[...].astype(BF16), wo_ref[...], preferred_element_type=F32)
    x1 = _ln(ALPHA * x_ref[...] + g1 * y) * g_ref[0:1, :] + b_ref[0:1, :]
    x1_ref[...] = x1
    h2 = _ln(x1) * (1.0 + sc2) + sh2
    h2_ref[...] = h2
    q_ref[...] = jnp.dot(h2.astype(BF16), wq_ref[...], preferred_element_type=F32)


def _out_proj(o, x, mods, w_out_bf16, ln_g, ln_b, wq_bf16, layer):
    row_spec = pl.BlockSpec((ROW_BLOCK, D_MODEL), lambda i: (i, 0))
    mat_spec = pl.BlockSpec((None, D_MODEL, D_MODEL), lambda i: (layer, 0, 0))
    vec_spec = pl.BlockSpec((None, 2, D_MODEL), lambda i: (layer, 0, 0))
    shape = jax.ShapeDtypeStruct((N_TOK, D_MODEL), F32)
    return pl.pallas_call(
        _out_proj_kernel,
        grid=(N_ROW_BLOCKS,),
        in_specs=[row_spec, row_spec,
                  pl.BlockSpec((None, None, 1, 6 * D_MODEL), lambda i: (layer, _mod_group(i), 0, 0)),
                  mat_spec, vec_spec, vec_spec, mat_spec],
        out_specs=[row_spec, row_spec, row_spec],
        out_shape=[shape, shape, shape],
        compiler_params=pltpu.CompilerParams(vmem_limit_bytes=VMEM_LIMIT),
        name="out_proj",
    )(o, x, mods, w_out_bf16, ln_g, ln_b, wq_bf16)


def _top16_rows(sc, order):
    out_row = lax.broadcasted_iota(jnp.int32, (PK_TOPK, sc.shape[1]), 0)
    vals = jnp.zeros((PK_TOPK, sc.shape[1]), F32)
    sel = jnp.zeros((PK_TOPK, sc.shape[1]), jnp.int32)
    for i in range(PK_TOPK):
        m = jnp.max(sc, axis=0, keepdims=True)
        am = jnp.min(jnp.where(sc == m, order, jnp.int32(2 ** 30)), axis=0, keepdims=True)
        vals = jnp.where(out_row == i, m, vals)
        sel = jnp.where(out_row == i, am, sel)
        sc = jnp.where(order == am, NEG, sc)
    return vals, sel


def _pair_candidates(sv, si):
    n_tok = sv[0].shape[1]
    rows = lax.broadcasted_iota(jnp.int32, (SUBLANES, n_tok), 0)
    cands, flats, experts = [], [], []

    def piece(a, b0, valid):
        v = sv[0][a:a + 1, :] + sv[1][b0:b0 + SUBLANES, :]
        cands.append(v if valid >= SUBLANES else jnp.where(rows < valid, v, NEG))
        flats.append(a * PK_TOPK + b0 + rows)
        experts.append(si[0][a:a + 1, :] * N_KEYS + si[1][b0:b0 + SUBLANES, :])

    piece(0, 0, SUBLANES)
    piece(0, SUBLANES, SUBLANES)
    for a in range(1, SUBLANES):
        piece(a, 0, PK_TOPK // (a + 1))
    cands.append(sv[0][SUBLANES:, :] + sv[1][0:1, :])
    flats.append((SUBLANES + rows) * PK_TOPK)
    experts.append(si[0][SUBLANES:, :] * N_KEYS + si[1][0:1, :])
    order = jnp.concatenate(flats, axis=0) * N_EXPERTS + jnp.concatenate(experts, axis=0)
    return jnp.concatenate(cands, axis=0), order


def _table_half(idx):
    return ((idx >> 7) ^ idx) & 1


def _half_offsets(idx, half):
    i1 = idx >> 7
    local = (i1 & 1) * (HALF_EXPERTS // 2) + (i1 >> 1) * (N_KEYS // 2) + ((idx & (N_KEYS - 1)) >> 1)
    return jnp.where(_table_half(idx) == half, local, 0) * SUBLANES


def _split_tables(tab):
    t = tab.reshape(DEPTH, N_KEYS // 2, 2, N_KEYS // 2, 2, D_MODEL)
    halves = [jnp.stack([t[:, :, 0, :, h], t[:, :, 1, :, 1 - h]], axis=1) for h in range(2)]
    return jnp.stack(halves, axis=1).reshape(DEPTH, N_EXPERTS * SUBLANES, LANES)


def _route_kernel(q_ref, keys_ref, loc0_ref, loc1_ref, idxt_ref, gatet_ref, count0_ref,
                  slot_s, idx_s, gate_s):
    n_tok = q_ref.shape[0]
    key_row = lax.broadcasted_iota(jnp.int32, (N_KEYS, n_tok), 0)
    for h in range(PK_HEADS):
        sv = []
        si = []
        for j in range(2):
            c0 = (h * 2 + j) * HEAD_D
            sc = lax.dot_general(keys_ref[j], q_ref[:, c0:c0 + HEAD_D], (((1,), (1,)), ((), ())),
                                 precision=HIGHEST, preferred_element_type=F32)
            v, i = _top16_rows(sc, key_row)
            sv.append(v)
            si.append(i)
        cv, winners = _top16_rows(*_pair_candidates(sv, si))
        ce = winners & (N_EXPERTS - 1)
        e = jnp.exp(cv - jnp.max(cv, axis=0, keepdims=True))
        gate_s[h * PK_TOPK:(h + 1) * PK_TOPK, :] = e / jnp.sum(e, axis=0, keepdims=True)
        idx_s[h * PK_TOPK:(h + 1) * PK_TOPK, :] = ce
    in0 = _table_half(idx_s[...]) == 0
    before = (lax.broadcasted_iota(jnp.int32, (N_PICKS, N_PICKS), 1)
              < lax.broadcasted_iota(jnp.int32, (N_PICKS, N_PICKS), 0))
    rank0 = jnp.dot(jnp.where(before, 1.0, 0.0).astype(BF16), jnp.where(in0, 1.0, 0.0).astype(BF16),
                    preferred_element_type=F32).astype(jnp.int32)
    count0 = jnp.sum(jnp.where(in0, 1, 0), axis=0, keepdims=True)
    pick = lax.broadcasted_iota(jnp.int32, (N_PICKS, n_tok), 0)
    slot_s[...] = jnp.where(in0, rank0, count0 + pick - rank0)
    idx_sorted = jnp.zeros((N_PICKS, n_tok), jnp.int32)
    gate_sorted = jnp.zeros((N_PICKS, n_tok), F32)
    for e in range(N_PICKS):
        hit = pick == slot_s[e:e + 1, :]
        idx_sorted = jnp.where(hit, idx_s[e:e + 1, :], idx_sorted)
        gate_sorted = jnp.where(hit, gate_s[e:e + 1, :], gate_sorted)
    idxt_ref[...] = idx_sorted
    gatet_ref[...] = gate_sorted
    count0_ref[...] = count0
    idx = idx_sorted.astype(F32).T.astype(jnp.int32)
    loc0_ref[...] = _half_offsets(idx, 0)
    loc1_ref[...] = _half_offsets(idx, 1)


def _route(q, keys):
    row_spec = pl.BlockSpec((ROUTE_BLOCK, N_PICKS), lambda i: (i, 0))
    col_spec = pl.BlockSpec((N_PICKS, ROUTE_BLOCK), lambda i: (0, i))
    return pl.pallas_call(
        _route_kernel,
        grid=(N_TOK // ROUTE_BLOCK,),
        in_specs=[pl.BlockSpec((ROUTE_BLOCK, D_MODEL), lambda i: (i, 0)), _whole(keys.shape)],
        out_specs=[row_spec, row_spec, col_spec, col_spec,
                   pl.BlockSpec((1, ROUTE_BLOCK), lambda i: (0, i))],
        out_shape=[jax.ShapeDtypeStruct((N_TOK, N_PICKS), jnp.int32),
                   jax.ShapeDtypeStruct((N_TOK, N_PICKS), jnp.int32),
                   jax.ShapeDtypeStruct((N_PICKS, N_TOK), jnp.int32),
                   jax.ShapeDtypeStruct((N_PICKS, N_TOK), F32),
                   jax.ShapeDtypeStruct((1, N_TOK), jnp.int32)],
        scratch_shapes=[pltpu.VMEM((N_PICKS, ROUTE_BLOCK), jnp.int32),
                        pltpu.VMEM((N_PICKS, ROUTE_BLOCK), jnp.int32),
                        pltpu.VMEM((N_PICKS, ROUTE_BLOCK), F32)],
        compiler_params=pltpu.CompilerParams(vmem_limit_bytes=VMEM_LIMIT),
        name="route",
    )(q, keys)


N_GROUPS = N_PICKS // SUBLANES
WALK_GROUPS = 9


def _group_ranges(half, count0):
    if half == 0:
        return range(WALK_GROUPS), (WALK_GROUPS, (count0 + SUBLANES - 1) // SUBLANES)
    return range(N_GROUPS - WALK_GROUPS, N_GROUPS), (count0 // SUBLANES, N_GROUPS - WALK_GROUPS)


_MERGE_ORDER = (0, 4, 2, 6, 1, 5, 3, 7)


def _merge8(tiles, sub):
    m4 = sub < 4
    m2 = (sub & 3) < 2
    m1 = (sub & 1) == 0

    def l1(a, b):
        return jnp.where(m4, a, b) + pltpu.roll(jnp.where(m4, b, a), 4, 0)

    def l2(c, d):
        return jnp.where(m2, c, d) + jnp.where(m2, pltpu.roll(c, 6, 0), pltpu.roll(d, 2, 0))

    def l3(e, f):
        return jnp.where(m1, e, f) + jnp.where(m1, pltpu.roll(e, 7, 0), pltpu.roll(f, 1, 0))

    c = [l1(tiles[2 * i], tiles[2 * i + 1]) for i in range(4)]
    return l3(l2(c[0], c[1]), l2(c[2], c[3]))


def _peer_dots(half, cnt_ref, loc_ref, h_ref, tab_ref, actt_ref, gm):
    sub = lax.broadcasted_iota(jnp.int32, (SUBLANES, LANES), 0)
    lane = lax.broadcasted_iota(jnp.int32, (N_PICKS, PEER_BLOCK), 1)
    tok0 = pl.program_id(0) * PEER_BLOCK
    actt_ref[...] = jnp.zeros((N_PICKS, PEER_BLOCK), F32)
    for buf in gm:
        buf[...] = jnp.zeros((N_PICKS, LANES), F32)

    def group_sums(t, g):
        ht = h_ref[pl.ds(pl.multiple_of(t * SUBLANES, SUBLANES), SUBLANES), :]
        loc_row = loc_ref.at[t]
        prods = []
        for s in range(SUBLANES):
            r = pl.multiple_of(loc_row[g * SUBLANES + _MERGE_ORDER[s]], SUBLANES)
            prods.append(tab_ref[pl.ds(r, SUBLANES), :] * ht)
        return _merge8(prods, sub)

    def partial_sums(t, buf):
        for g in _group_ranges(half, 0)[0]:
            buf[g * SUBLANES:(g + 1) * SUBLANES, :] = group_sums(t, g)

    def extra_partial_sums(t, bufs):
        bounds = [_group_ranges(half, cnt_ref[tok0 + t + i])[1] for i in range(2)]

        @pl.when((bounds[0][1] > bounds[0][0]) | (bounds[1][1] > bounds[1][0]))
        def _():
            for i in range(2):
                def one(g, carry, i=i):
                    bufs[i][pl.ds(pl.multiple_of(g * SUBLANES, SUBLANES), SUBLANES), :] = group_sums(t + i, g)
                    return carry

                lax.fori_loop(*bounds[i], one, 0)

    def reduce_into_column(t, buf):
        col = jnp.sum(buf[...], axis=1, keepdims=True)
        actt_ref[...] = jnp.where(lane == t, col, actt_ref[...])

    def four_tokens(k, carry):
        t = 4 * k
        reduce_into_column(t - 2, gm[2])
        reduce_into_column(t - 1, gm[3])
        partial_sums(t, gm[0])
        partial_sums(t + 1, gm[1])
        extra_partial_sums(t, gm[0:2])
        reduce_into_column(t, gm[0])
        reduce_into_column(t + 1, gm[1])
        partial_sums(t + 2, gm[2])
        partial_sums(t + 3, gm[3])
        extra_partial_sums(t + 2, gm[2:4])
        return carry

    lax.fori_loop(0, PEER_BLOCK // 4, four_tokens, 0)
    reduce_into_column(PEER_BLOCK - 2, gm[2])
    reduce_into_column(PEER_BLOCK - 1, gm[3])


def _peer_u_first_kernel(cnt_ref, loc_ref, h_ref, tab_ref, actt_ref, *gm):
    _peer_dots(0, cnt_ref, loc_ref, h_ref, tab_ref, actt_ref, gm)


def _peer_u_second_kernel(cnt_ref, loc_ref, h_ref, tab_ref, act0t_ref, idxt_ref, gatet_ref, w0_ref, w1_ref,
                          actt_s, *gm):
    _peer_dots(1, cnt_ref, loc_ref, h_ref, tab_ref, actt_s, gm)
    in0 = _table_half(idxt_ref[...]) == 0
    w = gatet_ref[...] * _gelu(jnp.where(in0, act0t_ref[...], actt_s[...]))
    w0_ref[...] = jnp.where(in0, w, 0.0)
    w1_ref[...] = jnp.where(in0, 0.0, w)


def _peer_sum(half, cnt_ref, loc_ref, w_ref, tab_ref, init, f_ref, wb):
    tok0 = pl.program_id(0) * PEER_BLOCK

    def splat_weights(t, buf):
        buf[...] = jnp.take_along_axis(w_ref[...], jnp.full((N_PICKS, PEER_BLOCK), t, jnp.int32), axis=1)

    def add_group(t, g, accs, weight_row):
        loc_row = loc_ref.at[t]
        accs = list(accs)
        for s in range(SUBLANES):
            r = pl.multiple_of(loc_row[g * SUBLANES + s], SUBLANES)
            w = jnp.broadcast_to(weight_row(g * SUBLANES + s), (SUBLANES, LANES))
            accs[s % 4] = accs[s % 4] + tab_ref[pl.ds(r, SUBLANES), :] * w
        return tuple(accs)

    def token_rows(t):
        return pl.ds(pl.multiple_of(t * SUBLANES, SUBLANES), SUBLANES)

    def weighted_sum(t, buf):
        accs = (init(token_rows(t)),) + (jnp.zeros((SUBLANES, LANES), F32),) * 3
        for g in _group_ranges(half, 0)[0]:
            accs = add_group(t, g, accs, lambda e: buf[e:e + 1, :])
        return accs

    def total(accs):
        return (accs[0] + accs[1]) + (accs[2] + accs[3])

    def extra_weighted_sums(t, bufs):
        bounds = [_group_ranges(half, cnt_ref[tok0 + t + i])[1] for i in range(2)]

        @pl.when((bounds[0][1] > bounds[0][0]) | (bounds[1][1] > bounds[1][0]))
        def _():
            for i in range(2):
                accs = (f_ref[token_rows(t + i), :],) + (jnp.zeros((SUBLANES, LANES), F32),) * 3
                accs = lax.fori_loop(
                    *bounds[i],
                    lambda g, a, i=i: add_group(t + i, g, a, lambda e: bufs[i][pl.ds(e, 1), :]), accs)
                f_ref[token_rows(t + i), :] = total(accs)

    def token_pair(t, bufs, next_t, next_bufs):
        splat_weights(jnp.minimum(next_t, PEER_BLOCK - 1), next_bufs[0])
        splat_weights(jnp.minimum(next_t + 1, PEER_BLOCK - 1), next_bufs[1])
        f_ref[token_rows(t), :] = total(weighted_sum(t, bufs[0]))
        f_ref[token_rows(t + 1), :] = total(weighted_sum(t + 1, bufs[1]))
        extra_weighted_sums(t, bufs)

    splat_weights(0, wb[0])
    splat_weights(1, wb[1])

    def four_tokens(k, carry):
        t = 4 * k
        token_pair(t, wb[0:2], t + 2, wb[2:4])
        token_pair(t + 2, wb[2:4], t + 4, wb[0:2])
        return carry

    lax.fori_loop(0, PEER_BLOCK // 4, four_tokens, 0)


def _peer_v_first_kernel(cnt_ref, loc_ref, w_ref, tab_ref, f_ref, *wb):
    _peer_sum(0, cnt_ref, loc_ref, w_ref, tab_ref, lambda rows: jnp.zeros((SUBLANES, LANES), F32), f_ref, wb)


def _peer_v_second_kernel(cnt_ref, loc_ref, w_ref, tab_ref, f0_ref, f_ref, *wb):
    _peer_sum(1, cnt_ref, loc_ref, w_ref, tab_ref, lambda rows: f0_ref[rows, :], f_ref, wb)


def _peer(h2, loc0, loc1, idx_t, gate_t, count0, u_tab, v_tab, layer):
    nblk = N_TOK // PEER_BLOCK
    h_tiles = h2.reshape(N_TOK * SUBLANES, LANES)
    cnt_spec = pl.BlockSpec(memory_space=pltpu.SMEM)
    smem_spec = pl.BlockSpec((PEER_BLOCK, N_PICKS), lambda i: (i, 0), memory_space=pltpu.SMEM)
    row_spec = pl.BlockSpec((PEER_BLOCK, N_PICKS), lambda i: (i, 0))
    col_spec = pl.BlockSpec((N_PICKS, PEER_BLOCK), lambda i: (0, i))
    tile_spec = pl.BlockSpec((PEER_BLOCK * SUBLANES, LANES), lambda i: (i, 0))
    half_spec = lambda half: pl.BlockSpec((None, HALF_EXPERTS * SUBLANES, LANES),
                                          lambda i: (layer, half, 0), pipeline_mode=pl.Buffered(1))
    row_shape = jax.ShapeDtypeStruct((N_TOK, N_PICKS), F32)
    col_shape = jax.ShapeDtypeStruct((N_PICKS, N_TOK), F32)
    tile_shape = jax.ShapeDtypeStruct((N_TOK * SUBLANES, LANES), F32)
    params = pltpu.CompilerParams(vmem_limit_bytes=VMEM_LIMIT)
    pair_scratch = [pltpu.VMEM((N_PICKS, LANES), F32)] * 4

    act0_t = pl.pallas_call(
        _peer_u_first_kernel, grid=(nblk,),
        in_specs=[cnt_spec, smem_spec, tile_spec, half_spec(0)],
        out_specs=col_spec, out_shape=col_shape, scratch_shapes=pair_scratch,
        compiler_params=params, name="peer_u0",
    )(count0, loc0, h_tiles, u_tab)
    w0, w1 = pl.pallas_call(
        _peer_u_second_kernel, grid=(nblk,),
        in_specs=[cnt_spec, smem_spec, tile_spec, half_spec(1), col_spec, col_spec, col_spec],
        out_specs=[col_spec, col_spec], out_shape=[col_shape, col_shape],
        scratch_shapes=[pltpu.VMEM((N_PICKS, PEER_BLOCK), F32)] + pair_scratch,
        compiler_params=params, name="peer_u1",
    )(count0, loc1, h_tiles, u_tab, act0_t, idx_t, gate_t)
    f0 = pl.pallas_call(
        _peer_v_first_kernel, grid=(nblk,),
        in_specs=[cnt_spec, smem_spec, col_spec, half_spec(0)],
        out_specs=tile_spec, out_shape=tile_shape, scratch_shapes=pair_scratch,
        compiler_params=params, name="peer_v0",
    )(count0, loc0, w0, v_tab)
    f = pl.pallas_call(
        _peer_v_second_kernel, grid=(nblk,),
        in_specs=[cnt_spec, smem_spec, col_spec, half_spec(1), tile_spec],
        out_specs=tile_spec, out_shape=tile_shape, scratch_shapes=pair_scratch,
        compiler_params=params, name="peer_v1",
    )(count0, loc1, w1, v_tab, f0)
    return f.reshape(N_TOK, D_MODEL)


def _close_kernel(x1_ref, f_ref, mod_ref, g_ref, b_ref, o_ref):
    g2 = mod_ref[...][:, 5 * D_MODEL:6 * D_MODEL]
    o_ref[...] = _ln(ALPHA * x1_ref[...] + g2 * f_ref[...]) * g_ref[1:2, :] + b_ref[1:2, :]


def _close(x1, f, mods, ln_g, ln_b, layer):
    row_spec = pl.BlockSpec((ROW_BLOCK, D_MODEL), lambda i: (i, 0))
    vec_spec = pl.BlockSpec((None, 2, D_MODEL), lambda i: (layer, 0, 0))
    return pl.pallas_call(
        _close_kernel,
        grid=(N_ROW_BLOCKS,),
        in_specs=[row_spec, row_spec,
                  pl.BlockSpec((None, None, 1, 6 * D_MODEL), lambda i: (layer, _mod_group(i), 0, 0)),
                  vec_spec, vec_spec],
        out_specs=row_spec,
        out_shape=jax.ShapeDtypeStruct((N_TOK, D_MODEL), F32),
        compiler_params=pltpu.CompilerParams(vmem_limit_bytes=VMEM_LIMIT),
        name="close",
    )(x1, f, mods, ln_g, ln_b)


def _rope_tables(dim, copies):
    rows = LAT_LEN // GRID_W
    row = jnp.repeat(jnp.arange(rows, dtype=F32), GRID_W)
    col = jnp.tile(jnp.arange(GRID_W, dtype=F32), rows)
    nf = dim // 4
    inv = ROPE_BASE ** (-jnp.arange(nf, dtype=F32) / nf)
    ar = row[:, None] * inv
    ac = col[:, None] * inv
    ang = jnp.concatenate([ar, ar, ac, ac], axis=-1)
    cos, sin = jnp.cos(ang), jnp.sin(ang)
    even_quarter = (jnp.arange(dim) // nf) % 2 == 0
    sin_up = jnp.where(even_quarter, -sin, 0.0)
    sin_dn = jnp.where(even_quarter, 0.0, sin)
    return tuple(jnp.tile(t, (1, copies)) for t in (cos, sin_up, sin_dn))


def kernel(x_prompt, x_sample, c, cache_win_k, cache_win_v, cache_diff_k, cache_diff_v, c_ctx, w_mod, b_mod, w_in, w_out, chunk_w, chunk_b, win_sink, diff_lam_q, diff_lam_k, diff_subln_g, pool_w, pool_scale, ln_g, ln_b, peer_wq, peer_keys, peer_u, peer_v):
    cvec = jnp.concatenate([c_ctx[None, :], c, jnp.zeros((8 - 1 - N_LAT_SEQ, D_MODEL), F32)], axis=0)
    mods = _modulation(cvec, w_mod, b_mod)[:, :1 + N_LAT_SEQ].reshape(DEPTH, 1 + N_LAT_SEQ, 1, 6 * D_MODEL)

    w_in_b = w_in.astype(BF16)
    w_out_b = w_out.astype(BF16)
    wq_b = peer_wq.astype(BF16)
    cw_b = chunk_w.astype(BF16)
    cb_full = jnp.repeat(jnp.swapaxes(chunk_b, 1, 2), HEAD_D, axis=2)
    eye = jnp.eye(4, dtype=F32)
    wbd = jnp.einsum('lgcd,gh->lgchd', pool_w, eye).reshape(DEPTH, GROUP_W, GROUP_W).astype(BF16)
    caches = (cache_win_k.reshape(N_LAT_SEQ, DEPTH, CTX_LEN, 128),
              cache_win_v.reshape(N_LAT_SEQ, DEPTH, CTX_LEN, 128),
              cache_diff_k.reshape(N_LAT_SEQ, DEPTH, CTX_LEN, GROUP_W),
              cache_diff_v.reshape(N_LAT_SEQ, DEPTH, CTX_LEN, GROUP_W))
    rope_tabs = _rope_tables(HEAD_D, GROUP_W // HEAD_D) + _rope_tables(C_SUB, GROUP_W // C_SUB)
    u_tiles = _split_tables(peer_u)
    v_tiles = _split_tables(peer_v)

    x = jnp.concatenate([x_prompt.reshape(N_CTX_TOK, D_MODEL), x_sample.reshape(N_LAT_TOK, D_MODEL)], axis=0)
    kbs, vbs, kcs, vcs = [], [], [], []
    for l in range(DEPTH):
        lam_init = 0.8 - 0.6 * math.exp(-0.3 * l)
        layer_w = (cw_b[l], cb_full[l], win_sink[l], diff_lam_q[l], diff_lam_k[l],
                   diff_subln_g[l].reshape(1, HEAD_D), wbd[l], pool_scale[l].reshape(1, GROUP_W))
        p = _in_proj(x, mods, w_in_b, l)
        pc = p[:N_CTX_TOK].reshape(N_CTX_SEQ, CTX_LEN, P_COLS)
        kbs.append(pc[..., COL_BK:COL_BV].reshape(N_CTX_SEQ, CTX_LEN, 2, HEAD_D))
        vbs.append(pc[..., COL_BV:COL_CQ].reshape(N_CTX_SEQ, CTX_LEN, 2, HEAD_D))
        kcs.append(pc[..., COL_CK:COL_CV].reshape(N_CTX_SEQ, CTX_LEN, 4, 2, C_SUB))
        vcs.append(pc[..., COL_CV:COL_DZ].reshape(N_CTX_SEQ, CTX_LEN, 4, HEAD_D))
        o = jnp.concatenate([_ctx_mix(p, layer_w, lam_init),
                             _lat_mix(p, caches, rope_tabs, layer_w, lam_init, l)], axis=0)
        x1, h2, q = _out_proj(o, x, mods, w_out_b, ln_g, ln_b, wq_b, l)
        loc0, loc1, idx_t, gate_t, count0 = _route(q, peer_keys[l])
        f = _peer(h2, loc0, loc1, idx_t, gate_t, count0.reshape(N_TOK), u_tiles, v_tiles, l)
        x = _close(x1, f, mods, ln_g, ln_b, l)
    return (x[:N_CTX_TOK].reshape(N_CTX_SEQ, CTX_LEN, D_MODEL),
            x[N_CTX_TOK:].reshape(N_LAT_SEQ, LAT_LEN, D_MODEL),
            jnp.stack(kbs, axis=1), jnp.stack(vbs, axis=1), jnp.stack(kcs, axis=1), jnp.stack(vcs, axis=1))
```

```python
import functools
import math

import jax
import jax.numpy as jnp
import numpy as np
from jax import lax
from jax.experimental import pallas as pl
from jax.experimental.pallas import tpu as pltpu

F32 = jnp.float32
BF16 = jnp.bfloat16
HIGHEST = lax.Precision.HIGHEST

D_MODEL = 1024
DEPTH = 4
N_CTX_SEQ = 16
CTX_LEN = 256
N_LAT_SEQ = 2
LAT_LEN = 1024
N_CTX_TOK = N_CTX_SEQ * CTX_LEN
N_LAT_TOK = N_LAT_SEQ * LAT_LEN
N_TOK = N_CTX_TOK + N_LAT_TOK
GRID_W = 64
BLOCK = 128
GROUP_W = 256
HEAD_D = 64
C_SUB = 32
WINDOW = 128
POOL_SIZES = (2, 4, 8, 16)
P_COLS = 2048
COL_AU, COL_AV, COL_BQ, COL_BK, COL_BV, COL_CQ, COL_CK, COL_CV, COL_DZ = (
    0, 256, 512, 768, 896, 1024, 1280, 1536, 1792)
PK_HEADS = 8
N_KEYS = 128
N_EXPERTS = N_KEYS * N_KEYS
PK_TOPK = 16
N_PICKS = PK_HEADS * PK_TOPK
ROPE_BASE = 10000.0
ALPHA = (2 * DEPTH) ** 0.25
LN_EPS = 1e-5
NEG = -1e30

ROW_BLOCK = 512
N_ROW_BLOCKS = N_TOK // ROW_BLOCK
N_CTX_ROW_BLOCKS = N_CTX_TOK // ROW_BLOCK
ROW_BLOCKS_PER_LAT_SEQ = LAT_LEN // ROW_BLOCK
ROUTE_BLOCK = 128
PEER_BLOCK = 128
SUBLANES = 8
LANES = 128
HALF_EXPERTS = N_EXPERTS // 2
VMEM_LIMIT = 56 * 1024 * 1024


def _mm(a, b):
    return jnp.dot(a.astype(BF16), b.astype(BF16), preferred_element_type=F32)


def _mm_nt(a, b):
    return lax.dot_general(a.astype(BF16), b.astype(BF16), (((1,), (1,)), ((), ())),
                           preferred_element_type=F32)


def _gelu(x):
    return x * (0.5 * (1.0 + jnp.tanh(0.7978845608028654 * (x + 0.044715 * (x * x * x)))))


def _ln(x):
    mu = jnp.mean(x, axis=-1, keepdims=True)
    xc = x - mu
    var = jnp.mean(xc * xc, axis=-1, keepdims=True)
    return xc * lax.rsqrt(var + LN_EPS)


def _mod_group(i):
    return jnp.where(i < N_CTX_ROW_BLOCKS, 0, 1 + (i - N_CTX_ROW_BLOCKS) // ROW_BLOCKS_PER_LAT_SEQ)


def _mod_kernel(c_ref, w_ref, b_ref, o_ref):
    c = c_ref[...]
    s = c * jax.nn.sigmoid(c)
    o_ref[...] = jnp.dot(s, w_ref[...], precision=HIGHEST, preferred_element_type=F32) + b_ref[...]


def _modulation(cvec, w_mod, b_mod):
    nj = 4
    cols = 6 * D_MODEL // nj
    return pl.pallas_call(
        _mod_kernel,
        grid=(DEPTH, nj),
        in_specs=[pl.BlockSpec((8, D_MODEL), lambda l, j: (0, 0)),
                  pl.BlockSpec((None, D_MODEL, cols), lambda l, j: (l, 0, j)),
                  pl.BlockSpec((None, 1, cols), lambda l, j: (l, 0, j))],
        out_specs=pl.BlockSpec((None, 8, cols), lambda l, j: (l, 0, j)),
        out_shape=jax.ShapeDtypeStruct((DEPTH, 8, 6 * D_MODEL), F32),
        compiler_params=pltpu.CompilerParams(vmem_limit_bytes=VMEM_LIMIT),
        name="modulation",
    )(cvec, w_mod, b_mod.reshape(DEPTH, 1, 6 * D_MODEL))


def _in_proj_kernel(x_ref, mod_ref, w_ref, o_ref):
    m = mod_ref[...]
    h = _ln(x_ref[...]) * (1.0 + m[:, D_MODEL:2 * D_MODEL]) + m[:, 0:D_MODEL]
    o_ref[...] = jnp.dot(h.astype(BF16), w_ref[...], preferred_element_type=F32)


def _in_proj(x, mods, w_in_bf16, layer):
    return pl.pallas_call(
        _in_proj_kernel,
        grid=(N_ROW_BLOCKS,),
        in_specs=[pl.BlockSpec((ROW_BLOCK, D_MODEL), lambda i: (i, 0)),
                  pl.BlockSpec((None, None, 1, 6 * D_MODEL), lambda i: (layer, _mod_group(i), 0, 0)),
                  pl.BlockSpec((None, D_MODEL, P_COLS), lambda i: (layer, 0, 0))],
        out_specs=pl.BlockSpec((ROW_BLOCK, P_COLS), lambda i: (i, 0)),
        out_shape=jax.ShapeDtypeStruct((N_TOK, P_COLS), F32),
        compiler_params=pltpu.CompilerParams(vmem_limit_bytes=VMEM_LIMIT),
        name="in_proj",
    )(x, mods, w_in_bf16)


def _chunk_gating(u_raw, v_raw, cw_ref, cb_full):
    u = _gelu(u_raw)
    v = _gelu(v_raw).astype(BF16)
    head = lax.broadcasted_iota(jnp.int32, (BLOCK, GROUP_W), 1) // HEAD_D
    mixed = cb_full
    for h in range(4):
        mixed = mixed + jnp.where(head == h, jnp.dot(cw_ref[h], v, preferred_element_type=F32), 0.0)
    return u * mixed


def _pool(z_rows, z_win, row0, key0, seq_len, wbd_ref, pool_scale):
    r, k = z_rows.shape[0], z_win.shape[0]
    t = row0 + lax.broadcasted_iota(jnp.int32, (r, k), 0)
    pos = key0 + lax.broadcasted_iota(jnp.int32, (r, k), 1)
    t1 = row0 + lax.broadcasted_iota(jnp.int32, (r, 1), 0)
    z_hi = z_win.astype(BF16)
    z_lo = (z_win - z_hi.astype(F32)).astype(BF16)
    group = lax.broadcasted_iota(jnp.int32, (r, GROUP_W), 1) // HEAD_D
    mean = jnp.zeros((r, GROUP_W), F32)
    for g, w in enumerate(POOL_SIZES):
        lo = jnp.maximum(t - w // 2, 0)
        hi = jnp.minimum(t + w // 2, seq_len)
        band = jnp.where((pos >= lo) & (pos < hi), 1.0, 0.0).astype(BF16)
        tot = (jnp.dot(band, z_hi, preferred_element_type=F32)
               + jnp.dot(band, z_lo, preferred_element_type=F32))
        cnt = (jnp.minimum(t1 + w // 2, seq_len) - jnp.maximum(t1 - w // 2, 0)).astype(F32)
        mean = mean + jnp.where(group == g, tot / cnt, 0.0)
    pooled = mean - z_rows
    return jnp.dot(pooled.astype(BF16), wbd_ref[...], preferred_element_type=F32) * pool_scale


def _diff_lambda(lq_ref, lk_ref, lam_init):
    lq = lq_ref[...]
    lk = lk_ref[...]
    s0 = jnp.sum(lq[0:1] * lk[0:1], axis=-1, keepdims=True)
    s1 = jnp.sum(lq[1:2] * lk[1:2], axis=-1, keepdims=True)
    return jnp.exp(s0) - jnp.exp(s1) + lam_init


def _sub_norm(o, g, lam_init):
    ms = jnp.mean(o * o, axis=-1, keepdims=True)
    return o * lax.rsqrt(ms + LN_EPS) * g * (1.0 - lam_init)


def _ctx_mix_kernel(lam_init, p_ref, cw_ref, cb_ref, sink_ref, lq_ref, lk_ref, sg_ref, wbd_ref,
                    ps_ref, o_ref):
    s_len = CTX_LEN
    for n in range(s_len // BLOCK):
        rows = slice(n * BLOCK, (n + 1) * BLOCK)
        o_ref[rows, 0:GROUP_W] = _chunk_gating(p_ref[rows, COL_AU:COL_AU + GROUP_W],
                                               p_ref[rows, COL_AV:COL_AV + GROUP_W], cw_ref, cb_ref[...])
    for kv in range(2):
        kk = p_ref[:, COL_BK + kv * HEAD_D:COL_BK + (kv + 1) * HEAD_D].astype(BF16)
        vv = p_ref[:, COL_BV + kv * HEAD_D:COL_BV + (kv + 1) * HEAD_D].astype(BF16)
        for g in range(2):
            h = kv * 2 + g
            q = p_ref[:, COL_BQ + h * HEAD_D:COL_BQ + (h + 1) * HEAD_D] * (HEAD_D ** -0.5)
            s = _mm_nt(q, kk)
            sk = sink_ref[h]
            m = jnp.maximum(jnp.max(s, axis=-1, keepdims=True), sk)
            e = jnp.exp(s - m)
            den = jnp.sum(e, axis=-1, keepdims=True) + jnp.exp(sk - m)
            o_ref[:, GROUP_W + h * HEAD_D:GROUP_W + (h + 1) * HEAD_D] = _mm(e, vv) / den
    lam = _diff_lambda(lq_ref, lk_ref, lam_init)
    scale = C_SUB ** -0.5
    for h in range(4):
        vv = p_ref[:, COL_CV + h * HEAD_D:COL_CV + (h + 1) * HEAD_D].astype(BF16)
        parts = []
        for j in range(2):
            c0 = h * HEAD_D + j * C_SUB
            s = _mm_nt(p_ref[:, COL_CQ + c0:COL_CQ + c0 + C_SUB],
                       p_ref[:, COL_CK + c0:COL_CK + c0 + C_SUB]) * scale
            e = jnp.exp(s - jnp.max(s, axis=-1, keepdims=True))
            parts.append(_mm(e, vv) / jnp.sum(e, axis=-1, keepdims=True))
        o = parts[0] - lam * parts[1]
        o_ref[:, 2 * GROUP_W + h * HEAD_D:2 * GROUP_W + (h + 1) * HEAD_D] = _sub_norm(o, sg_ref[...], lam_init)
    z = p_ref[:, COL_DZ:COL_DZ + GROUP_W]
    o_ref[:, 3 * GROUP_W:4 * GROUP_W] = _pool(z, z, 0, 0, s_len, wbd_ref, ps_ref[...])


def _whole(shape):
    nd = len(shape)
    return pl.BlockSpec(shape, lambda *_: (0,) * nd)


def _ctx_mix(p, layer_w, lam_init):
    cw, cb_full, sink, lq, lk, sg, wbd, ps = layer_w
    return pl.pallas_call(
        functools.partial(_ctx_mix_kernel, lam_init),
        grid=(N_CTX_SEQ,),
        in_specs=[pl.BlockSpec((CTX_LEN, P_COLS), lambda b: (b, 0)),
                  _whole(cw.shape), _whole(cb_full.shape),
                  pl.BlockSpec(memory_space=pltpu.SMEM),
                  _whole(lq.shape), _whole(lk.shape), _whole(sg.shape), _whole(wbd.shape),
                  _whole(ps.shape)],
        out_specs=pl.BlockSpec((CTX_LEN, D_MODEL), lambda b: (b, 0)),
        out_shape=jax.ShapeDtypeStruct((N_CTX_TOK, D_MODEL), F32),
        compiler_params=pltpu.CompilerParams(vmem_limit_bytes=VMEM_LIMIT),
        name="ctx_mix",
    )(p, cw, cb_full, sink, lq, lk, sg, wbd, ps)


def _rope(x, cos, sin_up, sin_dn, shift):
    outs = []
    for c0 in range(0, x.shape[1], LANES):
        xs = x[:, c0:c0 + LANES]
        up = pltpu.roll(xs, LANES - shift, 1)
        dn = pltpu.roll(xs, shift, 1)
        outs.append(xs * cos[:, c0:c0 + LANES] + up * sin_up[:, c0:c0 + LANES]
                    + dn * sin_dn[:, c0:c0 + LANES])
    return outs[0] if len(outs) == 1 else jnp.concatenate(outs, axis=1)


def _lat_mix_kernel(lam_init, p_ref, wk_ref, wv_ref, dk_ref, dv_ref,
                    cosb_ref, sbu_ref, sbd_ref, cosc_ref, scu_ref, scd_ref,
                    cw_ref, cb_ref, sink_ref, lq_ref, lk_ref, sg_ref, wbd_ref, ps_ref,
                    o_ref, kb_s, kc_s):
    n = pl.program_id(1)
    s_len = LAT_LEN

    @pl.when(n == 0)
    def _():
        kb_s[...] = _rope(p_ref[:, COL_BK:COL_BK + 128], cosb_ref[:, 0:128], sbu_ref[:, 0:128],
                          sbd_ref[:, 0:128], HEAD_D // 4).astype(BF16)
        kc_s[...] = _rope(p_ref[:, COL_CK:COL_CK + GROUP_W], cosc_ref[...], scu_ref[...],
                          scd_ref[...], C_SUB // 4).astype(BF16)

    r0 = pl.multiple_of(n * BLOCK, BLOCK)
    rows = pl.ds(r0, BLOCK)
    o_ref[:, 0:GROUP_W] = _chunk_gating(p_ref[rows, COL_AU:COL_AU + GROUP_W],
                                        p_ref[rows, COL_AV:COL_AV + GROUP_W], cw_ref, cb_ref[...])
    w0 = pl.multiple_of(jnp.clip((n - 1) * BLOCK, 0, s_len - 3 * BLOCK), BLOCK)
    win = pl.ds(w0, 3 * BLOCK)
    qb = _rope(p_ref[rows, COL_BQ:COL_BQ + GROUP_W], cosb_ref[rows, :], sbu_ref[rows, :],
               sbd_ref[rows, :], HEAD_D // 4) * (HEAD_D ** -0.5)
    qpos = r0 + lax.broadcasted_iota(jnp.int32, (BLOCK, 3 * BLOCK), 0)
    kpos = w0 + lax.broadcasted_iota(jnp.int32, (BLOCK, 3 * BLOCK), 1)
    valid = jnp.abs(kpos - qpos) <= WINDOW
    for kv in range(2):
        cs = slice(kv * HEAD_D, (kv + 1) * HEAD_D)
        k_loc = kb_s[win, cs]
        v_loc = p_ref[win, COL_BV + kv * HEAD_D:COL_BV + (kv + 1) * HEAD_D].astype(BF16)
        k_ctx = wk_ref[:, cs].astype(BF16)
        v_ctx = wv_ref[:, cs].astype(BF16)
        for g in range(2):
            h = kv * 2 + g
            q = qb[:, h * HEAD_D:(h + 1) * HEAD_D]
            s_loc = jnp.where(valid, _mm_nt(q, k_loc), NEG)
            s_ctx = _mm_nt(q, k_ctx)
            sk = sink_ref[h]
            m = jnp.maximum(jnp.maximum(jnp.max(s_loc, axis=-1, keepdims=True),
                                        jnp.max(s_ctx, axis=-1, keepdims=True)), sk)
            e_loc = jnp.exp(s_loc - m)
            e_ctx = jnp.exp(s_ctx - m)
            den = (jnp.sum(e_loc, axis=-1, keepdims=True) + jnp.sum(e_ctx, axis=-1, keepdims=True)
                   + jnp.exp(sk - m))
            o_ref[:, GROUP_W + h * HEAD_D:GROUP_W + (h + 1) * HEAD_D] = (
                _mm(e_loc, v_loc) + _mm(e_ctx, v_ctx)) / den
    lam = _diff_lambda(lq_ref, lk_ref, lam_init)
    scale = C_SUB ** -0.5
    qc = _rope(p_ref[rows, COL_CQ:COL_CQ + GROUP_W], cosc_ref[rows, :], scu_ref[rows, :],
               scd_ref[rows, :], C_SUB // 4)
    for h in range(4):
        v_ctx = dv_ref[:, h * HEAD_D:(h + 1) * HEAD_D].astype(BF16)
        v_lat = p_ref[:, COL_CV + h * HEAD_D:COL_CV + (h + 1) * HEAD_D].astype(BF16)
        parts = []
        for j in range(2):
            cs = slice(h * HEAD_D + j * C_SUB, h * HEAD_D + (j + 1) * C_SUB)
            q = qc[:, cs]
            s_ctx = _mm_nt(q, dk_ref[:, cs]) * scale
            s_lat = _mm_nt(q, kc_s[:, cs]) * scale
            m = jnp.maximum(jnp.max(s_ctx, axis=-1, keepdims=True), jnp.max(s_lat, axis=-1, keepdims=True))
            e_ctx = jnp.exp(s_ctx - m)
            e_lat = jnp.exp(s_lat - m)
            den = jnp.sum(e_ctx, axis=-1, keepdims=True) + jnp.sum(e_lat, axis=-1, keepdims=True)
            parts.append((_mm(e_ctx, v_ctx) + _mm(e_lat, v_lat)) / den)
        o = parts[0] - lam * parts[1]
        o_ref[:, 2 * GROUP_W + h * HEAD_D:2 * GROUP_W + (h + 1) * HEAD_D] = _sub_norm(o, sg_ref[...], lam_init)
    o_ref[:, 3 * GROUP_W:4 * GROUP_W] = _pool(p_ref[rows, COL_DZ:COL_DZ + GROUP_W],
                                              p_ref[win, COL_DZ:COL_DZ + GROUP_W],
                                              r0, w0, s_len, wbd_ref, ps_ref[...])


def _lat_mix(p, caches, rope_tabs, layer_w, lam_init, layer):
    cw, cb_full, sink, lq, lk, sg, wbd, ps = layer_w
    nb = LAT_LEN // BLOCK
    ctx_blocks = N_CTX_TOK // LAT_LEN
    cache_spec = lambda width: pl.BlockSpec((None, None, CTX_LEN, width), lambda b, n: (b, layer, 0, 0))
    rope_spec = pl.BlockSpec((LAT_LEN, GROUP_W), lambda b, n: (0, 0))
    return pl.pallas_call(
        functools.partial(_lat_mix_kernel, lam_init),
        grid=(N_LAT_SEQ, nb),
        in_specs=[pl.BlockSpec((LAT_LEN, P_COLS), lambda b, n: (ctx_blocks + b, 0)),
                  cache_spec(128), cache_spec(128), cache_spec(256), cache_spec(256)]
                 + [rope_spec] * 6
                 + [_whole(cw.shape), _whole(cb_full.shape), pl.BlockSpec(memory_space=pltpu.SMEM),
                    _whole(lq.shape), _whole(lk.shape), _whole(sg.shape), _whole(wbd.shape),
                    _whole(ps.shape)],
        out_specs=pl.BlockSpec((BLOCK, D_MODEL), lambda b, n: (b * nb + n, 0)),
        out_shape=jax.ShapeDtypeStruct((N_LAT_TOK, D_MODEL), F32),
        scratch_shapes=[pltpu.VMEM((LAT_LEN, 128), BF16), pltpu.VMEM((LAT_LEN, GROUP_W), BF16)],
        compiler_params=pltpu.CompilerParams(vmem_limit_bytes=VMEM_LIMIT,
                                             dimension_semantics=("arbitrary", "arbitrary")),
        name="lat_mix",
    )(p, *caches, *rope_tabs, cw, cb_full, sink, lq, lk, sg, wbd, ps)


def _out_proj_kernel(o_ref, x_ref, mod_ref, wo_ref, g_ref, b_ref, wq_ref, x1_ref, h2_ref, q_ref):
    m = mod_ref[...]
    g1 = m[:, 2 * D_MODEL:3 * D_MODEL]
    sh2 = m[:, 3 * D_MODEL:4 * D_MODEL]
    sc2 = m[:, 4 * D_MODEL:5 * D_MODEL]
    y = jnp.dot(o_ref[...].astype(BF16), wo_ref[...], preferred_element_type=F32)
    x1 = _ln(ALPHA * x_ref[...] + g1 * y) * g_ref[0:1, :] + b_ref[0:1, :]
    x1_ref[...] = x1
    h2 = _ln(x1) * (1.0 + sc2) + sh2
    h2_ref[...] = h2
    q_ref[...] = jnp.dot(h2.astype(BF16), wq_ref[...], preferred_element_type=F32)


def _out_proj(o, x, mods, w_out_bf16, ln_g, ln_b, wq_bf16, layer):
    row_spec = pl.BlockSpec((ROW_BLOCK, D_MODEL), lambda i: (i, 0))
    mat_spec = pl.BlockSpec((None, D_MODEL, D_MODEL), lambda i: (layer, 0, 0))
    vec_spec = pl.BlockSpec((None, 2, D_MODEL), lambda i: (layer, 0, 0))
    shape = jax.ShapeDtypeStruct((N_TOK, D_MODEL), F32)
    return pl.pallas_call(
        _out_proj_kernel,
        grid=(N_ROW_BLOCKS,),
        in_specs=[row_spec, row_spec,
                  pl.BlockSpec((None, None, 1, 6 * D_MODEL), lambda i: (layer, _mod_group(i), 0, 0)),
                  mat_spec, vec_spec, vec_spec, mat_spec],
        out_specs=[row_spec, row_spec, row_spec],
        out_shape=[shape, shape, shape],
        compiler_params=pltpu.CompilerParams(vmem_limit_bytes=VMEM_LIMIT),
        name="out_proj",
    )(o, x, mods, w_out_bf16, ln_g, ln_b, wq_bf16)


def _top16_rows(sc, order):
    out_row = lax.broadcasted_iota(jnp.int32, (PK_TOPK, sc.shape[1]), 0)
    vals = jnp.zeros((PK_TOPK, sc.shape[1]), F32)
    sel = jnp.zeros((PK_TOPK, sc.shape[1]), jnp.int32)
    for i in range(PK_TOPK):
        m = jnp.max(sc, axis=0, keepdims=True)
        am = jnp.min(jnp.where(sc == m, order, jnp.int32(2 ** 30)), axis=0, keepdims=True)
        vals = jnp.where(out_row == i, m, vals)
        sel = jnp.where(out_row == i, am, sel)
        sc = jnp.where(order == am, NEG, sc)
    return vals, sel


def _pair_candidates(sv, si):
    n_tok = sv[0].shape[1]
    rows = lax.broadcasted_iota(jnp.int32, (SUBLANES, n_tok), 0)
    cands, flats, experts = [], [], []

    def piece(a, b0, valid):
        v = sv[0][a:a + 1, :] + sv[1][b0:b0 + SUBLANES, :]
        cands.append(v if valid >= SUBLANES else jnp.where(rows < valid, v, NEG))
        flats.append(a * PK_TOPK + b0 + rows)
        experts.append(si[0][a:a + 1, :] * N_KEYS + si[1][b0:b0 + SUBLANES, :])

    piece(0, 0, SUBLANES)
    piece(0, SUBLANES, SUBLANES)
    for a in range(1, SUBLANES):
        piece(a, 0, PK_TOPK // (a + 1))
    cands.append(sv[0][SUBLANES:, :] + sv[1][0:1, :])
    flats.append((SUBLANES + rows) * PK_TOPK)
    experts.append(si[0][SUBLANES:, :] * N_KEYS + si[1][0:1, :])
    order = jnp.concatenate(flats, axis=0) * N_EXPERTS + jnp.concatenate(experts, axis=0)
    return jnp.concatenate(cands, axis=0), order


RUN_EXPERTS = N_KEYS // 2
RUNS_PER_PARITY = N_KEYS // 2


def _table_half(idx):
    return ((idx >> 7) ^ (idx >> 6)) & 1


def _half_offsets(idx, half):
    i1 = idx >> 7
    local = ((i1 & 1) * RUNS_PER_PARITY + (i1 >> 1)) * RUN_EXPERTS + (idx & (RUN_EXPERTS - 1))
    return jnp.where(_table_half(idx) == half, local, 0) * SUBLANES


def _load_table_half(half, layer, tab_hbm, tab_s, sem):
    @pl.when(pl.program_id(0) == 0)
    def _():
        run_rows = RUN_EXPERTS * SUBLANES
        copies = []
        for parity in range(2):
            for pair in range(RUNS_PER_PARITY):
                first = ((2 * pair + parity) * N_KEYS + (half ^ parity) * RUN_EXPERTS) * SUBLANES
                slot = (parity * RUNS_PER_PARITY + pair) * run_rows
                copies.append(pltpu.make_async_copy(tab_hbm.at[layer, pl.ds(first, run_rows), :],
                                                    tab_s.at[pl.ds(slot, run_rows), :], sem))
        for copy in copies:
            copy.start()
        for copy in copies:
            copy.wait()


def _route_kernel(q_ref, keys_ref, loc0_ref, loc1_ref, idxt_ref, gatet_ref, count0_ref,
                  slot_s, idx_s, gate_s):
    n_tok = q_ref.shape[0]
    key_row = lax.broadcasted_iota(jnp.int32, (N_KEYS, n_tok), 0)
    for h in range(PK_HEADS):
        sv = []
        si = []
        for j in range(2):
            c0 = (h * 2 + j) * HEAD_D
            sc = lax.dot_general(keys_ref[j], q_ref[:, c0:c0 + HEAD_D], (((1,), (1,)), ((), ())),
                                 precision=HIGHEST, preferred_element_type=F32)
            v, i = _top16_rows(sc, key_row)
            sv.append(v)
            si.append(i)
        cv, winners = _top16_rows(*_pair_candidates(sv, si))
        ce = winners & (N_EXPERTS - 1)
        e = jnp.exp(cv - jnp.max(cv, axis=0, keepdims=True))
        gate_s[h * PK_TOPK:(h + 1) * PK_TOPK, :] = e / jnp.sum(e, axis=0, keepdims=True)
        idx_s[h * PK_TOPK:(h + 1) * PK_TOPK, :] = ce
    in0 = _table_half(idx_s[...]) == 0
    before = (lax.broadcasted_iota(jnp.int32, (N_PICKS, N_PICKS), 1)
              < lax.broadcasted_iota(jnp.int32, (N_PICKS, N_PICKS), 0))
    rank0 = jnp.dot(jnp.where(before, 1.0, 0.0).astype(BF16), jnp.where(in0, 1.0, 0.0).astype(BF16),
                    preferred_element_type=F32).astype(jnp.int32)
    count0 = jnp.sum(jnp.where(in0, 1, 0), axis=0, keepdims=True)
    pick = lax.broadcasted_iota(jnp.int32, (N_PICKS, n_tok), 0)
    slot_s[...] = jnp.where(in0, rank0, count0 + pick - rank0)
    idx_sorted = jnp.zeros((N_PICKS, n_tok), jnp.int32)
    gate_sorted = jnp.zeros((N_PICKS, n_tok), F32)
    for e in range(N_PICKS):
        hit = pick == slot_s[e:e + 1, :]
        idx_sorted = jnp.where(hit, idx_s[e:e + 1, :], idx_sorted)
        gate_sorted = jnp.where(hit, gate_s[e:e + 1, :], gate_sorted)
    idxt_ref[...] = idx_sorted
    gatet_ref[...] = gate_sorted
    count0_ref[...] = count0
    idx = idx_sorted.astype(F32).T.astype(jnp.int32)
    loc0_ref[...] = _half_offsets(idx, 0)
    loc1_ref[...] = _half_offsets(idx, 1)


def _route(q, keys):
    row_spec = pl.BlockSpec((ROUTE_BLOCK, N_PICKS), lambda i: (i, 0))
    col_spec = pl.BlockSpec((N_PICKS, ROUTE_BLOCK), lambda i: (0, i))
    return pl.pallas_call(
        _route_kernel,
        grid=(N_TOK // ROUTE_BLOCK,),
        in_specs=[pl.BlockSpec((ROUTE_BLOCK, D_MODEL), lambda i: (i, 0)), _whole(keys.shape)],
        out_specs=[row_spec, row_spec, col_spec, col_spec,
                   pl.BlockSpec((1, ROUTE_BLOCK), lambda i: (0, i))],
        out_shape=[jax.ShapeDtypeStruct((N_TOK, N_PICKS), jnp.int32),
                   jax.ShapeDtypeStruct((N_TOK, N_PICKS), jnp.int32),
                   jax.ShapeDtypeStruct((N_PICKS, N_TOK), jnp.int32),
                   jax.ShapeDtypeStruct((N_PICKS, N_TOK), F32),
                   jax.ShapeDtypeStruct((1, N_TOK), jnp.int32)],
        scratch_shapes=[pltpu.VMEM((N_PICKS, ROUTE_BLOCK), jnp.int32),
                        pltpu.VMEM((N_PICKS, ROUTE_BLOCK), jnp.int32),
                        pltpu.VMEM((N_PICKS, ROUTE_BLOCK), F32)],
        compiler_params=pltpu.CompilerParams(vmem_limit_bytes=VMEM_LIMIT),
        name="route",
    )(q, keys)


N_GROUPS = N_PICKS // SUBLANES
WALK_GROUPS = 9


def _group_ranges(half, count0):
    if half == 0:
        return range(WALK_GROUPS), (WALK_GROUPS, (count0 + SUBLANES - 1) // SUBLANES)
    return range(N_GROUPS - WALK_GROUPS, N_GROUPS), (count0 // SUBLANES, N_GROUPS - WALK_GROUPS)


_MERGE_ORDER = (0, 4, 2, 6, 1, 5, 3, 7)


def _merge8(tiles, sub):
    m4 = sub < 4
    m2 = (sub & 3) < 2
    m1 = (sub & 1) == 0

    def l1(a, b):
        return jnp.where(m4, a, b) + pltpu.roll(jnp.where(m4, b, a), 4, 0)

    def l2(c, d):
        return jnp.where(m2, c, d) + jnp.where(m2, pltpu.roll(c, 6, 0), pltpu.roll(d, 2, 0))

    def l3(e, f):
        return jnp.where(m1, e, f) + jnp.where(m1, pltpu.roll(e, 7, 0), pltpu.roll(f, 1, 0))

    c = [l1(tiles[2 * i], tiles[2 * i + 1]) for i in range(4)]
    return l3(l2(c[0], c[1]), l2(c[2], c[3]))


def _peer_dots(half, cnt_ref, loc_ref, h_ref, tab_ref, actt_ref, gm):
    sub = lax.broadcasted_iota(jnp.int32, (SUBLANES, LANES), 0)
    lane = lax.broadcasted_iota(jnp.int32, (N_PICKS, PEER_BLOCK), 1)
    tok0 = pl.program_id(0) * PEER_BLOCK
    actt_ref[...] = jnp.zeros((N_PICKS, PEER_BLOCK), F32)
    for buf in gm:
        buf[...] = jnp.zeros((N_PICKS, LANES), F32)

    def group_sums(t, g):
        ht = h_ref[pl.ds(pl.multiple_of(t * SUBLANES, SUBLANES), SUBLANES), :]
        loc_row = loc_ref.at[t]
        prods = []
        for s in range(SUBLANES):
            r = pl.multiple_of(loc_row[g * SUBLANES + _MERGE_ORDER[s]], SUBLANES)
            prods.append(tab_ref[pl.ds(r, SUBLANES), :] * ht)
        return _merge8(prods, sub)

    def partial_sums(t, buf):
        for g in _group_ranges(half, 0)[0]:
            buf[g * SUBLANES:(g + 1) * SUBLANES, :] = group_sums(t, g)

    def extra_partial_sums(t, bufs):
        bounds = [_group_ranges(half, cnt_ref[tok0 + t + i])[1] for i in range(2)]

        @pl.when((bounds[0][1] > bounds[0][0]) | (bounds[1][1] > bounds[1][0]))
        def _():
            for i in range(2):
                def one(g, carry, i=i):
                    bufs[i][pl.ds(pl.multiple_of(g * SUBLANES, SUBLANES), SUBLANES), :] = group_sums(t + i, g)
                    return carry

                lax.fori_loop(*bounds[i], one, 0)

    def reduce_into_column(t, buf):
        col = jnp.sum(buf[...], axis=1, keepdims=True)
        actt_ref[...] = jnp.where(lane == t, col, actt_ref[...])

    def four_tokens(k, carry):
        t = 4 * k
        reduce_into_column(t - 2, gm[2])
        reduce_into_column(t - 1, gm[3])
        partial_sums(t, gm[0])
        partial_sums(t + 1, gm[1])
        extra_partial_sums(t, gm[0:2])
        reduce_into_column(t, gm[0])
        reduce_into_column(t + 1, gm[1])
        partial_sums(t + 2, gm[2])
        partial_sums(t + 3, gm[3])
        extra_partial_sums(t + 2, gm[2:4])
        return carry

    lax.fori_loop(0, PEER_BLOCK // 4, four_tokens, 0)
    reduce_into_column(PEER_BLOCK - 2, gm[2])
    reduce_into_column(PEER_BLOCK - 1, gm[3])


def _peer_u_first_kernel(layer, cnt_ref, loc_ref, h_ref, tab_hbm, actt_ref, tab_s, sem, *gm):
    _load_table_half(0, layer, tab_hbm, tab_s, sem)
    _peer_dots(0, cnt_ref, loc_ref, h_ref, tab_s, actt_ref, gm)


def _peer_u_second_kernel(layer, cnt_ref, loc_ref, h_ref, tab_hbm, act0t_ref, idxt_ref, gatet_ref,
                          w0_ref, w1_ref, tab_s, sem, actt_s, *gm):
    _load_table_half(1, layer, tab_hbm, tab_s, sem)
    _peer_dots(1, cnt_ref, loc_ref, h_ref, tab_s, actt_s, gm)
    in0 = _table_half(idxt_ref[...]) == 0
    w = gatet_ref[...] * _gelu(jnp.where(in0, act0t_ref[...], actt_s[...]))
    w0_ref[...] = jnp.where(in0, w, 0.0)
    w1_ref[...] = jnp.where(in0, 0.0, w)


def _peer_sum(half, cnt_ref, loc_ref, w_ref, tab_ref, init, f_ref, wb):
    tok0 = pl.program_id(0) * PEER_BLOCK

    def splat_weights(t, buf):
        buf[...] = jnp.take_along_axis(w_ref[...], jnp.full((N_PICKS, PEER_BLOCK), t, jnp.int32), axis=1)

    def add_group(t, g, accs, weight_row):
        loc_row = loc_ref.at[t]
        accs = list(accs)
        for s in range(SUBLANES):
            r = pl.multiple_of(loc_row[g * SUBLANES + s], SUBLANES)
            w = jnp.broadcast_to(weight_row(g * SUBLANES + s), (SUBLANES, LANES))
            accs[s % 4] = accs[s % 4] + tab_ref[pl.ds(r, SUBLANES), :] * w
        return tuple(accs)

    def token_rows(t):
        return pl.ds(pl.multiple_of(t * SUBLANES, SUBLANES), SUBLANES)

    def weighted_sum(t, buf):
        accs = (init(token_rows(t)),) + (jnp.zeros((SUBLANES, LANES), F32),) * 3
        for g in _group_ranges(half, 0)[0]:
            accs = add_group(t, g, accs, lambda e: buf[e:e + 1, :])
        return accs

    def total(accs):
        return (accs[0] + accs[1]) + (accs[2] + accs[3])

    def extra_weighted_sums(t, bufs):
        bounds = [_group_ranges(half, cnt_ref[tok0 + t + i])[1] for i in range(2)]

        @pl.when((bounds[0][1] > bounds[0][0]) | (bounds[1][1] > bounds[1][0]))
        def _():
            for i in range(2):
                accs = (f_ref[token_rows(t + i), :],) + (jnp.zeros((SUBLANES, LANES), F32),) * 3
                accs = lax.fori_loop(
                    *bounds[i],
                    lambda g, a, i=i: add_group(t + i, g, a, lambda e: bufs[i][pl.ds(e, 1), :]), accs)
                f_ref[token_rows(t + i), :] = total(accs)

    def token_pair(t, bufs, next_t, next_bufs):
        splat_weights(jnp.minimum(next_t, PEER_BLOCK - 1), next_bufs[0])
        splat_weights(jnp.minimum(next_t + 1, PEER_BLOCK - 1), next_bufs[1])
        f_ref[token_rows(t), :] = total(weighted_sum(t, bufs[0]))
        f_ref[token_rows(t + 1), :] = total(weighted_sum(t + 1, bufs[1]))
        extra_weighted_sums(t, bufs)

    splat_weights(0, wb[0])
    splat_weights(1, wb[1])

    def four_tokens(k, carry):
        t = 4 * k
        token_pair(t, wb[0:2], t + 2, wb[2:4])
        token_pair(t + 2, wb[2:4], t + 4, wb[0:2])
        return carry

    lax.fori_loop(0, PEER_BLOCK // 4, four_tokens, 0)


def _peer_v_first_kernel(layer, cnt_ref, loc_ref, w_ref, tab_hbm, f_ref, tab_s, sem, *wb):
    _load_table_half(0, layer, tab_hbm, tab_s, sem)
    _peer_sum(0, cnt_ref, loc_ref, w_ref, tab_s, lambda rows: jnp.zeros((SUBLANES, LANES), F32), f_ref, wb)


def _peer_v_second_kernel(layer, cnt_ref, loc_ref, w_ref, tab_hbm, f0_ref, f_ref, tab_s, sem, *wb):
    _load_table_half(1, layer, tab_hbm, tab_s, sem)
    _peer_sum(1, cnt_ref, loc_ref, w_ref, tab_s, lambda rows: f0_ref[rows, :], f_ref, wb)


def _peer(h2, loc0, loc1, idx_t, gate_t, count0, u_tab, v_tab, layer):
    nblk = N_TOK // PEER_BLOCK
    h_tiles = h2.reshape(N_TOK * SUBLANES, LANES)
    cnt_spec = pl.BlockSpec(memory_space=pltpu.SMEM)
    smem_spec = pl.BlockSpec((PEER_BLOCK, N_PICKS), lambda i: (i, 0), memory_space=pltpu.SMEM)
    row_spec = pl.BlockSpec((PEER_BLOCK, N_PICKS), lambda i: (i, 0))
    col_spec = pl.BlockSpec((N_PICKS, PEER_BLOCK), lambda i: (0, i))
    tile_spec = pl.BlockSpec((PEER_BLOCK * SUBLANES, LANES), lambda i: (i, 0))
    hbm_spec = pl.BlockSpec(memory_space=pl.ANY)
    row_shape = jax.ShapeDtypeStruct((N_TOK, N_PICKS), F32)
    col_shape = jax.ShapeDtypeStruct((N_PICKS, N_TOK), F32)
    tile_shape = jax.ShapeDtypeStruct((N_TOK * SUBLANES, LANES), F32)
    params = pltpu.CompilerParams(vmem_limit_bytes=VMEM_LIMIT, dimension_semantics=("arbitrary",))
    table_scratch = [pltpu.VMEM((HALF_EXPERTS * SUBLANES, LANES), F32), pltpu.SemaphoreType.DMA]
    pair_scratch = [pltpu.VMEM((N_PICKS, LANES), F32)] * 4

    act0_t = pl.pallas_call(
        functools.partial(_peer_u_first_kernel, layer), grid=(nblk,),
        in_specs=[cnt_spec, smem_spec, tile_spec, hbm_spec],
        out_specs=col_spec, out_shape=col_shape, scratch_shapes=table_scratch + pair_scratch,
        compiler_params=params, name="peer_u0",
    )(count0, loc0, h_tiles, u_tab)
    w0, w1 = pl.pallas_call(
        functools.partial(_peer_u_second_kernel, layer), grid=(nblk,),
        in_specs=[cnt_spec, smem_spec, tile_spec, hbm_spec, col_spec, col_spec, col_spec],
        out_specs=[col_spec, col_spec], out_shape=[col_shape, col_shape],
        scratch_shapes=table_scratch + [pltpu.VMEM((N_PICKS, PEER_BLOCK), F32)] + pair_scratch,
        compiler_params=params, name="peer_u1",
    )(count0, loc1, h_tiles, u_tab, act0_t, idx_t, gate_t)
    f0 = pl.pallas_call(
        functools.partial(_peer_v_first_kernel, layer), grid=(nblk,),
        in_specs=[cnt_spec, smem_spec, col_spec, hbm_spec],
        out_specs=tile_spec, out_shape=tile_shape, scratch_shapes=table_scratch + pair_scratch,
        compiler_params=params, name="peer_v0",
    )(count0, loc0, w0, v_tab)
    f = pl.pallas_call(
        functools.partial(_peer_v_second_kernel, layer), grid=(nblk,),
        in_specs=[cnt_spec, smem_spec, col_spec, hbm_spec, tile_spec],
        out_specs=tile_spec, out_shape=tile_shape, scratch_shapes=table_scratch + pair_scratch,
        compiler_params=params, name="peer_v1",
    )(count0, loc1, w1, v_tab, f0)
    return f.reshape(N_TOK, D_MODEL)


def _close_kernel(x1_ref, f_ref, mod_ref, g_ref, b_ref, o_ref):
    g2 = mod_ref[...][:, 5 * D_MODEL:6 * D_MODEL]
    o_ref[...] = _ln(ALPHA * x1_ref[...] + g2 * f_ref[...]) * g_ref[1:2, :] + b_ref[1:2, :]


def _close(x1, f, mods, ln_g, ln_b, layer):
    row_spec = pl.BlockSpec((ROW_BLOCK, D_MODEL), lambda i: (i, 0))
    vec_spec = pl.BlockSpec((None, 2, D_MODEL), lambda i: (layer, 0, 0))
    return pl.pallas_call(
        _close_kernel,
        grid=(N_ROW_BLOCKS,),
        in_specs=[row_spec, row_spec,
                  pl.BlockSpec((None, None, 1, 6 * D_MODEL), lambda i: (layer, _mod_group(i), 0, 0)),
                  vec_spec, vec_spec],
        out_specs=row_spec,
        out_shape=jax.ShapeDtypeStruct((N_TOK, D_MODEL), F32),
        compiler_params=pltpu.CompilerParams(vmem_limit_bytes=VMEM_LIMIT),
        name="close",
    )(x1, f, mods, ln_g, ln_b)


def _rope_tables(dim, copies):
    rows = LAT_LEN // GRID_W
    row = jnp.repeat(jnp.arange(rows, dtype=F32), GRID_W)
    col = jnp.tile(jnp.arange(GRID_W, dtype=F32), rows)
    nf = dim // 4
    inv = ROPE_BASE ** (-jnp.arange(nf, dtype=F32) / nf)
    ar = row[:, None] * inv
    ac = col[:, None] * inv
    ang = jnp.concatenate([ar, ar, ac, ac], axis=-1)
    cos, sin = jnp.cos(ang), jnp.sin(ang)
    even_quarter = (jnp.arange(dim) // nf) % 2 == 0
    sin_up = jnp.where(even_quarter, -sin, 0.0)
    sin_dn = jnp.where(even_quarter, 0.0, sin)
    return tuple(jnp.tile(t, (1, copies)) for t in (cos, sin_up, sin_dn))


def kernel(x_prompt, x_sample, c, cache_win_k, cache_win_v, cache_diff_k, cache_diff_v, c_ctx, w_mod, b_mod, w_in, w_out, chunk_w, chunk_b, win_sink, diff_lam_q, diff_lam_k, diff_subln_g, pool_w, pool_scale, ln_g, ln_b, peer_wq, peer_keys, peer_u, peer_v):
    cvec = jnp.concatenate([c_ctx[None, :], c, jnp.zeros((8 - 1 - N_LAT_SEQ, D_MODEL), F32)], axis=0)
    mods = _modulation(cvec, w_mod, b_mod)[:, :1 + N_LAT_SEQ].reshape(DEPTH, 1 + N_LAT_SEQ, 1, 6 * D_MODEL)

    w_in_b = w_in.astype(BF16)
    w_out_b = w_out.astype(BF16)
    wq_b = peer_wq.astype(BF16)
    cw_b = chunk_w.astype(BF16)
    cb_full = jnp.repeat(jnp.swapaxes(chunk_b, 1, 2), HEAD_D, axis=2)
    eye = jnp.eye(4, dtype=F32)
    wbd = jnp.einsum('lgcd,gh->lgchd', pool_w, eye).reshape(DEPTH, GROUP_W, GROUP_W).astype(BF16)
    caches = (cache_win_k.reshape(N_LAT_SEQ, DEPTH, CTX_LEN, 128),
              cache_win_v.reshape(N_LAT_SEQ, DEPTH, CTX_LEN, 128),
              cache_diff_k.reshape(N_LAT_SEQ, DEPTH, CTX_LEN, GROUP_W),
              cache_diff_v.reshape(N_LAT_SEQ, DEPTH, CTX_LEN, GROUP_W))
    rope_tabs = _rope_tables(HEAD_D, GROUP_W // HEAD_D) + _rope_tables(C_SUB, GROUP_W // C_SUB)
    u_tiles = peer_u.reshape(DEPTH, N_EXPERTS * SUBLANES, LANES)
    v_tiles = peer_v.reshape(DEPTH, N_EXPERTS * SUBLANES, LANES)

    x = jnp.concatenate([x_prompt.reshape(N_CTX_TOK, D_MODEL), x_sample.reshape(N_LAT_TOK, D_MODEL)], axis=0)
    kbs, vbs, kcs, vcs = [], [], [], []
    for l in range(DEPTH):
        lam_init = 0.8 - 0.6 * math.exp(-0.3 * l)
        layer_w = (cw_b[l], cb_full[l], win_sink[l], diff_lam_q[l], diff_lam_k[l],
                   diff_subln_g[l].reshape(1, HEAD_D), wbd[l], pool_scale[l].reshape(1, GROUP_W))
        p = _in_proj(x, mods, w_in_b, l)
        pc = p[:N_CTX_TOK].reshape(N_CTX_SEQ, CTX_LEN, P_COLS)
        kbs.append(pc[..., COL_BK:COL_BV].reshape(N_CTX_SEQ, CTX_LEN, 2, HEAD_D))
        vbs.append(pc[..., COL_BV:COL_CQ].reshape(N_CTX_SEQ, CTX_LEN, 2, HEAD_D))
        kcs.append(pc[..., COL_CK:COL_CV].reshape(N_CTX_SEQ, CTX_LEN, 4, 2, C_SUB))
        vcs.append(pc[..., COL_CV:COL_DZ].reshape(N_CTX_SEQ, CTX_LEN, 4, HEAD_D))
        o = jnp.concatenate([_ctx_mix(p, layer_w, lam_init),
                             _lat_mix(p, caches, rope_tabs, layer_w, lam_init, l)], axis=0)
        x1, h2, q = _out_proj(o, x, mods, w_out_b, ln_g, ln_b, wq_b, l)
        loc0, loc1, idx_t, gate_t, count0 = _route(q, peer_keys[l])
        f = _peer(h2, loc0, loc1, idx_t, gate_t, count0.reshape(N_TOK), u_tiles, v_tiles, l)
        x = _close(x1, f, mods, ln_g, ln_b, l)
    return (x[:N_CTX_TOK].reshape(N_CTX_SEQ, CTX_LEN, D_MODEL),
            x[N_CTX_TOK:].reshape(N_LAT_SEQ, LAT_LEN, D_MODEL),
            jnp.stack(kbs, axis=1), jnp.stack(vbs, axis=1), jnp.stack(kcs, axis=1), jnp.stack(vcs, axis=1))
```

```python
import functools
import math

import jax
import jax.numpy as jnp
import numpy as np
from jax import lax
from jax.experimental import pallas as pl
from jax.experimental.pallas import tpu as pltpu

F32 = jnp.float32
BF16 = jnp.bfloat16
HIGHEST = lax.Precision.HIGHEST

D_MODEL = 1024
DEPTH = 4
N_CTX_SEQ = 16
CTX_LEN = 256
N_LAT_SEQ = 2
LAT_LEN = 1024
N_CTX_TOK = N_CTX_SEQ * CTX_LEN
N_LAT_TOK = N_LAT_SEQ * LAT_LEN
N_TOK = N_CTX_TOK + N_LAT_TOK
GRID_W = 64
BLOCK = 128
GROUP_W = 256
HEAD_D = 64
C_SUB = 32
WINDOW = 128
POOL_SIZES = (2, 4, 8, 16)
P_COLS = 2048
COL_AU, COL_AV, COL_BQ, COL_BK, COL_BV, COL_CQ, COL_CK, COL_CV, COL_DZ = (
    0, 256, 512, 768, 896, 1024, 1280, 1536, 1792)
PK_HEADS = 8
N_KEYS = 128
N_EXPERTS = N_KEYS * N_KEYS
PK_TOPK = 16
N_PICKS = PK_HEADS * PK_TOPK
ROPE_BASE = 10000.0
ALPHA = (2 * DEPTH) ** 0.25
LN_EPS = 1e-5
NEG = -1e30

ROW_BLOCK = 512
N_ROW_BLOCKS = N_TOK // ROW_BLOCK
N_CTX_ROW_BLOCKS = N_CTX_TOK // ROW_BLOCK
ROW_BLOCKS_PER_LAT_SEQ = LAT_LEN // ROW_BLOCK
ROUTE_BLOCK = 128
PEER_BLOCK = 128
SUBLANES = 8
LANES = 128
HALF_EXPERTS = N_EXPERTS // 2
VMEM_LIMIT = 56 * 1024 * 1024


def _mm(a, b):
    return jnp.dot(a.astype(BF16), b.astype(BF16), preferred_element_type=F32)


def _mm_nt(a, b):
    return lax.dot_general(a.astype(BF16), b.astype(BF16), (((1,), (1,)), ((), ())),
                           preferred_element_type=F32)


def _gelu(x):
    return x * (0.5 * (1.0 + jnp.tanh(0.7978845608028654 * (x + 0.044715 * (x * x * x)))))


def _ln(x):
    mu = jnp.mean(x, axis=-1, keepdims=True)
    xc = x - mu
    var = jnp.mean(xc * xc, axis=-1, keepdims=True)
    return xc * lax.rsqrt(var + LN_EPS)


def _mod_group(i):
    return jnp.where(i < N_CTX_ROW_BLOCKS, 0, 1 + (i - N_CTX_ROW_BLOCKS) // ROW_BLOCKS_PER_LAT_SEQ)


def _mod_kernel(c_ref, w_ref, b_ref, o_ref):
    c = c_ref[...]
    s = c * jax.nn.sigmoid(c)
    o_ref[...] = jnp.dot(s, w_ref[...], precision=HIGHEST, preferred_element_type=F32) + b_ref[...]


def _modulation(cvec, w_mod, b_mod):
    nj = 4
    cols = 6 * D_MODEL // nj
    return pl.pallas_call(
        _mod_kernel,
        grid=(DEPTH, nj),
        in_specs=[pl.BlockSpec((8, D_MODEL), lambda l, j: (0, 0)),
                  pl.BlockSpec((None, D_MODEL, cols), lambda l, j: (l, 0, j)),
                  pl.BlockSpec((None, 1, cols), lambda l, j: (l, 0, j))],
        out_specs=pl.BlockSpec((None, 8, cols), lambda l, j: (l, 0, j)),
        out_shape=jax.ShapeDtypeStruct((DEPTH, 8, 6 * D_MODEL), F32),
        compiler_params=pltpu.CompilerParams(vmem_limit_bytes=VMEM_LIMIT),
        name="modulation",
    )(cvec, w_mod, b_mod.reshape(DEPTH, 1, 6 * D_MODEL))


def _in_proj_kernel(x_ref, mod_ref, w_ref, o_ref):
    m = mod_ref[...]
    h = _ln(x_ref[...]) * (1.0 + m[:, D_MODEL:2 * D_MODEL]) + m[:, 0:D_MODEL]
    o_ref[...] = jnp.dot(h.astype(BF16), w_ref[...], preferred_element_type=F32)


def _in_proj(x, mods, w_in_bf16, layer):
    return pl.pallas_call(
        _in_proj_kernel,
        grid=(N_ROW_BLOCKS,),
        in_specs=[pl.BlockSpec((ROW_BLOCK, D_MODEL), lambda i: (i, 0)),
                  pl.BlockSpec((None, None, 1, 6 * D_MODEL), lambda i: (layer, _mod_group(i), 0, 0)),
                  pl.BlockSpec((None, D_MODEL, P_COLS), lambda i: (layer, 0, 0))],
        out_specs=pl.BlockSpec((ROW_BLOCK, P_COLS), lambda i: (i, 0)),
        out_shape=jax.ShapeDtypeStruct((N_TOK, P_COLS), F32),
        compiler_params=pltpu.CompilerParams(vmem_limit_bytes=VMEM_LIMIT),
        name="in_proj",
    )(x, mods, w_in_bf16)


def _chunk_gating(u_raw, v_raw, cw_ref, cb_full):
    u = _gelu(u_raw)
    v = _gelu(v_raw).astype(BF16)
    head = lax.broadcasted_iota(jnp.int32, (BLOCK, GROUP_W), 1) // HEAD_D
    mixed = cb_full
    for h in range(4):
        mixed = mixed + jnp.where(head == h, jnp.dot(cw_ref[h], v, preferred_element_type=F32), 0.0)
    return u * mixed


def _pool(z_rows, z_win, row0, key0, seq_len, wbd_ref, pool_scale):
    r, k = z_rows.shape[0], z_win.shape[0]
    t = row0 + lax.broadcasted_iota(jnp.int32, (r, k), 0)
    pos = key0 + lax.broadcasted_iota(jnp.int32, (r, k), 1)
    t1 = row0 + lax.broadcasted_iota(jnp.int32, (r, 1), 0)
    z_hi = z_win.astype(BF16)
    z_lo = (z_win - z_hi.astype(F32)).astype(BF16)
    group = lax.broadcasted_iota(jnp.int32, (r, GROUP_W), 1) // HEAD_D
    mean = jnp.zeros((r, GROUP_W), F32)
    for g, w in enumerate(POOL_SIZES):
        lo = jnp.maximum(t - w // 2, 0)
        hi = jnp.minimum(t + w // 2, seq_len)
        band = jnp.where((pos >= lo) & (pos < hi), 1.0, 0.0).astype(BF16)
        tot = (jnp.dot(band, z_hi, preferred_element_type=F32)
               + jnp.dot(band, z_lo, preferred_element_type=F32))
        cnt = (jnp.minimum(t1 + w // 2, seq_len) - jnp.maximum(t1 - w // 2, 0)).astype(F32)
        mean = mean + jnp.where(group == g, tot / cnt, 0.0)
    pooled = mean - z_rows
    return jnp.dot(pooled.astype(BF16), wbd_ref[...], preferred_element_type=F32) * pool_scale


def _diff_lambda(lq_ref, lk_ref, lam_init):
    lq = lq_ref[...]
    lk = lk_ref[...]
    s0 = jnp.sum(lq[0:1] * lk[0:1], axis=-1, keepdims=True)
    s1 = jnp.sum(lq[1:2] * lk[1:2], axis=-1, keepdims=True)
    return jnp.exp(s0) - jnp.exp(s1) + lam_init


def _sub_norm(o, g, lam_init):
    ms = jnp.mean(o * o, axis=-1, keepdims=True)
    return o * lax.rsqrt(ms + LN_EPS) * g * (1.0 - lam_init)


def _ctx_mix_kernel(lam_init, p_ref, cw_ref, cb_ref, sink_ref, lq_ref, lk_ref, sg_ref, wbd_ref,
                    ps_ref, o_ref):
    s_len = CTX_LEN
    for n in range(s_len // BLOCK):
        rows = slice(n * BLOCK, (n + 1) * BLOCK)
        o_ref[rows, 0:GROUP_W] = _chunk_gating(p_ref[rows, COL_AU:COL_AU + GROUP_W],
                                               p_ref[rows, COL_AV:COL_AV + GROUP_W], cw_ref, cb_ref[...])
    for kv in range(2):
        kk = p_ref[:, COL_BK + kv * HEAD_D:COL_BK + (kv + 1) * HEAD_D].astype(BF16)
        vv = p_ref[:, COL_BV + kv * HEAD_D:COL_BV + (kv + 1) * HEAD_D].astype(BF16)
        for g in range(2):
            h = kv * 2 + g
            q = p_ref[:, COL_BQ + h * HEAD_D:COL_BQ + (h + 1) * HEAD_D] * (HEAD_D ** -0.5)
            s = _mm_nt(q, kk)
            sk = sink_ref[h]
            m = jnp.maximum(jnp.max(s, axis=-1, keepdims=True), sk)
            e = jnp.exp(s - m)
            den = jnp.sum(e, axis=-1, keepdims=True) + jnp.exp(sk - m)
            o_ref[:, GROUP_W + h * HEAD_D:GROUP_W + (h + 1) * HEAD_D] = _mm(e, vv) / den
    lam = _diff_lambda(lq_ref, lk_ref, lam_init)
    scale = C_SUB ** -0.5
    for h in range(4):
        vv = p_ref[:, COL_CV + h * HEAD_D:COL_CV + (h + 1) * HEAD_D].astype(BF16)
        parts = []
        for j in range(2):
            c0 = h * HEAD_D + j * C_SUB
            s = _mm_nt(p_ref[:, COL_CQ + c0:COL_CQ + c0 + C_SUB],
                       p_ref[:, COL_CK + c0:COL_CK + c0 + C_SUB]) * scale
            e = jnp.exp(s - jnp.max(s, axis=-1, keepdims=True))
            parts.append(_mm(e, vv) / jnp.sum(e, axis=-1, keepdims=True))
        o = parts[0] - lam * parts[1]
        o_ref[:, 2 * GROUP_W + h * HEAD_D:2 * GROUP_W + (h + 1) * HEAD_D] = _sub_norm(o, sg_ref[...], lam_init)
    z = p_ref[:, COL_DZ:COL_DZ + GROUP_W]
    o_ref[:, 3 * GROUP_W:4 * GROUP_W] = _pool(z, z, 0, 0, s_len, wbd_ref, ps_ref[...])


def _whole(shape):
    nd = len(shape)
    return pl.BlockSpec(shape, lambda *_: (0,) * nd)


def _ctx_mix(p, layer_w, lam_init):
    cw, cb_full, sink, lq, lk, sg, wbd, ps = layer_w
    return pl.pallas_call(
        functools.partial(_ctx_mix_kernel, lam_init),
        grid=(N_CTX_SEQ,),
        in_specs=[pl.BlockSpec((CTX_LEN, P_COLS), lambda b: (b, 0)),
                  _whole(cw.shape), _whole(cb_full.shape),
                  pl.BlockSpec(memory_space=pltpu.SMEM),
                  _whole(lq.shape), _whole(lk.shape), _whole(sg.shape), _whole(wbd.shape),
                  _whole(ps.shape)],
        out_specs=pl.BlockSpec((CTX_LEN, D_MODEL), lambda b: (b, 0)),
        out_shape=jax.ShapeDtypeStruct((N_CTX_TOK, D_MODEL), F32),
        compiler_params=pltpu.CompilerParams(vmem_limit_bytes=VMEM_LIMIT),
        name="ctx_mix",
    )(p, cw, cb_full, sink, lq, lk, sg, wbd, ps)


def _rope(x, cos, sin_up, sin_dn, shift):
    outs = []
    for c0 in range(0, x.shape[1], LANES):
        xs = x[:, c0:c0 + LANES]
        up = pltpu.roll(xs, LANES - shift, 1)
        dn = pltpu.roll(xs, shift, 1)
        outs.append(xs * cos[:, c0:c0 + LANES] + up * sin_up[:, c0:c0 + LANES]
                    + dn * sin_dn[:, c0:c0 + LANES])
    return outs[0] if len(outs) == 1 else jnp.concatenate(outs, axis=1)


def _lat_mix_kernel(lam_init, p_ref, wk_ref, wv_ref, dk_ref, dv_ref,
                    cosb_ref, sbu_ref, sbd_ref, cosc_ref, scu_ref, scd_ref,
                    cw_ref, cb_ref, sink_ref, lq_ref, lk_ref, sg_ref, wbd_ref, ps_ref,
                    o_ref, kb_s, kc_s):
    n = pl.program_id(1)
    s_len = LAT_LEN

    @pl.when(n == 0)
    def _():
        kb_s[...] = _rope(p_ref[:, COL_BK:COL_BK + 128], cosb_ref[:, 0:128], sbu_ref[:, 0:128],
                          sbd_ref[:, 0:128], HEAD_D // 4).astype(BF16)
        kc_s[...] = _rope(p_ref[:, COL_CK:COL_CK + GROUP_W], cosc_ref[...], scu_ref[...],
                          scd_ref[...], C_SUB // 4).astype(BF16)

    r0 = pl.multiple_of(n * BLOCK, BLOCK)
    rows = pl.ds(r0, BLOCK)
    o_ref[:, 0:GROUP_W] = _chunk_gating(p_ref[rows, COL_AU:COL_AU + GROUP_W],
                                        p_ref[rows, COL_AV:COL_AV + GROUP_W], cw_ref, cb_ref[...])
    w0 = pl.multiple_of(jnp.clip((n - 1) * BLOCK, 0, s_len - 3 * BLOCK), BLOCK)
    win = pl.ds(w0, 3 * BLOCK)
    qb = _rope(p_ref[rows, COL_BQ:COL_BQ + GROUP_W], cosb_ref[rows, :], sbu_ref[rows, :],
               sbd_ref[rows, :], HEAD_D // 4) * (HEAD_D ** -0.5)
    qpos = r0 + lax.broadcasted_iota(jnp.int32, (BLOCK, 3 * BLOCK), 0)
    kpos = w0 + lax.broadcasted_iota(jnp.int32, (BLOCK, 3 * BLOCK), 1)
    valid = jnp.abs(kpos - qpos) <= WINDOW
    for kv in range(2):
        cs = slice(kv * HEAD_D, (kv + 1) * HEAD_D)
        k_loc = kb_s[win, cs]
        v_loc = p_ref[win, COL_BV + kv * HEAD_D:COL_BV + (kv + 1) * HEAD_D].astype(BF16)
        k_ctx = wk_ref[:, cs].astype(BF16)
        v_ctx = wv_ref[:, cs].astype(BF16)
        for g in range(2):
            h = kv * 2 + g
            q = qb[:, h * HEAD_D:(h + 1) * HEAD_D]
            s_loc = jnp.where(valid, _mm_nt(q, k_loc), NEG)
            s_ctx = _mm_nt(q, k_ctx)
            sk = sink_ref[h]
            m = jnp.maximum(jnp.maximum(jnp.max(s_loc, axis=-1, keepdims=True),
                                        jnp.max(s_ctx, axis=-1, keepdims=True)), sk)
            e_loc = jnp.exp(s_loc - m)
            e_ctx = jnp.exp(s_ctx - m)
            den = (jnp.sum(e_loc, axis=-1, keepdims=True) + jnp.sum(e_ctx, axis=-1, keepdims=True)
                   + jnp.exp(sk - m))
            o_ref[:, GROUP_W + h * HEAD_D:GROUP_W + (h + 1) * HEAD_D] = (
                _mm(e_loc, v_loc) + _mm(e_ctx, v_ctx)) / den
    lam = _diff_lambda(lq_ref, lk_ref, lam_init)
    scale = C_SUB ** -0.5
    qc = _rope(p_ref[rows, COL_CQ:COL_CQ + GROUP_W], cosc_ref[rows, :], scu_ref[rows, :],
               scd_ref[rows, :], C_SUB // 4)
    for h in range(4):
        v_ctx = dv_ref[:, h * HEAD_D:(h + 1) * HEAD_D].astype(BF16)
        v_lat = p_ref[:, COL_CV + h * HEAD_D:COL_CV + (h + 1) * HEAD_D].astype(BF16)
        parts = []
        for j in range(2):
            cs = slice(h * HEAD_D + j * C_SUB, h * HEAD_D + (j + 1) * C_SUB)
            q = qc[:, cs]
            s_ctx = _mm_nt(q, dk_ref[:, cs]) * scale
            s_lat = _mm_nt(q, kc_s[:, cs]) * scale
            m = jnp.maximum(jnp.max(s_ctx, axis=-1, keepdims=True), jnp.max(s_lat, axis=-1, keepdims=True))
            e_ctx = jnp.exp(s_ctx - m)
            e_lat = jnp.exp(s_lat - m)
            den = jnp.sum(e_ctx, axis=-1, keepdims=True) + jnp.sum(e_lat, axis=-1, keepdims=True)
            parts.append((_mm(e_ctx, v_ctx) + _mm(e_lat, v_lat)) / den)
        o = parts[0] - lam * parts[1]
        o_ref[:, 2 * GROUP_W + h * HEAD_D:2 * GROUP_W + (h + 1) * HEAD_D] = _sub_norm(o, sg_ref[...], lam_init)
    o_ref[:, 3 * GROUP_W:4 * GROUP_W] = _pool(p_ref[rows, COL_DZ:COL_DZ + GROUP_W],
                                              p_ref[win, COL_DZ:COL_DZ + GROUP_W],
                                              r0, w0, s_len, wbd_ref, ps_ref[...])


def _lat_mix(p, caches, rope_tabs, layer_w, lam_init, layer):
    cw, cb_full, sink, lq, lk, sg, wbd, ps = layer_w
    nb = LAT_LEN // BLOCK
    ctx_blocks = N_CTX_TOK // LAT_LEN
    cache_spec = lambda width: pl.BlockSpec((None, None, CTX_LEN, width), lambda b, n: (b, layer, 0, 0))
    rope_spec = pl.BlockSpec((LAT_LEN, GROUP_W), lambda b, n: (0, 0))
    return pl.pallas_call(
        functools.partial(_lat_mix_kernel, lam_init),
        grid=(N_LAT_SEQ, nb),
        in_specs=[pl.BlockSpec((LAT_LEN, P_COLS), lambda b, n: (ctx_blocks + b, 0)),
                  cache_spec(128), cache_spec(128), cache_spec(256), cache_spec(256)]
                 + [rope_spec] * 6
                 + [_whole(cw.shape), _whole(cb_full.shape), pl.BlockSpec(memory_space=pltpu.SMEM),
                    _whole(lq.shape), _whole(lk.shape), _whole(sg.shape), _whole(wbd.shape),
                    _whole(ps.shape)],
        out_specs=pl.BlockSpec((BLOCK, D_MODEL), lambda b, n: (b * nb + n, 0)),
        out_shape=jax.ShapeDtypeStruct((N_LAT_TOK, D_MODEL), F32),
        scratch_shapes=[pltpu.VMEM((LAT_LEN, 128), BF16), pltpu.VMEM((LAT_LEN, GROUP_W), BF16)],
        compiler_params=pltpu.CompilerParams(vmem_limit_bytes=VMEM_LIMIT,
                                             dimension_semantics=("arbitrary", "arbitrary")),
        name="lat_mix",
    )(p, *caches, *rope_tabs, cw, cb_full, sink, lq, lk, sg, wbd, ps)


def _out_proj_kernel(o_ref, x_ref, mod_ref, wo_ref, g_ref, b_ref, wq_ref, x1_ref, h2_ref, q_ref):
    m = mod_ref[...]
    g1 = m[:, 2 * D_MODEL:3 * D_MODEL]
    sh2 = m[:, 3 * D_MODEL:4 * D_MODEL]
    sc2 = m[:, 4 * D_MODEL:5 * D_MODEL]
    y = jnp.dot(o_ref[...].astype(BF16), wo_ref[...], preferred_element_type=F32)
    x1 = _ln(ALPHA * x_ref[...] + g1 * y) * g_ref[0:1, :] + b_ref[0:1, :]
    x1_ref[...] = x1
    h2 = _ln(x1) * (1.0 + sc2) + sh2
    h2_ref[...] = h2
    q_ref[...] = jnp.dot(h2.astype(BF16), wq_ref[...], preferred_element_type=F32)


def _out_proj(o, x, mods, w_out_bf16, ln_g, ln_b, wq_bf16, layer):
    row_spec = pl.BlockSpec((ROW_BLOCK, D_MODEL), lambda i: (i, 0))
    mat_spec = pl.BlockSpec((None, D_MODEL, D_MODEL), lambda i: (layer, 0, 0))
    vec_spec = pl.BlockSpec((None, 2, D_MODEL), lambda i: (layer, 0, 0))
    shape = jax.ShapeDtypeStruct((N_TOK, D_MODEL), F32)
    return pl.pallas_call(
        _out_proj_kernel,
        grid=(N_ROW_BLOCKS,),
        in_specs=[row_spec, row_spec,
                  pl.BlockSpec((None, None, 1, 6 * D_MODEL), lambda i: (layer, _mod_group(i), 0, 0)),
                  mat_spec, vec_spec, vec_spec, mat_spec],
        out_specs=[row_spec, row_spec, row_spec],
        out_shape=[shape, shape, shape],
        compiler_params=pltpu.CompilerParams(vmem_limit_bytes=VMEM_LIMIT),
        name="out_proj",
    )(o, x, mods, w_out_bf16, ln_g, ln_b, wq_bf16)


def _top16_rows(sc, order):
    out_row = lax.broadcasted_iota(jnp.int32, (PK_TOPK, sc.shape[1]), 0)
    vals = jnp.zeros((PK_TOPK, sc.shape[1]), F32)
    sel = jnp.zeros((PK_TOPK, sc.shape[1]), jnp.int32)
    for i in range(PK_TOPK):
        m = jnp.max(sc, axis=0, keepdims=True)
        am = jnp.min(jnp.where(sc == m, order, jnp.int32(2 ** 30)), axis=0, keepdims=True)
        vals = jnp.where(out_row == i, m, vals)
        sel = jnp.where(out_row == i, am, sel)
        sc = jnp.where(order == am, NEG, sc)
    return vals, sel


def _pair_candidates(sv, si):
    n_tok = sv[0].shape[1]
    rows = lax.broadcasted_iota(jnp.int32, (SUBLANES, n_tok), 0)
    cands, flats, experts = [], [], []

    def piece(a, b0, valid):
        v = sv[0][a:a + 1, :] + sv[1][b0:b0 + SUBLANES, :]
        cands.append(v if valid >= SUBLANES else jnp.where(rows < valid, v, NEG))
        flats.append(a * PK_TOPK + b0 + rows)
        experts.append(si[0][a:a + 1, :] * N_KEYS + si[1][b0:b0 + SUBLANES, :])

    piece(0, 0, SUBLANES)
    piece(0, SUBLANES, SUBLANES)
    for a in range(1, SUBLANES):
        piece(a, 0, PK_TOPK // (a + 1))
    cands.append(sv[0][SUBLANES:, :] + sv[1][0:1, :])
    flats.append((SUBLANES + rows) * PK_TOPK)
    experts.append(si[0][SUBLANES:, :] * N_KEYS + si[1][0:1, :])
    order = jnp.concatenate(flats, axis=0) * N_EXPERTS + jnp.concatenate(experts, axis=0)
    return jnp.concatenate(cands, axis=0), order


RUN_EXPERTS = N_KEYS // 2
RUNS_PER_PARITY = N_KEYS // 2


KEY_BITS = N_KEYS.bit_length() - 1


def _table_half(idx):
    return ((idx >> KEY_BITS) ^ (idx >> (KEY_BITS - 1))) & 1


def _half_offsets(idx, half):
    i1 = idx >> KEY_BITS
    local = ((i1 & 1) * RUNS_PER_PARITY + (i1 >> 1)) * RUN_EXPERTS + (idx & (RUN_EXPERTS - 1))
    return jnp.where(_table_half(idx) == half, local, 0) * SUBLANES


def _load_table_half(half, layer, tab_hbm, tab_s, sem):
    @pl.when(pl.program_id(0) == 0)
    def _():
        run_rows = RUN_EXPERTS * SUBLANES
        copies = []
        for parity in range(2):
            for pair in range(RUNS_PER_PARITY):
                first = ((2 * pair + parity) * N_KEYS + (half ^ parity) * RUN_EXPERTS) * SUBLANES
                slot = (parity * RUNS_PER_PARITY + pair) * run_rows
                copies.append(pltpu.make_async_copy(tab_hbm.at[layer, pl.ds(first, run_rows), :],
                                                    tab_s.at[pl.ds(slot, run_rows), :], sem))
        for copy in copies:
            copy.start()
        for copy in copies:
            copy.wait()


def _route_kernel(q_ref, keys_ref, loc0_ref, loc1_ref, idxt_ref, gatet_ref, count0_ref,
                  slot_s, idx_s, gate_s):
    n_tok = q_ref.shape[0]
    key_row = lax.broadcasted_iota(jnp.int32, (N_KEYS, n_tok), 0)
    for h in range(PK_HEADS):
        sv = []
        si = []
        for j in range(2):
            c0 = (h * 2 + j) * HEAD_D
            sc = lax.dot_general(keys_ref[j], q_ref[:, c0:c0 + HEAD_D], (((1,), (1,)), ((), ())),
                                 precision=HIGHEST, preferred_element_type=F32)
            v, i = _top16_rows(sc, key_row)
            sv.append(v)
            si.append(i)
        cv, winners = _top16_rows(*_pair_candidates(sv, si))
        ce = winners & (N_EXPERTS - 1)
        e = jnp.exp(cv - jnp.max(cv, axis=0, keepdims=True))
        gate_s[h * PK_TOPK:(h + 1) * PK_TOPK, :] = e / jnp.sum(e, axis=0, keepdims=True)
        idx_s[h * PK_TOPK:(h + 1) * PK_TOPK, :] = ce
    in0 = _table_half(idx_s[...]) == 0
    before = (lax.broadcasted_iota(jnp.int32, (N_PICKS, N_PICKS), 1)
              < lax.broadcasted_iota(jnp.int32, (N_PICKS, N_PICKS), 0))
    rank0 = jnp.dot(jnp.where(before, 1.0, 0.0).astype(BF16), jnp.where(in0, 1.0, 0.0).astype(BF16),
                    preferred_element_type=F32).astype(jnp.int32)
    count0 = jnp.sum(jnp.where(in0, 1, 0), axis=0, keepdims=True)
    pick = lax.broadcasted_iota(jnp.int32, (N_PICKS, n_tok), 0)
    slot_s[...] = jnp.where(in0, rank0, count0 + pick - rank0)
    idx_sorted = jnp.zeros((N_PICKS, n_tok), jnp.int32)
    gate_sorted = jnp.zeros((N_PICKS, n_tok), F32)
    for e in range(N_PICKS):
        hit = pick == slot_s[e:e + 1, :]
        idx_sorted = jnp.where(hit, idx_s[e:e + 1, :], idx_sorted)
        gate_sorted = jnp.where(hit, gate_s[e:e + 1, :], gate_sorted)
    idxt_ref[...] = idx_sorted
    gatet_ref[...] = gate_sorted
    count0_ref[...] = count0
    idx = idx_sorted.astype(F32).T.astype(jnp.int32)
    loc0_ref[...] = _half_offsets(idx, 0)
    loc1_ref[...] = _half_offsets(idx, 1)


def _route(q, keys):
    row_spec = pl.BlockSpec((ROUTE_BLOCK, N_PICKS), lambda i: (i, 0))
    col_spec = pl.BlockSpec((N_PICKS, ROUTE_BLOCK), lambda i: (0, i))
    return pl.pallas_call(
        _route_kernel,
        grid=(N_TOK // ROUTE_BLOCK,),
        in_specs=[pl.BlockSpec((ROUTE_BLOCK, D_MODEL), lambda i: (i, 0)), _whole(keys.shape)],
        out_specs=[row_spec, row_spec, col_spec, col_spec,
                   pl.BlockSpec((1, ROUTE_BLOCK), lambda i: (0, i))],
        out_shape=[jax.ShapeDtypeStruct((N_TOK, N_PICKS), jnp.int32),
                   jax.ShapeDtypeStruct((N_TOK, N_PICKS), jnp.int32),
                   jax.ShapeDtypeStruct((N_PICKS, N_TOK), jnp.int32),
                   jax.ShapeDtypeStruct((N_PICKS, N_TOK), F32),
                   jax.ShapeDtypeStruct((1, N_TOK), jnp.int32)],
        scratch_shapes=[pltpu.VMEM((N_PICKS, ROUTE_BLOCK), jnp.int32),
                        pltpu.VMEM((N_PICKS, ROUTE_BLOCK), jnp.int32),
                        pltpu.VMEM((N_PICKS, ROUTE_BLOCK), F32)],
        compiler_params=pltpu.CompilerParams(vmem_limit_bytes=VMEM_LIMIT),
        name="route",
    )(q, keys)


N_GROUPS = N_PICKS // SUBLANES
WALK_GROUPS = 9


def _group_ranges(half, count0):
    group_bits = SUBLANES.bit_length() - 1
    if half == 0:
        return range(WALK_GROUPS), (WALK_GROUPS, (count0 + SUBLANES - 1) >> group_bits)
    return range(N_GROUPS - WALK_GROUPS, N_GROUPS), (count0 >> group_bits, N_GROUPS - WALK_GROUPS)


_MERGE_ORDER = (0, 4, 2, 6, 1, 5, 3, 7)


def _merge8(tiles, sub):
    m4 = sub < 4
    m2 = (sub & 3) < 2
    m1 = (sub & 1) == 0

    def l1(a, b):
        return jnp.where(m4, a, b) + pltpu.roll(jnp.where(m4, b, a), 4, 0)

    def l2(c, d):
        return jnp.where(m2, c, d) + jnp.where(m2, pltpu.roll(c, 6, 0), pltpu.roll(d, 2, 0))

    def l3(e, f):
        return jnp.where(m1, e, f) + jnp.where(m1, pltpu.roll(e, 7, 0), pltpu.roll(f, 1, 0))

    c = [l1(tiles[2 * i], tiles[2 * i + 1]) for i in range(4)]
    return l3(l2(c[0], c[1]), l2(c[2], c[3]))


def _peer_dots(half, cnt_ref, loc_ref, h_ref, tab_ref, actt_ref, gm):
    sub = lax.broadcasted_iota(jnp.int32, (SUBLANES, LANES), 0)
    lane = lax.broadcasted_iota(jnp.int32, (N_PICKS, PEER_BLOCK), 1)
    tok0 = pl.program_id(0) * PEER_BLOCK
    actt_ref[...] = jnp.zeros((N_PICKS, PEER_BLOCK), F32)
    for buf in gm:
        buf[...] = jnp.zeros((N_PICKS, LANES), F32)

    def group_sums(t, g):
        ht = h_ref[pl.ds(pl.multiple_of(t * SUBLANES, SUBLANES), SUBLANES), :]
        loc_row = loc_ref.at[t]
        prods = []
        for s in range(SUBLANES):
            r = pl.multiple_of(loc_row[g * SUBLANES + _MERGE_ORDER[s]], SUBLANES)
            prods.append(tab_ref[pl.ds(r, SUBLANES), :] * ht)
        return _merge8(prods, sub)

    def partial_sums(t, buf):
        for g in _group_ranges(half, 0)[0]:
            buf[g * SUBLANES:(g + 1) * SUBLANES, :] = group_sums(t, g)

    def extra_partial_sums(t, bufs):
        bounds = [_group_ranges(half, cnt_ref[tok0 + t + i])[1] for i in range(2)]

        @pl.when((bounds[0][1] > bounds[0][0]) | (bounds[1][1] > bounds[1][0]))
        def _():
            for i in range(2):
                def one(g, carry, i=i):
                    bufs[i][pl.ds(pl.multiple_of(g * SUBLANES, SUBLANES), SUBLANES), :] = group_sums(t + i, g)
                    return carry

                lax.fori_loop(*bounds[i], one, 0)

    def reduce_into_column(t, buf):
        col = jnp.sum(buf[...], axis=1, keepdims=True)
        pltpu.store(actt_ref, jnp.broadcast_to(col, (N_PICKS, PEER_BLOCK)), mask=lane == t)

    def four_tokens(k, carry):
        t = 4 * k
        reduce_into_column(t - 2, gm[2])
        reduce_into_column(t - 1, gm[3])
        partial_sums(t, gm[0])
        partial_sums(t + 1, gm[1])
        extra_partial_sums(t, gm[0:2])
        reduce_into_column(t, gm[0])
        reduce_into_column(t + 1, gm[1])
        partial_sums(t + 2, gm[2])
        partial_sums(t + 3, gm[3])
        extra_partial_sums(t + 2, gm[2:4])
        return carry

    lax.fori_loop(0, PEER_BLOCK // 4, four_tokens, 0)
    reduce_into_column(PEER_BLOCK - 2, gm[2])
    reduce_into_column(PEER_BLOCK - 1, gm[3])


def _peer_u_first_kernel(layer, cnt_ref, loc_ref, h_ref, tab_hbm, actt_ref, tab_s, sem, *gm):
    _load_table_half(0, layer, tab_hbm, tab_s, sem)
    _peer_dots(0, cnt_ref, loc_ref, h_ref, tab_s, actt_ref, gm)


def _peer_u_second_kernel(layer, cnt_ref, loc_ref, h_ref, tab_hbm, act0t_ref, idxt_ref, gatet_ref,
                          w0_ref, w1_ref, tab_s, sem, actt_s, *gm):
    _load_table_half(1, layer, tab_hbm, tab_s, sem)
    _peer_dots(1, cnt_ref, loc_ref, h_ref, tab_s, actt_s, gm)
    in0 = _table_half(idxt_ref[...]) == 0
    w = gatet_ref[...] * _gelu(jnp.where(in0, act0t_ref[...], actt_s[...]))
    w0_ref[...] = jnp.where(in0, w, 0.0)
    w1_ref[...] = jnp.where(in0, 0.0, w)


def _peer_sum(half, cnt_ref, loc_ref, w_ref, tab_ref, init, f_ref, wb):
    tok0 = pl.program_id(0) * PEER_BLOCK

    def splat_weights(t, buf):
        buf[...] = jnp.take_along_axis(w_ref[...], jnp.full((N_PICKS, PEER_BLOCK), t, jnp.int32), axis=1)

    def add_group(t, g, accs, weight_row):
        loc_row = loc_ref.at[t]
        accs = list(accs)
        for s in range(SUBLANES):
            r = pl.multiple_of(loc_row[g * SUBLANES + s], SUBLANES)
            w = jnp.broadcast_to(weight_row(g * SUBLANES + s), (SUBLANES, LANES))
            accs[s % 4] = accs[s % 4] + tab_ref[pl.ds(r, SUBLANES), :] * w
        return tuple(accs)

    def token_rows(t):
        return pl.ds(pl.multiple_of(t * SUBLANES, SUBLANES), SUBLANES)

    def weighted_sum(t, buf):
        accs = (init(token_rows(t)),) + (jnp.zeros((SUBLANES, LANES), F32),) * 3
        for g in _group_ranges(half, 0)[0]:
            accs = add_group(t, g, accs, lambda e: buf[e:e + 1, :])
        return accs

    def total(accs):
        return (accs[0] + accs[1]) + (accs[2] + accs[3])

    def extra_weighted_sums(t, bufs):
        bounds = [_group_ranges(half, cnt_ref[tok0 + t + i])[1] for i in range(2)]

        @pl.when((bounds[0][1] > bounds[0][0]) | (bounds[1][1] > bounds[1][0]))
        def _():
            for i in range(2):
                accs = (f_ref[token_rows(t + i), :],) + (jnp.zeros((SUBLANES, LANES), F32),) * 3
                accs = lax.fori_loop(
                    *bounds[i],
                    lambda g, a, i=i: add_group(t + i, g, a, lambda e: bufs[i][pl.ds(e, 1), :]), accs)
                f_ref[token_rows(t + i), :] = total(accs)

    def token_pair(t, bufs, next_t, next_bufs):
        splat_weights(jnp.minimum(next_t, PEER_BLOCK - 1), next_bufs[0])
        splat_weights(jnp.minimum(next_t + 1, PEER_BLOCK - 1), next_bufs[1])
        f_ref[token_rows(t), :] = total(weighted_sum(t, bufs[0]))
        f_ref[token_rows(t + 1), :] = total(weighted_sum(t + 1, bufs[1]))
        extra_weighted_sums(t, bufs)

    splat_weights(0, wb[0])
    splat_weights(1, wb[1])

    def four_tokens(k, carry):
        t = 4 * k
        token_pair(t, wb[0:2], t + 2, wb[2:4])
        token_pair(t + 2, wb[2:4], t + 4, wb[0:2])
        return carry

    lax.fori_loop(0, PEER_BLOCK // 4, four_tokens, 0)


def _peer_v_first_kernel(layer, cnt_ref, loc_ref, w_ref, tab_hbm, f_ref, tab_s, sem, *wb):
    _load_table_half(0, layer, tab_hbm, tab_s, sem)
    _peer_sum(0, cnt_ref, loc_ref, w_ref, tab_s, lambda rows: jnp.zeros((SUBLANES, LANES), F32), f_ref, wb)


def _peer_v_second_kernel(layer, cnt_ref, loc_ref, w_ref, tab_hbm, f0_ref, f_ref, tab_s, sem, *wb):
    _load_table_half(1, layer, tab_hbm, tab_s, sem)
    _peer_sum(1, cnt_ref, loc_ref, w_ref, tab_s, lambda rows: f0_ref[rows, :], f_ref, wb)


def _peer(h2, loc0, loc1, idx_t, gate_t, count0, u_tab, v_tab, layer):
    nblk = N_TOK // PEER_BLOCK
    h_tiles = h2.reshape(N_TOK * SUBLANES, LANES)
    cnt_spec = pl.BlockSpec(memory_space=pltpu.SMEM)
    smem_spec = pl.BlockSpec((PEER_BLOCK, N_PICKS), lambda i: (i, 0), memory_space=pltpu.SMEM)
    row_spec = pl.BlockSpec((PEER_BLOCK, N_PICKS), lambda i: (i, 0))
    col_spec = pl.BlockSpec((N_PICKS, PEER_BLOCK), lambda i: (0, i))
    tile_spec = pl.BlockSpec((PEER_BLOCK * SUBLANES, LANES), lambda i: (i, 0))
    hbm_spec = pl.BlockSpec(memory_space=pl.ANY)
    row_shape = jax.ShapeDtypeStruct((N_TOK, N_PICKS), F32)
    col_shape = jax.ShapeDtypeStruct((N_PICKS, N_TOK), F32)
    tile_shape = jax.ShapeDtypeStruct((N_TOK * SUBLANES, LANES), F32)
    params = pltpu.CompilerParams(vmem_limit_bytes=VMEM_LIMIT, dimension_semantics=("arbitrary",))
    table_scratch = [pltpu.VMEM((HALF_EXPERTS * SUBLANES, LANES), F32), pltpu.SemaphoreType.DMA]
    pair_scratch = [pltpu.VMEM((N_PICKS, LANES), F32)] * 4

    act0_t = pl.pallas_call(
        functools.partial(_peer_u_first_kernel, layer), grid=(nblk,),
        in_specs=[cnt_spec, smem_spec, tile_spec, hbm_spec],
        out_specs=col_spec, out_shape=col_shape, scratch_shapes=table_scratch + pair_scratch,
        compiler_params=params, name="peer_u0",
    )(count0, loc0, h_tiles, u_tab)
    w0, w1 = pl.pallas_call(
        functools.partial(_peer_u_second_kernel, layer), grid=(nblk,),
        in_specs=[cnt_spec, smem_spec, tile_spec, hbm_spec, col_spec, col_spec, col_spec],
        out_specs=[col_spec, col_spec], out_shape=[col_shape, col_shape],
        scratch_shapes=table_scratch + [pltpu.VMEM((N_PICKS, PEER_BLOCK), F32)] + pair_scratch,
        compiler_params=params, name="peer_u1",
    )(count0, loc1, h_tiles, u_tab, act0_t, idx_t, gate_t)
    f0 = pl.pallas_call(
        functools.partial(_peer_v_first_kernel, layer), grid=(nblk,),
        in_specs=[cnt_spec, smem_spec, col_spec, hbm_spec],
        out_specs=tile_spec, out_shape=tile_shape, scratch_shapes=table_scratch + pair_scratch,
        compiler_params=params, name="peer_v0",
    )(count0, loc0, w0, v_tab)
    f = pl.pallas_call(
        functools.partial(_peer_v_second_kernel, layer), grid=(nblk,),
        in_specs=[cnt_spec, smem_spec, col_spec, hbm_spec, tile_spec],
        out_specs=tile_spec, out_shape=tile_shape, scratch_shapes=table_scratch + pair_scratch,
        compiler_params=params, name="peer_v1",
    )(count0, loc1, w1, v_tab, f0)
    return f.reshape(N_TOK, D_MODEL)


def _close_kernel(x1_ref, f_ref, mod_ref, g_ref, b_ref, o_ref):
    g2 = mod_ref[...][:, 5 * D_MODEL:6 * D_MODEL]
    o_ref[...] = _ln(ALPHA * x1_ref[...] + g2 * f_ref[...]) * g_ref[1:2, :] + b_ref[1:2, :]


def _close(x1, f, mods, ln_g, ln_b, layer):
    row_spec = pl.BlockSpec((ROW_BLOCK, D_MODEL), lambda i: (i, 0))
    vec_spec = pl.BlockSpec((None, 2, D_MODEL), lambda i: (layer, 0, 0))
    return pl.pallas_call(
        _close_kernel,
        grid=(N_ROW_BLOCKS,),
        in_specs=[row_spec, row_spec,
                  pl.BlockSpec((None, None, 1, 6 * D_MODEL), lambda i: (layer, _mod_group(i), 0, 0)),
                  vec_spec, vec_spec],
        out_specs=row_spec,
        out_shape=jax.ShapeDtypeStruct((N_TOK, D_MODEL), F32),
        compiler_params=pltpu.CompilerParams(vmem_limit_bytes=VMEM_LIMIT),
        name="close",
    )(x1, f, mods, ln_g, ln_b)


def _rope_tables(dim, copies):
    rows = LAT_LEN // GRID_W
    row = jnp.repeat(jnp.arange(rows, dtype=F32), GRID_W)
    col = jnp.tile(jnp.arange(GRID_W, dtype=F32), rows)
    nf = dim // 4
    inv = ROPE_BASE ** (-jnp.arange(nf, dtype=F32) / nf)
    ar = row[:, None] * inv
    ac = col[:, None] * inv
    ang = jnp.concatenate([ar, ar, ac, ac], axis=-1)
    cos, sin = jnp.cos(ang), jnp.sin(ang)
    even_quarter = (jnp.arange(dim) // nf) % 2 == 0
    sin_up = jnp.where(even_quarter, -sin, 0.0)
    sin_dn = jnp.where(even_quarter, 0.0, sin)
    return tuple(jnp.tile(t, (1, copies)) for t in (cos, sin_up, sin_dn))


def kernel(x_prompt, x_sample, c, cache_win_k, cache_win_v, cache_diff_k, cache_diff_v, c_ctx, w_mod, b_mod, w_in, w_out, chunk_w, chunk_b, win_sink, diff_lam_q, diff_lam_k, diff_subln_g, pool_w, pool_scale, ln_g, ln_b, peer_wq, peer_keys, peer_u, peer_v):
    cvec = jnp.concatenate([c_ctx[None, :], c, jnp.zeros((8 - 1 - N_LAT_SEQ, D_MODEL), F32)], axis=0)
    mods = _modulation(cvec, w_mod, b_mod)[:, :1 + N_LAT_SEQ].reshape(DEPTH, 1 + N_LAT_SEQ, 1, 6 * D_MODEL)

    w_in_b = w_in.astype(BF16)
    w_out_b = w_out.astype(BF16)
    wq_b = peer_wq.astype(BF16)
    cw_b = chunk_w.astype(BF16)
    cb_full = jnp.repeat(jnp.swapaxes(chunk_b, 1, 2), HEAD_D, axis=2)
    eye = jnp.eye(4, dtype=F32)
    wbd = jnp.einsum('lgcd,gh->lgchd', pool_w, eye).reshape(DEPTH, GROUP_W, GROUP_W).astype(BF16)
    caches = (cache_win_k.reshape(N_LAT_SEQ, DEPTH, CTX_LEN, 128),
              cache_win_v.reshape(N_LAT_SEQ, DEPTH, CTX_LEN, 128),
              cache_diff_k.reshape(N_LAT_SEQ, DEPTH, CTX_LEN, GROUP_W),
              cache_diff_v.reshape(N_LAT_SEQ, DEPTH, CTX_LEN, GROUP_W))
    rope_tabs = _rope_tables(HEAD_D, GROUP_W // HEAD_D) + _rope_tables(C_SUB, GROUP_W // C_SUB)
    u_tiles = peer_u.reshape(DEPTH, N_EXPERTS * SUBLANES, LANES)
    v_tiles = peer_v.reshape(DEPTH, N_EXPERTS * SUBLANES, LANES)

    x = jnp.concatenate([x_prompt.reshape(N_CTX_TOK, D_MODEL), x_sample.reshape(N_LAT_TOK, D_MODEL)], axis=0)
    kbs, vbs, kcs, vcs = [], [], [], []
    for l in range(DEPTH):
        lam_init = 0.8 - 0.6 * math.exp(-0.3 * l)
        layer_w = (cw_b[l], cb_full[l], win_sink[l], diff_lam_q[l], diff_lam_k[l],
                   diff_subln_g[l].reshape(1, HEAD_D), wbd[l], pool_scale[l].reshape(1, GROUP_W))
        p = _in_proj(x, mods, w_in_b, l)
        pc = p[:N_CTX_TOK].reshape(N_CTX_SEQ, CTX_LEN, P_COLS)
        kbs.append(pc[..., COL_BK:COL_BV].reshape(N_CTX_SEQ, CTX_LEN, 2, HEAD_D))
        vbs.append(pc[..., COL_BV:COL_CQ].reshape(N_CTX_SEQ, CTX_LEN, 2, HEAD_D))
        kcs.append(pc[..., COL_CK:COL_CV].reshape(N_CTX_SEQ, CTX_LEN, 4, 2, C_SUB))
        vcs.append(pc[..., COL_CV:COL_DZ].reshape(N_CTX_SEQ, CTX_LEN, 4, HEAD_D))
        o = jnp.concatenate([_ctx_mix(p, layer_w, lam_init),
                             _lat_mix(p, caches, rope_tabs, layer_w, lam_init, l)], axis=0)
        x1, h2, q = _out_proj(o, x, mods, w_out_b, ln_g, ln_b, wq_b, l)
        loc0, loc1, idx_t, gate_t, count0 = _route(q, peer_keys[l])
        f = _peer(h2, loc0, loc1, idx_t, gate_t, count0.reshape(N_TOK), u_tiles, v_tiles, l)
        x = _close(x1, f, mods, ln_g, ln_b, l)
    return (x[:N_CTX_TOK].reshape(N_CTX_SEQ, CTX_LEN, D_MODEL),
            x[N_CTX_TOK:].reshape(N_LAT_SEQ, LAT_LEN, D_MODEL),
            jnp.stack(kbs, axis=1), jnp.stack(vbs, axis=1), jnp.stack(kcs, axis=1), jnp.stack(vcs, axis=1))
```

```python
import functools
import math

import jax
import jax.numpy as jnp
import numpy as np
from jax import lax
from jax.experimental import pallas as pl
from jax.experimental.pallas import tpu as pltpu

F32 = jnp.float32
BF16 = jnp.bfloat16
HIGHEST = lax.Precision.HIGHEST

D_MODEL = 1024
DEPTH = 4
N_CTX_SEQ = 16
CTX_LEN = 256
N_LAT_SEQ = 2
LAT_LEN = 1024
N_CTX_TOK = N_CTX_SEQ * CTX_LEN
N_LAT_TOK = N_LAT_SEQ * LAT_LEN
N_TOK = N_CTX_TOK + N_LAT_TOK
GRID_W = 64
BLOCK = 128
GROUP_W = 256
HEAD_D = 64
C_SUB = 32
WINDOW = 128
POOL_SIZES = (2, 4, 8, 16)
P_COLS = 2048
COL_AU, COL_AV, COL_BQ, COL_BK, COL_BV, COL_CQ, COL_CK, COL_CV, COL_DZ = (
    0, 256, 512, 768, 896, 1024, 1280, 1536, 1792)
PK_HEADS = 8
N_KEYS = 128
N_EXPERTS = N_KEYS * N_KEYS
PK_TOPK = 16
N_PICKS = PK_HEADS * PK_TOPK
ROPE_BASE = 10000.0
ALPHA = (2 * DEPTH) ** 0.25
LN_EPS = 1e-5
NEG = -1e30

ROW_BLOCK = 512
N_ROW_BLOCKS = N_TOK // ROW_BLOCK
N_CTX_ROW_BLOCKS = N_CTX_TOK // ROW_BLOCK
ROW_BLOCKS_PER_LAT_SEQ = LAT_LEN // ROW_BLOCK
ROUTE_BLOCK = 128
PEER_BLOCK = 128
SUBLANES = 8
LANES = 128
HALF_EXPERTS = N_EXPERTS // 2
VMEM_LIMIT = 56 * 1024 * 1024


def _mm(a, b):
    return jnp.dot(a.astype(BF16), b.astype(BF16), preferred_element_type=F32)


def _mm_nt(a, b):
    return lax.dot_general(a.astype(BF16), b.astype(BF16), (((1,), (1,)), ((), ())),
                           preferred_element_type=F32)


def _gelu(x):
    return x * (0.5 * (1.0 + jnp.tanh(0.7978845608028654 * (x + 0.044715 * (x * x * x)))))


def _ln(x):
    mu = jnp.mean(x, axis=-1, keepdims=True)
    xc = x - mu
    var = jnp.mean(xc * xc, axis=-1, keepdims=True)
    return xc * lax.rsqrt(var + LN_EPS)


def _mod_group(i):
    return jnp.where(i < N_CTX_ROW_BLOCKS, 0, 1 + (i - N_CTX_ROW_BLOCKS) // ROW_BLOCKS_PER_LAT_SEQ)


def _mod_kernel(c_ref, w_ref, b_ref, o_ref):
    c = c_ref[...]
    s = c * jax.nn.sigmoid(c)
    o_ref[...] = jnp.dot(s, w_ref[...], precision=HIGHEST, preferred_element_type=F32) + b_ref[...]


def _modulation(cvec, w_mod, b_mod):
    nj = 4
    cols = 6 * D_MODEL // nj
    return pl.pallas_call(
        _mod_kernel,
        grid=(DEPTH, nj),
        in_specs=[pl.BlockSpec((8, D_MODEL), lambda l, j: (0, 0)),
                  pl.BlockSpec((None, D_MODEL, cols), lambda l, j: (l, 0, j)),
                  pl.BlockSpec((None, 1, cols), lambda l, j: (l, 0, j))],
        out_specs=pl.BlockSpec((None, 8, cols), lambda l, j: (l, 0, j)),
        out_shape=jax.ShapeDtypeStruct((DEPTH, 8, 6 * D_MODEL), F32),
        compiler_params=pltpu.CompilerParams(vmem_limit_bytes=VMEM_LIMIT),
        name="modulation",
    )(cvec, w_mod, b_mod.reshape(DEPTH, 1, 6 * D_MODEL))


def _in_proj_kernel(x_ref, mod_ref, w_ref, o_ref):
    m = mod_ref[...]
    h = _ln(x_ref[...]) * (1.0 + m[:, D_MODEL:2 * D_MODEL]) + m[:, 0:D_MODEL]
    o_ref[...] = jnp.dot(h.astype(BF16), w_ref[...], preferred_element_type=F32)


def _in_proj(x, mods, w_in_bf16, layer):
    return pl.pallas_call(
        _in_proj_kernel,
        grid=(N_ROW_BLOCKS,),
        in_specs=[pl.BlockSpec((ROW_BLOCK, D_MODEL), lambda i: (i, 0)),
                  pl.BlockSpec((None, None, 1, 6 * D_MODEL), lambda i: (layer, _mod_group(i), 0, 0)),
                  pl.BlockSpec((None, D_MODEL, P_COLS), lambda i: (layer, 0, 0))],
        out_specs=pl.BlockSpec((ROW_BLOCK, P_COLS), lambda i: (i, 0)),
        out_shape=jax.ShapeDtypeStruct((N_TOK, P_COLS), F32),
        compiler_params=pltpu.CompilerParams(vmem_limit_bytes=VMEM_LIMIT),
        name="in_proj",
    )(x, mods, w_in_bf16)


def _chunk_gating(u_raw, v_raw, cw_ref, cb_full):
    u = _gelu(u_raw)
    v = _gelu(v_raw).astype(BF16)
    head = lax.broadcasted_iota(jnp.int32, (BLOCK, GROUP_W), 1) // HEAD_D
    mixed = cb_full
    for h in range(4):
        mixed = mixed + jnp.where(head == h, jnp.dot(cw_ref[h], v, preferred_element_type=F32), 0.0)
    return u * mixed


def _pool(z_rows, z_win, row0, key0, seq_len, wbd_ref, pool_scale):
    r, k = z_rows.shape[0], z_win.shape[0]
    t = row0 + lax.broadcasted_iota(jnp.int32, (r, k), 0)
    pos = key0 + lax.broadcasted_iota(jnp.int32, (r, k), 1)
    t1 = row0 + lax.broadcasted_iota(jnp.int32, (r, 1), 0)
    z_hi = z_win.astype(BF16)
    z_lo = (z_win - z_hi.astype(F32)).astype(BF16)
    group = lax.broadcasted_iota(jnp.int32, (r, GROUP_W), 1) // HEAD_D
    mean = jnp.zeros((r, GROUP_W), F32)
    for g, w in enumerate(POOL_SIZES):
        lo = jnp.maximum(t - w // 2, 0)
        hi = jnp.minimum(t + w // 2, seq_len)
        band = jnp.where((pos >= lo) & (pos < hi), 1.0, 0.0).astype(BF16)
        tot = (jnp.dot(band, z_hi, preferred_element_type=F32)
               + jnp.dot(band, z_lo, preferred_element_type=F32))
        cnt = (jnp.minimum(t1 + w // 2, seq_len) - jnp.maximum(t1 - w // 2, 0)).astype(F32)
        mean = mean + jnp.where(group == g, tot / cnt, 0.0)
    pooled = mean - z_rows
    return jnp.dot(pooled.astype(BF16), wbd_ref[...], preferred_element_type=F32) * pool_scale


def _diff_lambda(lq_ref, lk_ref, lam_init):
    lq = lq_ref[...]
    lk = lk_ref[...]
    s0 = jnp.sum(lq[0:1] * lk[0:1], axis=-1, keepdims=True)
    s1 = jnp.sum(lq[1:2] * lk[1:2], axis=-1, keepdims=True)
    return jnp.exp(s0) - jnp.exp(s1) + lam_init


def _sub_norm(o, g, lam_init):
    ms = jnp.mean(o * o, axis=-1, keepdims=True)
    return o * lax.rsqrt(ms + LN_EPS) * g * (1.0 - lam_init)


def _ctx_mix_kernel(lam_init, p_ref, cw_ref, cb_ref, sink_ref, lq_ref, lk_ref, sg_ref, wbd_ref,
                    ps_ref, o_ref):
    s_len = CTX_LEN
    for n in range(s_len // BLOCK):
        rows = slice(n * BLOCK, (n + 1) * BLOCK)
        o_ref[rows, 0:GROUP_W] = _chunk_gating(p_ref[rows, COL_AU:COL_AU + GROUP_W],
                                               p_ref[rows, COL_AV:COL_AV + GROUP_W], cw_ref, cb_ref[...])
    for kv in range(2):
        kk = p_ref[:, COL_BK + kv * HEAD_D:COL_BK + (kv + 1) * HEAD_D].astype(BF16)
        vv = p_ref[:, COL_BV + kv * HEAD_D:COL_BV + (kv + 1) * HEAD_D].astype(BF16)
        for g in range(2):
            h = kv * 2 + g
            q = p_ref[:, COL_BQ + h * HEAD_D:COL_BQ + (h + 1) * HEAD_D] * (HEAD_D ** -0.5)
            s = _mm_nt(q, kk)
            sk = sink_ref[h]
            m = jnp.maximum(jnp.max(s, axis=-1, keepdims=True), sk)
            e = jnp.exp(s - m)
            den = jnp.sum(e, axis=-1, keepdims=True) + jnp.exp(sk - m)
            o_ref[:, GROUP_W + h * HEAD_D:GROUP_W + (h + 1) * HEAD_D] = _mm(e, vv) / den
    lam = _diff_lambda(lq_ref, lk_ref, lam_init)
    scale = C_SUB ** -0.5
    for h in range(4):
        vv = p_ref[:, COL_CV + h * HEAD_D:COL_CV + (h + 1) * HEAD_D].astype(BF16)
        parts = []
        for j in range(2):
            c0 = h * HEAD_D + j * C_SUB
            s = _mm_nt(p_ref[:, COL_CQ + c0:COL_CQ + c0 + C_SUB],
                       p_ref[:, COL_CK + c0:COL_CK + c0 + C_SUB]) * scale
            e = jnp.exp(s - jnp.max(s, axis=-1, keepdims=True))
            parts.append(_mm(e, vv) / jnp.sum(e, axis=-1, keepdims=True))
        o = parts[0] - lam * parts[1]
        o_ref[:, 2 * GROUP_W + h * HEAD_D:2 * GROUP_W + (h + 1) * HEAD_D] = _sub_norm(o, sg_ref[...], lam_init)
    z = p_ref[:, COL_DZ:COL_DZ + GROUP_W]
    o_ref[:, 3 * GROUP_W:4 * GROUP_W] = _pool(z, z, 0, 0, s_len, wbd_ref, ps_ref[...])


def _whole(shape):
    nd = len(shape)
    return pl.BlockSpec(shape, lambda *_: (0,) * nd)


def _ctx_mix(p, layer_w, lam_init):
    cw, cb_full, sink, lq, lk, sg, wbd, ps = layer_w
    return pl.pallas_call(
        functools.partial(_ctx_mix_kernel, lam_init),
        grid=(N_CTX_SEQ,),
        in_specs=[pl.BlockSpec((CTX_LEN, P_COLS), lambda b: (b, 0)),
                  _whole(cw.shape), _whole(cb_full.shape),
                  pl.BlockSpec(memory_space=pltpu.SMEM),
                  _whole(lq.shape), _whole(lk.shape), _whole(sg.shape), _whole(wbd.shape),
                  _whole(ps.shape)],
        out_specs=pl.BlockSpec((CTX_LEN, D_MODEL), lambda b: (b, 0)),
        out_shape=jax.ShapeDtypeStruct((N_CTX_TOK, D_MODEL), F32),
        compiler_params=pltpu.CompilerParams(vmem_limit_bytes=VMEM_LIMIT),
        name="ctx_mix",
    )(p, cw, cb_full, sink, lq, lk, sg, wbd, ps)


def _rope(x, cos, sin_up, sin_dn, shift):
    outs = []
    for c0 in range(0, x.shape[1], LANES):
        xs = x[:, c0:c0 + LANES]
        up = pltpu.roll(xs, LANES - shift, 1)
        dn = pltpu.roll(xs, shift, 1)
        outs.append(xs * cos[:, c0:c0 + LANES] + up * sin_up[:, c0:c0 + LANES]
                    + dn * sin_dn[:, c0:c0 + LANES])
    return outs[0] if len(outs) == 1 else jnp.concatenate(outs, axis=1)


def _lat_mix_kernel(lam_init, p_ref, wk_ref, wv_ref, dk_ref, dv_ref,
                    cosb_ref, sbu_ref, sbd_ref, cosc_ref, scu_ref, scd_ref,
                    cw_ref, cb_ref, sink_ref, lq_ref, lk_ref, sg_ref, wbd_ref, ps_ref,
                    o_ref, kb_s, kc_s):
    n = pl.program_id(1)
    s_len = LAT_LEN

    @pl.when(n == 0)
    def _():
        kb_s[...] = _rope(p_ref[:, COL_BK:COL_BK + 128], cosb_ref[:, 0:128], sbu_ref[:, 0:128],
                          sbd_ref[:, 0:128], HEAD_D // 4).astype(BF16)
        kc_s[...] = _rope(p_ref[:, COL_CK:COL_CK + GROUP_W], cosc_ref[...], scu_ref[...],
                          scd_ref[...], C_SUB // 4).astype(BF16)

    r0 = pl.multiple_of(n * BLOCK, BLOCK)
    rows = pl.ds(r0, BLOCK)
    o_ref[:, 0:GROUP_W] = _chunk_gating(p_ref[rows, COL_AU:COL_AU + GROUP_W],
                                        p_ref[rows, COL_AV:COL_AV + GROUP_W], cw_ref, cb_ref[...])
    w0 = pl.multiple_of(jnp.clip((n - 1) * BLOCK, 0, s_len - 3 * BLOCK), BLOCK)
    win = pl.ds(w0, 3 * BLOCK)
    qb = _rope(p_ref[rows, COL_BQ:COL_BQ + GROUP_W], cosb_ref[rows, :], sbu_ref[rows, :],
               sbd_ref[rows, :], HEAD_D // 4) * (HEAD_D ** -0.5)
    qpos = r0 + lax.broadcasted_iota(jnp.int32, (BLOCK, 3 * BLOCK), 0)
    kpos = w0 + lax.broadcasted_iota(jnp.int32, (BLOCK, 3 * BLOCK), 1)
    valid = jnp.abs(kpos - qpos) <= WINDOW
    for kv in range(2):
        cs = slice(kv * HEAD_D, (kv + 1) * HEAD_D)
        k_loc = kb_s[win, cs]
        v_loc = p_ref[win, COL_BV + kv * HEAD_D:COL_BV + (kv + 1) * HEAD_D].astype(BF16)
        k_ctx = wk_ref[:, cs].astype(BF16)
        v_ctx = wv_ref[:, cs].astype(BF16)
        for g in range(2):
            h = kv * 2 + g
            q = qb[:, h * HEAD_D:(h + 1) * HEAD_D]
            s_loc = jnp.where(valid, _mm_nt(q, k_loc), NEG)
            s_ctx = _mm_nt(q, k_ctx)
            sk = sink_ref[h]
            m = jnp.maximum(jnp.maximum(jnp.max(s_loc, axis=-1, keepdims=True),
                                        jnp.max(s_ctx, axis=-1, keepdims=True)), sk)
            e_loc = jnp.exp(s_loc - m)
            e_ctx = jnp.exp(s_ctx - m)
            den = (jnp.sum(e_loc, axis=-1, keepdims=True) + jnp.sum(e_ctx, axis=-1, keepdims=True)
                   + jnp.exp(sk - m))
            o_ref[:, GROUP_W + h * HEAD_D:GROUP_W + (h + 1) * HEAD_D] = (
                _mm(e_loc, v_loc) + _mm(e_ctx, v_ctx)) / den
    lam = _diff_lambda(lq_ref, lk_ref, lam_init)
    scale = C_SUB ** -0.5
    qc = _rope(p_ref[rows, COL_CQ:COL_CQ + GROUP_W], cosc_ref[rows, :], scu_ref[rows, :],
               scd_ref[rows, :], C_SUB // 4)
    for h in range(4):
        v_ctx = dv_ref[:, h * HEAD_D:(h + 1) * HEAD_D].astype(BF16)
        v_lat = p_ref[:, COL_CV + h * HEAD_D:COL_CV + (h + 1) * HEAD_D].astype(BF16)
        parts = []
        for j in range(2):
            cs = slice(h * HEAD_D + j * C_SUB, h * HEAD_D + (j + 1) * C_SUB)
            q = qc[:, cs]
            s_ctx = _mm_nt(q, dk_ref[:, cs]) * scale
            s_lat = _mm_nt(q, kc_s[:, cs]) * scale
            m = jnp.maximum(jnp.max(s_ctx, axis=-1, keepdims=True), jnp.max(s_lat, axis=-1, keepdims=True))
            e_ctx = jnp.exp(s_ctx - m)
            e_lat = jnp.exp(s_lat - m)
            den = jnp.sum(e_ctx, axis=-1, keepdims=True) + jnp.sum(e_lat, axis=-1, keepdims=True)
            parts.append((_mm(e_ctx, v_ctx) + _mm(e_lat, v_lat)) / den)
        o = parts[0] - lam * parts[1]
        o_ref[:, 2 * GROUP_W + h * HEAD_D:2 * GROUP_W + (h + 1) * HEAD_D] = _sub_norm(o, sg_ref[...], lam_init)
    o_ref[:, 3 * GROUP_W:4 * GROUP_W] = _pool(p_ref[rows, COL_DZ:COL_DZ + GROUP_W],
                                              p_ref[win, COL_DZ:COL_DZ + GROUP_W],
                                              r0, w0, s_len, wbd_ref, ps_ref[...])


def _lat_mix(p, caches, rope_tabs, layer_w, lam_init, layer):
    cw, cb_full, sink, lq, lk, sg, wbd, ps = layer_w
    nb = LAT_LEN // BLOCK
    ctx_blocks = N_CTX_TOK // LAT_LEN
    cache_spec = lambda width: pl.BlockSpec((None, None, CTX_LEN, width), lambda b, n: (b, layer, 0, 0))
    rope_spec = pl.BlockSpec((LAT_LEN, GROUP_W), lambda b, n: (0, 0))
    return pl.pallas_call(
        functools.partial(_lat_mix_kernel, lam_init),
        grid=(N_LAT_SEQ, nb),
        in_specs=[pl.BlockSpec((LAT_LEN, P_COLS), lambda b, n: (ctx_blocks + b, 0)),
                  cache_spec(128), cache_spec(128), cache_spec(256), cache_spec(256)]
                 + [rope_spec] * 6
                 + [_whole(cw.shape), _whole(cb_full.shape), pl.BlockSpec(memory_space=pltpu.SMEM),
                    _whole(lq.shape), _whole(lk.shape), _whole(sg.shape), _whole(wbd.shape),
                    _whole(ps.shape)],
        out_specs=pl.BlockSpec((BLOCK, D_MODEL), lambda b, n: (b * nb + n, 0)),
        out_shape=jax.ShapeDtypeStruct((N_LAT_TOK, D_MODEL), F32),
        scratch_shapes=[pltpu.VMEM((LAT_LEN, 128), BF16), pltpu.VMEM((LAT_LEN, GROUP_W), BF16)],
        compiler_params=pltpu.CompilerParams(vmem_limit_bytes=VMEM_LIMIT,
                                             dimension_semantics=("arbitrary", "arbitrary")),
        name="lat_mix",
    )(p, *caches, *rope_tabs, cw, cb_full, sink, lq, lk, sg, wbd, ps)


def _out_proj_kernel(o_ref, x_ref, mod_ref, wo_ref, g_ref, b_ref, wq_ref, x1_ref, h2_ref, q_ref):
    m = mod_ref[...]
    g1 = m[:, 2 * D_MODEL:3 * D_MODEL]
    sh2 = m[:, 3 * D_MODEL:4 * D_MODEL]
    sc2 = m[:, 4 * D_MODEL:5 * D_MODEL]
    y = jnp.dot(o_ref[...].astype(BF16), wo_ref[...], preferred_element_type=F32)
    x1 = _ln(ALPHA * x_ref[...] + g1 * y) * g_ref[0:1, :] + b_ref[0:1, :]
    x1_ref[...] = x1
    h2 = _ln(x1) * (1.0 + sc2) + sh2
    h2_ref[...] = h2
    q_ref[...] = jnp.dot(h2.astype(BF16), wq_ref[...], preferred_element_type=F32)


def _out_proj(o, x, mods, w_out_bf16, ln_g, ln_b, wq_bf16, layer):
    row_spec = pl.BlockSpec((ROW_BLOCK, D_MODEL), lambda i: (i, 0))
    mat_spec = pl.BlockSpec((None, D_MODEL, D_MODEL), lambda i: (layer, 0, 0))
    vec_spec = pl.BlockSpec((None, 2, D_MODEL), lambda i: (layer, 0, 0))
    shape = jax.ShapeDtypeStruct((N_TOK, D_MODEL), F32)
    return pl.pallas_call(
        _out_proj_kernel,
        grid=(N_ROW_BLOCKS,),
        in_specs=[row_spec, row_spec,
                  pl.BlockSpec((None, None, 1, 6 * D_MODEL), lambda i: (layer, _mod_group(i), 0, 0)),
                  mat_spec, vec_spec, vec_spec, mat_spec],
        out_specs=[row_spec, row_spec, row_spec],
        out_shape=[shape, shape, shape],
        compiler_params=pltpu.CompilerParams(vmem_limit_bytes=VMEM_LIMIT),
        name="out_proj",
    )(o, x, mods, w_out_bf16, ln_g, ln_b, wq_bf16)


def _top16_rows(sc, order):
    out_row = lax.broadcasted_iota(jnp.int32, (PK_TOPK, sc.shape[1]), 0)
    vals = jnp.zeros((PK_TOPK, sc.shape[1]), F32)
    sel = jnp.zeros((PK_TOPK, sc.shape[1]), jnp.int32)
    for i in range(PK_TOPK):
        m = jnp.max(sc, axis=0, keepdims=True)
        am = jnp.min(jnp.where(sc == m, order, jnp.int32(2 ** 30)), axis=0, keepdims=True)
        vals = jnp.where(out_row == i, m, vals)
        sel = jnp.where(out_row == i, am, sel)
        sc = jnp.where(order == am, NEG, sc)
    return vals, sel


def _pair_candidates(sv, si):
    n_tok = sv[0].shape[1]
    rows = lax.broadcasted_iota(jnp.int32, (SUBLANES, n_tok), 0)
    cands, flats, experts = [], [], []

    def piece(a, b0, valid):
        v = sv[0][a:a + 1, :] + sv[1][b0:b0 + SUBLANES, :]
        cands.append(v if valid >= SUBLANES else jnp.where(rows < valid, v, NEG))
        flats.append(a * PK_TOPK + b0 + rows)
        experts.append(si[0][a:a + 1, :] * N_KEYS + si[1][b0:b0 + SUBLANES, :])

    piece(0, 0, SUBLANES)
    piece(0, SUBLANES, SUBLANES)
    for a in range(1, SUBLANES):
        piece(a, 0, PK_TOPK // (a + 1))
    cands.append(sv[0][SUBLANES:, :] + sv[1][0:1, :])
    flats.append((SUBLANES + rows) * PK_TOPK)
    experts.append(si[0][SUBLANES:, :] * N_KEYS + si[1][0:1, :])
    order = jnp.concatenate(flats, axis=0) * N_EXPERTS + jnp.concatenate(experts, axis=0)
    return jnp.concatenate(cands, axis=0), order


RUN_EXPERTS = N_KEYS // 2
RUNS_PER_PARITY = N_KEYS // 2


KEY_BITS = N_KEYS.bit_length() - 1


def _table_half(idx):
    return ((idx >> KEY_BITS) ^ (idx >> (KEY_BITS - 1))) & 1


def _half_offsets(idx, half):
    i1 = idx >> KEY_BITS
    local = ((i1 & 1) * RUNS_PER_PARITY + (i1 >> 1)) * RUN_EXPERTS + (idx & (RUN_EXPERTS - 1))
    return jnp.where(_table_half(idx) == half, local, 0) * SUBLANES


def _load_table_half(half, layer, tab_hbm, tab_s, sem):
    @pl.when(pl.program_id(0) == 0)
    def _():
        run_rows = RUN_EXPERTS * SUBLANES
        copies = []
        for parity in range(2):
            for pair in range(RUNS_PER_PARITY):
                first = ((2 * pair + parity) * N_KEYS + (half ^ parity) * RUN_EXPERTS) * SUBLANES
                slot = (parity * RUNS_PER_PARITY + pair) * run_rows
                copies.append(pltpu.make_async_copy(tab_hbm.at[layer, pl.ds(first, run_rows), :],
                                                    tab_s.at[pl.ds(slot, run_rows), :], sem))
        for copy in copies:
            copy.start()
        for copy in copies:
            copy.wait()


def _route_kernel(q_ref, keys_ref, loc0_ref, loc1_ref, idxt_ref, gatet_ref, count0_ref,
                  slot_s, idx_s, gate_s):
    n_tok = q_ref.shape[0]
    key_row = lax.broadcasted_iota(jnp.int32, (N_KEYS, n_tok), 0)
    for h in range(PK_HEADS):
        sv = []
        si = []
        for j in range(2):
            c0 = (h * 2 + j) * HEAD_D
            sc = lax.dot_general(keys_ref[j], q_ref[:, c0:c0 + HEAD_D], (((1,), (1,)), ((), ())),
                                 precision=HIGHEST, preferred_element_type=F32)
            v, i = _top16_rows(sc, key_row)
            sv.append(v)
            si.append(i)
        cv, winners = _top16_rows(*_pair_candidates(sv, si))
        ce = winners & (N_EXPERTS - 1)
        e = jnp.exp(cv - jnp.max(cv, axis=0, keepdims=True))
        gate_s[h * PK_TOPK:(h + 1) * PK_TOPK, :] = e / jnp.sum(e, axis=0, keepdims=True)
        idx_s[h * PK_TOPK:(h + 1) * PK_TOPK, :] = ce
    in0 = _table_half(idx_s[...]) == 0
    before = (lax.broadcasted_iota(jnp.int32, (N_PICKS, N_PICKS), 1)
              < lax.broadcasted_iota(jnp.int32, (N_PICKS, N_PICKS), 0))
    rank0 = jnp.dot(jnp.where(before, 1.0, 0.0).astype(BF16), jnp.where(in0, 1.0, 0.0).astype(BF16),
                    preferred_element_type=F32).astype(jnp.int32)
    count0 = jnp.sum(jnp.where(in0, 1, 0), axis=0, keepdims=True)
    pick = lax.broadcasted_iota(jnp.int32, (N_PICKS, n_tok), 0)
    slot_s[...] = jnp.where(in0, rank0, count0 + pick - rank0)
    idx_sorted = jnp.zeros((N_PICKS, n_tok), jnp.int32)
    gate_sorted = jnp.zeros((N_PICKS, n_tok), F32)
    for e in range(N_PICKS):
        hit = pick == slot_s[e:e + 1, :]
        idx_sorted = jnp.where(hit, idx_s[e:e + 1, :], idx_sorted)
        gate_sorted = jnp.where(hit, gate_s[e:e + 1, :], gate_sorted)
    idxt_ref[...] = idx_sorted
    gatet_ref[...] = gate_sorted
    count0_ref[...] = count0
    loc0_ref[...] = _half_offsets(idx_sorted, 0)
    loc1_ref[...] = _half_offsets(idx_sorted, 1)


def _route(q, keys):
    col_spec = pl.BlockSpec((N_PICKS, ROUTE_BLOCK), lambda i: (0, i))
    return pl.pallas_call(
        _route_kernel,
        grid=(N_TOK // ROUTE_BLOCK,),
        in_specs=[pl.BlockSpec((ROUTE_BLOCK, D_MODEL), lambda i: (i, 0)), _whole(keys.shape)],
        out_specs=[col_spec, col_spec, col_spec, col_spec,
                   pl.BlockSpec((1, ROUTE_BLOCK), lambda i: (0, i))],
        out_shape=[jax.ShapeDtypeStruct((N_PICKS, N_TOK), jnp.int32),
                   jax.ShapeDtypeStruct((N_PICKS, N_TOK), jnp.int32),
                   jax.ShapeDtypeStruct((N_PICKS, N_TOK), jnp.int32),
                   jax.ShapeDtypeStruct((N_PICKS, N_TOK), F32),
                   jax.ShapeDtypeStruct((1, N_TOK), jnp.int32)],
        scratch_shapes=[pltpu.VMEM((N_PICKS, ROUTE_BLOCK), jnp.int32),
                        pltpu.VMEM((N_PICKS, ROUTE_BLOCK), jnp.int32),
                        pltpu.VMEM((N_PICKS, ROUTE_BLOCK), F32)],
        compiler_params=pltpu.CompilerParams(vmem_limit_bytes=VMEM_LIMIT),
        name="route",
    )(q, keys)


N_GROUPS = N_PICKS // SUBLANES
WALK_GROUPS = 9


def _group_ranges(half, count0):
    group_bits = SUBLANES.bit_length() - 1
    if half == 0:
        return range(WALK_GROUPS), (WALK_GROUPS, (count0 + SUBLANES - 1) >> group_bits)
    return range(N_GROUPS - WALK_GROUPS, N_GROUPS), (count0 >> group_bits, N_GROUPS - WALK_GROUPS)


_MERGE_ORDER = (0, 4, 2, 6, 1, 5, 3, 7)


def _merge8(tiles, sub):
    m4 = sub < 4
    m2 = (sub & 3) < 2
    m1 = (sub & 1) == 0

    def l1(a, b):
        return jnp.where(m4, a, b) + pltpu.roll(jnp.where(m4, b, a), 4, 0)

    def l2(c, d):
        return jnp.where(m2, c, d) + jnp.where(m2, pltpu.roll(c, 6, 0), pltpu.roll(d, 2, 0))

    def l3(e, f):
        return jnp.where(m1, e, f) + jnp.where(m1, pltpu.roll(e, 7, 0), pltpu.roll(f, 1, 0))

    c = [l1(tiles[2 * i], tiles[2 * i + 1]) for i in range(4)]
    return l3(l2(c[0], c[1]), l2(c[2], c[3]))


def _pick_offset(loc_ref, t, e):
    if isinstance(e, int):
        return pl.multiple_of(loc_ref.at[e][t], SUBLANES)
    return pl.multiple_of(loc_ref[e, t], SUBLANES)


def _with_pick_offsets(loc_hbm, loc_bufs, loc_sems, body):
    step = pl.program_id(0)

    def fetch(block, slot):
        return pltpu.make_async_copy(loc_hbm.at[:, pl.ds(block * PEER_BLOCK, PEER_BLOCK)], loc_bufs[slot],
                                     loc_sems.at[slot])

    @pl.when(step == 0)
    def _():
        fetch(0, 0).start()

    for slot in range(2):
        @pl.when(lax.rem(step, 2) == slot)
        def _(slot=slot):
            fetch(step, slot).wait()

            @pl.when(step + 1 < pl.num_programs(0))
            def _():
                fetch(step + 1, 1 - slot).start()

            body(loc_bufs[slot])


def _peer_dots(half, cnt_ref, loc_ref, h_ref, tab_ref, actt_ref, gm):
    sub = lax.broadcasted_iota(jnp.int32, (SUBLANES, LANES), 0)
    lane = lax.broadcasted_iota(jnp.int32, (N_PICKS, PEER_BLOCK), 1)
    tok0 = pl.program_id(0) * PEER_BLOCK
    actt_ref[...] = jnp.zeros((N_PICKS, PEER_BLOCK), F32)
    for buf in gm:
        buf[...] = jnp.zeros((N_PICKS, LANES), F32)

    def group_sums(t, g):
        ht = h_ref[pl.ds(pl.multiple_of(t * SUBLANES, SUBLANES), SUBLANES), :]
        prods = []
        for s in range(SUBLANES):
            r = _pick_offset(loc_ref, t, g * SUBLANES + _MERGE_ORDER[s])
            prods.append(tab_ref[pl.ds(r, SUBLANES), :] * ht)
        return _merge8(prods, sub)

    def partial_sums(t, buf):
        for g in _group_ranges(half, 0)[0]:
            buf[g * SUBLANES:(g + 1) * SUBLANES, :] = group_sums(t, g)

    def extra_partial_sums(t, bufs):
        bounds = [_group_ranges(half, cnt_ref[tok0 + t + i])[1] for i in range(2)]

        @pl.when((bounds[0][1] > bounds[0][0]) | (bounds[1][1] > bounds[1][0]))
        def _():
            for i in range(2):
                def one(g, carry, i=i):
                    bufs[i][pl.ds(pl.multiple_of(g * SUBLANES, SUBLANES), SUBLANES), :] = group_sums(t + i, g)
                    return carry

                lax.fori_loop(*bounds[i], one, 0)

    def reduce_into_column(t, buf):
        col = jnp.sum(buf[...], axis=1, keepdims=True)
        pltpu.store(actt_ref, jnp.broadcast_to(col, (N_PICKS, PEER_BLOCK)), mask=lane == t)

    def four_tokens(k, carry):
        t = 4 * k
        reduce_into_column(t - 2, gm[2])
        reduce_into_column(t - 1, gm[3])
        partial_sums(t, gm[0])
        partial_sums(t + 1, gm[1])
        extra_partial_sums(t, gm[0:2])
        reduce_into_column(t, gm[0])
        reduce_into_column(t + 1, gm[1])
        partial_sums(t + 2, gm[2])
        partial_sums(t + 3, gm[3])
        extra_partial_sums(t + 2, gm[2:4])
        return carry

    lax.fori_loop(0, PEER_BLOCK // 4, four_tokens, 0)
    reduce_into_column(PEER_BLOCK - 2, gm[2])
    reduce_into_column(PEER_BLOCK - 1, gm[3])


def _peer_u_first_kernel(layer, cnt_ref, loc_hbm, h_ref, tab_hbm, actt_ref, tab_s, sem, loc_a, loc_b,
                         loc_sems, *gm):
    _load_table_half(0, layer, tab_hbm, tab_s, sem)
    _with_pick_offsets(loc_hbm, (loc_a, loc_b), loc_sems,
                       lambda loc: _peer_dots(0, cnt_ref, loc, h_ref, tab_s, actt_ref, gm))


def _peer_u_second_kernel(layer, cnt_ref, loc_hbm, h_ref, tab_hbm, act0t_ref, idxt_ref, gatet_ref,
                          w0_ref, w1_ref, tab_s, sem, loc_a, loc_b, loc_sems, actt_s, *gm):
    _load_table_half(1, layer, tab_hbm, tab_s, sem)
    _with_pick_offsets(loc_hbm, (loc_a, loc_b), loc_sems,
                       lambda loc: _peer_dots(1, cnt_ref, loc, h_ref, tab_s, actt_s, gm))
    in0 = _table_half(idxt_ref[...]) == 0
    w = gatet_ref[...] * _gelu(jnp.where(in0, act0t_ref[...], actt_s[...]))
    w0_ref[...] = jnp.where(in0, w, 0.0)
    w1_ref[...] = jnp.where(in0, 0.0, w)


def _peer_sum(half, cnt_ref, loc_ref, w_ref, tab_ref, init, f_ref, wb):
    tok0 = pl.program_id(0) * PEER_BLOCK

    def splat_weights(t, buf):
        buf[...] = jnp.take_along_axis(w_ref[...], jnp.full((N_PICKS, PEER_BLOCK), t, jnp.int32), axis=1)

    def add_group(t, g, accs, weight_row):
        accs = list(accs)
        for s in range(SUBLANES):
            r = _pick_offset(loc_ref, t, g * SUBLANES + s)
            w = jnp.broadcast_to(weight_row(g * SUBLANES + s), (SUBLANES, LANES))
            accs[s % 4] = accs[s % 4] + tab_ref[pl.ds(r, SUBLANES), :] * w
        return tuple(accs)

    def token_rows(t):
        return pl.ds(pl.multiple_of(t * SUBLANES, SUBLANES), SUBLANES)

    def weighted_sum(t, buf):
        accs = (init(token_rows(t)),) + (jnp.zeros((SUBLANES, LANES), F32),) * 3
        for g in _group_ranges(half, 0)[0]:
            accs = add_group(t, g, accs, lambda e: buf[e:e + 1, :])
        return accs

    def total(accs):
        return (accs[0] + accs[1]) + (accs[2] + accs[3])

    def extra_weighted_sums(t, bufs):
        bounds = [_group_ranges(half, cnt_ref[tok0 + t + i])[1] for i in range(2)]

        @pl.when((bounds[0][1] > bounds[0][0]) | (bounds[1][1] > bounds[1][0]))
        def _():
            for i in range(2):
                accs = (f_ref[token_rows(t + i), :],) + (jnp.zeros((SUBLANES, LANES), F32),) * 3
                accs = lax.fori_loop(
                    *bounds[i],
                    lambda g, a, i=i: add_group(t + i, g, a, lambda e: bufs[i][pl.ds(e, 1), :]), accs)
                f_ref[token_rows(t + i), :] = total(accs)

    def token_pair(t, bufs, next_t, next_bufs):
        splat_weights(jnp.minimum(next_t, PEER_BLOCK - 1), next_bufs[0])
        splat_weights(jnp.minimum(next_t + 1, PEER_BLOCK - 1), next_bufs[1])
        f_ref[token_rows(t), :] = total(weighted_sum(t, bufs[0]))
        f_ref[token_rows(t + 1), :] = total(weighted_sum(t + 1, bufs[1]))
        extra_weighted_sums(t, bufs)

    splat_weights(0, wb[0])
    splat_weights(1, wb[1])

    def four_tokens(k, carry):
        t = 4 * k
        token_pair(t, wb[0:2], t + 2, wb[2:4])
        token_pair(t + 2, wb[2:4], t + 4, wb[0:2])
        return carry

    lax.fori_loop(0, PEER_BLOCK // 4, four_tokens, 0)


def _peer_v_first_kernel(layer, cnt_ref, loc_hbm, w_ref, tab_hbm, f_ref, tab_s, sem, loc_a, loc_b,
                         loc_sems, *wb):
    _load_table_half(0, layer, tab_hbm, tab_s, sem)
    zero_tile = lambda rows: jnp.zeros((SUBLANES, LANES), F32)
    _with_pick_offsets(loc_hbm, (loc_a, loc_b), loc_sems,
                       lambda loc: _peer_sum(0, cnt_ref, loc, w_ref, tab_s, zero_tile, f_ref, wb))


def _peer_v_second_kernel(layer, cnt_ref, loc_hbm, w_ref, tab_hbm, f0_ref, f_ref, tab_s, sem, loc_a, loc_b,
                          loc_sems, *wb):
    _load_table_half(1, layer, tab_hbm, tab_s, sem)
    _with_pick_offsets(loc_hbm, (loc_a, loc_b), loc_sems,
                       lambda loc: _peer_sum(1, cnt_ref, loc, w_ref, tab_s, lambda rows: f0_ref[rows, :],
                                             f_ref, wb))


def _peer(h2, loc0, loc1, idx_t, gate_t, count0, u_tab, v_tab, layer):
    nblk = N_TOK // PEER_BLOCK
    h_tiles = h2.reshape(N_TOK * SUBLANES, LANES)
    cnt_spec = pl.BlockSpec(memory_space=pltpu.SMEM)
    row_spec = pl.BlockSpec((PEER_BLOCK, N_PICKS), lambda i: (i, 0))
    col_spec = pl.BlockSpec((N_PICKS, PEER_BLOCK), lambda i: (0, i))
    tile_spec = pl.BlockSpec((PEER_BLOCK * SUBLANES, LANES), lambda i: (i, 0))
    hbm_spec = pl.BlockSpec(memory_space=pl.ANY)
    row_shape = jax.ShapeDtypeStruct((N_TOK, N_PICKS), F32)
    col_shape = jax.ShapeDtypeStruct((N_PICKS, N_TOK), F32)
    tile_shape = jax.ShapeDtypeStruct((N_TOK * SUBLANES, LANES), F32)
    params = pltpu.CompilerParams(vmem_limit_bytes=VMEM_LIMIT, dimension_semantics=("arbitrary",))
    table_scratch = [pltpu.VMEM((HALF_EXPERTS * SUBLANES, LANES), F32), pltpu.SemaphoreType.DMA]
    pick_scratch = [pltpu.SMEM((N_PICKS, PEER_BLOCK), jnp.int32)] * 2 + [pltpu.SemaphoreType.DMA((2,))]
    pair_scratch = [pltpu.VMEM((N_PICKS, LANES), F32)] * 4

    act0_t = pl.pallas_call(
        functools.partial(_peer_u_first_kernel, layer), grid=(nblk,),
        in_specs=[cnt_spec, hbm_spec, tile_spec, hbm_spec],
        out_specs=col_spec, out_shape=col_shape, scratch_shapes=table_scratch + pick_scratch + pair_scratch,
        compiler_params=params, name="peer_u0",
    )(count0, loc0, h_tiles, u_tab)
    w0, w1 = pl.pallas_call(
        functools.partial(_peer_u_second_kernel, layer), grid=(nblk,),
        in_specs=[cnt_spec, hbm_spec, tile_spec, hbm_spec, col_spec, col_spec, col_spec],
        out_specs=[col_spec, col_spec], out_shape=[col_shape, col_shape],
        scratch_shapes=(table_scratch + pick_scratch + [pltpu.VMEM((N_PICKS, PEER_BLOCK), F32)]
                        + pair_scratch),
        compiler_params=params, name="peer_u1",
    )(count0, loc1, h_tiles, u_tab, act0_t, idx_t, gate_t)
    f0 = pl.pallas_call(
        functools.partial(_peer_v_first_kernel, layer), grid=(nblk,),
        in_specs=[cnt_spec, hbm_spec, col_spec, hbm_spec],
        out_specs=tile_spec, out_shape=tile_shape, scratch_shapes=table_scratch + pick_scratch + pair_scratch,
        compiler_params=params, name="peer_v0",
    )(count0, loc0, w0, v_tab)
    f = pl.pallas_call(
        functools.partial(_peer_v_second_kernel, layer), grid=(nblk,),
        in_specs=[cnt_spec, hbm_spec, col_spec, hbm_spec, tile_spec],
        out_specs=tile_spec, out_shape=tile_shape, scratch_shapes=table_scratch + pick_scratch + pair_scratch,
        compiler_params=params, name="peer_v1",
    )(count0, loc1, w1, v_tab, f0)
    return f.reshape(N_TOK, D_MODEL)


def _close_kernel(x1_ref, f_ref, mod_ref, g_ref, b_ref, o_ref):
    g2 = mod_ref[...][:, 5 * D_MODEL:6 * D_MODEL]
    o_ref[...] = _ln(ALPHA * x1_ref[...] + g2 * f_ref[...]) * g_ref[1:2, :] + b_ref[1:2, :]


def _close(x1, f, mods, ln_g, ln_b, layer):
    row_spec = pl.BlockSpec((ROW_BLOCK, D_MODEL), lambda i: (i, 0))
    vec_spec = pl.BlockSpec((None, 2, D_MODEL), lambda i: (layer, 0, 0))
    return pl.pallas_call(
        _close_kernel,
        grid=(N_ROW_BLOCKS,),
        in_specs=[row_spec, row_spec,
                  pl.BlockSpec((None, None, 1, 6 * D_MODEL), lambda i: (layer, _mod_group(i), 0, 0)),
                  vec_spec, vec_spec],
        out_specs=row_spec,
        out_shape=jax.ShapeDtypeStruct((N_TOK, D_MODEL), F32),
        compiler_params=pltpu.CompilerParams(vmem_limit_bytes=VMEM_LIMIT),
        name="close",
    )(x1, f, mods, ln_g, ln_b)


def _rope_tables(dim, copies):
    rows = LAT_LEN // GRID_W
    row = jnp.repeat(jnp.arange(rows, dtype=F32), GRID_W)
    col = jnp.tile(jnp.arange(GRID_W, dtype=F32), rows)
    nf = dim // 4
    inv = ROPE_BASE ** (-jnp.arange(nf, dtype=F32) / nf)
    ar = row[:, None] * inv
    ac = col[:, None] * inv
    ang = jnp.concatenate([ar, ar, ac, ac], axis=-1)
    cos, sin = jnp.cos(ang), jnp.sin(ang)
    even_quarter = (jnp.arange(dim) // nf) % 2 == 0
    sin_up = jnp.where(even_quarter, -sin, 0.0)
    sin_dn = jnp.where(even_quarter, 0.0, sin)
    return tuple(jnp.tile(t, (1, copies)) for t in (cos, sin_up, sin_dn))


def kernel(x_prompt, x_sample, c, cache_win_k, cache_win_v, cache_diff_k, cache_diff_v, c_ctx, w_mod, b_mod, w_in, w_out, chunk_w, chunk_b, win_sink, diff_lam_q, diff_lam_k, diff_subln_g, pool_w, pool_scale, ln_g, ln_b, peer_wq, peer_keys, peer_u, peer_v):
    cvec = jnp.concatenate([c_ctx[None, :], c, jnp.zeros((8 - 1 - N_LAT_SEQ, D_MODEL), F32)], axis=0)
    mods = _modulation(cvec, w_mod, b_mod)[:, :1 + N_LAT_SEQ].reshape(DEPTH, 1 + N_LAT_SEQ, 1, 6 * D_MODEL)

    w_in_b = w_in.astype(BF16)
    w_out_b = w_out.astype(BF16)
    wq_b = peer_wq.astype(BF16)
    cw_b = chunk_w.astype(BF16)
    cb_full = jnp.repeat(jnp.swapaxes(chunk_b, 1, 2), HEAD_D, axis=2)
    eye = jnp.eye(4, dtype=F32)
    wbd = jnp.einsum('lgcd,gh->lgchd', pool_w, eye).reshape(DEPTH, GROUP_W, GROUP_W).astype(BF16)
    caches = (cache_win_k.reshape(N_LAT_SEQ, DEPTH, CTX_LEN, 128),
              cache_win_v.reshape(N_LAT_SEQ, DEPTH, CTX_LEN, 128),
              cache_diff_k.reshape(N_LAT_SEQ, DEPTH, CTX_LEN, GROUP_W),
              cache_diff_v.reshape(N_LAT_SEQ, DEPTH, CTX_LEN, GROUP_W))
    rope_tabs = _rope_tables(HEAD_D, GROUP_W // HEAD_D) + _rope_tables(C_SUB, GROUP_W // C_SUB)
    u_tiles = peer_u.reshape(DEPTH, N_EXPERTS * SUBLANES, LANES)
    v_tiles = peer_v.reshape(DEPTH, N_EXPERTS * SUBLANES, LANES)

    x = jnp.concatenate([x_prompt.reshape(N_CTX_TOK, D_MODEL), x_sample.reshape(N_LAT_TOK, D_MODEL)], axis=0)
    kbs, vbs, kcs, vcs = [], [], [], []
    for l in range(DEPTH):
        lam_init = 0.8 - 0.6 * math.exp(-0.3 * l)
        layer_w = (cw_b[l], cb_full[l], win_sink[l], diff_lam_q[l], diff_lam_k[l],
                   diff_subln_g[l].reshape(1, HEAD_D), wbd[l], pool_scale[l].reshape(1, GROUP_W))
        p = _in_proj(x, mods, w_in_b, l)
        pc = p[:N_CTX_TOK].reshape(N_CTX_SEQ, CTX_LEN, P_COLS)
        kbs.append(pc[..., COL_BK:COL_BV].reshape(N_CTX_SEQ, CTX_LEN, 2, HEAD_D))
        vbs.append(pc[..., COL_BV:COL_CQ].reshape(N_CTX_SEQ, CTX_LEN, 2, HEAD_D))
        kcs.append(pc[..., COL_CK:COL_CV].reshape(N_CTX_SEQ, CTX_LEN, 4, 2, C_SUB))
        vcs.append(pc[..., COL_CV:COL_DZ].reshape(N_CTX_SEQ, CTX_LEN, 4, HEAD_D))
        o = jnp.concatenate([_ctx_mix(p, layer_w, lam_init),
                             _lat_mix(p, caches, rope_tabs, layer_w, lam_init, l)], axis=0)
        x1, h2, q = _out_proj(o, x, mods, w_out_b, ln_g, ln_b, wq_b, l)
        loc0, loc1, idx_t, gate_t, count0 = _route(q, peer_keys[l])
        f = _peer(h2, loc0, loc1, idx_t, gate_t, count0.reshape(N_TOK), u_tiles, v_tiles, l)
        x = _close(x1, f, mods, ln_g, ln_b, l)
    return (x[:N_CTX_TOK].reshape(N_CTX_SEQ, CTX_LEN, D_MODEL),
            x[N_CTX_TOK:].reshape(N_LAT_SEQ, LAT_LEN, D_MODEL),
            jnp.stack(kbs, axis=1), jnp.stack(vbs, axis=1), jnp.stack(kcs, axis=1), jnp.stack(vcs, axis=1))
```

```python
import functools
import math

import jax
import jax.numpy as jnp
import numpy as np
from jax import lax
from jax.experimental import pallas as pl
from jax.experimental.pallas import tpu as pltpu

F32 = jnp.float32
BF16 = jnp.bfloat16
HIGHEST = lax.Precision.HIGHEST

D_MODEL = 1024
DEPTH = 4
N_CTX_SEQ = 16
CTX_LEN = 256
N_LAT_SEQ = 2
LAT_LEN = 1024
N_CTX_TOK = N_CTX_SEQ * CTX_LEN
N_LAT_TOK = N_LAT_SEQ * LAT_LEN
N_TOK = N_CTX_TOK + N_LAT_TOK
GRID_W = 64
BLOCK = 128
GROUP_W = 256
HEAD_D = 64
C_SUB = 32
WINDOW = 128
POOL_SIZES = (2, 4, 8, 16)
P_COLS = 2048
COL_AU, COL_AV, COL_BQ, COL_BK, COL_BV, COL_CQ, COL_CK, COL_CV, COL_DZ = (
    0, 256, 512, 768, 896, 1024, 1280, 1536, 1792)
PK_HEADS = 8
N_KEYS = 128
N_EXPERTS = N_KEYS * N_KEYS
PK_TOPK = 16
N_PICKS = PK_HEADS * PK_TOPK
ROPE_BASE = 10000.0
ALPHA = (2 * DEPTH) ** 0.25
LN_EPS = 1e-5
NEG = -1e30

ROW_BLOCK = 512
N_ROW_BLOCKS = N_TOK // ROW_BLOCK
N_CTX_ROW_BLOCKS = N_CTX_TOK // ROW_BLOCK
ROW_BLOCKS_PER_LAT_SEQ = LAT_LEN // ROW_BLOCK
ROUTE_BLOCK = 128
PEER_BLOCK = 128
SUBLANES = 8
LANES = 128
HALF_EXPERTS = N_EXPERTS // 2
VMEM_LIMIT = 56 * 1024 * 1024


def _mm(a, b):
    return jnp.dot(a.astype(BF16), b.astype(BF16), preferred_element_type=F32)


def _mm_nt(a, b):
    return lax.dot_general(a.astype(BF16), b.astype(BF16), (((1,), (1,)), ((), ())),
                           preferred_element_type=F32)


def _gelu(x):
    return x * (0.5 * (1.0 + jnp.tanh(0.7978845608028654 * (x + 0.044715 * (x * x * x)))))


def _ln(x):
    mu = jnp.mean(x, axis=-1, keepdims=True)
    xc = x - mu
    var = jnp.mean(xc * xc, axis=-1, keepdims=True)
    return xc * lax.rsqrt(var + LN_EPS)


def _mod_group(i):
    return jnp.where(i < N_CTX_ROW_BLOCKS, 0, 1 + (i - N_CTX_ROW_BLOCKS) // ROW_BLOCKS_PER_LAT_SEQ)


def _mod_kernel(c_ref, w_ref, b_ref, o_ref):
    c = c_ref[...]
    s = c * jax.nn.sigmoid(c)
    o_ref[...] = jnp.dot(s, w_ref[...], precision=HIGHEST, preferred_element_type=F32) + b_ref[...]


def _modulation(cvec, w_mod, b_mod):
    nj = 4
    cols = 6 * D_MODEL // nj
    return pl.pallas_call(
        _mod_kernel,
        grid=(DEPTH, nj),
        in_specs=[pl.BlockSpec((8, D_MODEL), lambda l, j: (0, 0)),
                  pl.BlockSpec((None, D_MODEL, cols), lambda l, j: (l, 0, j)),
                  pl.BlockSpec((None, 1, cols), lambda l, j: (l, 0, j))],
        out_specs=pl.BlockSpec((None, 8, cols), lambda l, j: (l, 0, j)),
        out_shape=jax.ShapeDtypeStruct((DEPTH, 8, 6 * D_MODEL), F32),
        compiler_params=pltpu.CompilerParams(vmem_limit_bytes=VMEM_LIMIT),
        name="modulation",
    )(cvec, w_mod, b_mod.reshape(DEPTH, 1, 6 * D_MODEL))


def _in_proj_kernel(x_ref, mod_ref, w_ref, o_ref):
    m = mod_ref[...]
    h = _ln(x_ref[...]) * (1.0 + m[:, D_MODEL:2 * D_MODEL]) + m[:, 0:D_MODEL]
    o_ref[...] = jnp.dot(h.astype(BF16), w_ref[...], preferred_element_type=F32)


def _in_proj(x, mods, w_in_bf16, layer):
    return pl.pallas_call(
        _in_proj_kernel,
        grid=(N_ROW_BLOCKS,),
        in_specs=[pl.BlockSpec((ROW_BLOCK, D_MODEL), lambda i: (i, 0)),
                  pl.BlockSpec((None, None, 1, 6 * D_MODEL), lambda i: (layer, _mod_group(i), 0, 0)),
                  pl.BlockSpec((None, D_MODEL, P_COLS), lambda i: (layer, 0, 0))],
        out_specs=pl.BlockSpec((ROW_BLOCK, P_COLS), lambda i: (i, 0)),
        out_shape=jax.ShapeDtypeStruct((N_TOK, P_COLS), F32),
        compiler_params=pltpu.CompilerParams(vmem_limit_bytes=VMEM_LIMIT),
        name="in_proj",
    )(x, mods, w_in_bf16)


def _chunk_gating(u_raw, v_raw, cw_ref, cb_full):
    u = _gelu(u_raw)
    v = _gelu(v_raw).astype(BF16)
    head = lax.broadcasted_iota(jnp.int32, (BLOCK, GROUP_W), 1) // HEAD_D
    mixed = cb_full
    for h in range(4):
        mixed = mixed + jnp.where(head == h, jnp.dot(cw_ref[h], v, preferred_element_type=F32), 0.0)
    return u * mixed


def _pool(z_rows, z_win, row0, key0, seq_len, wbd_ref, pool_scale):
    r, k = z_rows.shape[0], z_win.shape[0]
    t = row0 + lax.broadcasted_iota(jnp.int32, (r, k), 0)
    pos = key0 + lax.broadcasted_iota(jnp.int32, (r, k), 1)
    t1 = row0 + lax.broadcasted_iota(jnp.int32, (r, 1), 0)
    z_hi = z_win.astype(BF16)
    z_lo = (z_win - z_hi.astype(F32)).astype(BF16)
    group = lax.broadcasted_iota(jnp.int32, (r, GROUP_W), 1) // HEAD_D
    mean = jnp.zeros((r, GROUP_W), F32)
    for g, w in enumerate(POOL_SIZES):
        lo = jnp.maximum(t - w // 2, 0)
        hi = jnp.minimum(t + w // 2, seq_len)
        band = jnp.where((pos >= lo) & (pos < hi), 1.0, 0.0).astype(BF16)
        tot = (jnp.dot(band, z_hi, preferred_element_type=F32)
               + jnp.dot(band, z_lo, preferred_element_type=F32))
        cnt = (jnp.minimum(t1 + w // 2, seq_len) - jnp.maximum(t1 - w // 2, 0)).astype(F32)
        mean = mean + jnp.where(group == g, tot / cnt, 0.0)
    pooled = mean - z_rows
    return jnp.dot(pooled.astype(BF16), wbd_ref[...], preferred_element_type=F32) * pool_scale


def _diff_lambda(lq_ref, lk_ref, lam_init):
    lq = lq_ref[...]
    lk = lk_ref[...]
    s0 = jnp.sum(lq[0:1] * lk[0:1], axis=-1, keepdims=True)
    s1 = jnp.sum(lq[1:2] * lk[1:2], axis=-1, keepdims=True)
    return jnp.exp(s0) - jnp.exp(s1) + lam_init


def _sub_norm(o, g, lam_init):
    ms = jnp.mean(o * o, axis=-1, keepdims=True)
    return o * lax.rsqrt(ms + LN_EPS) * g * (1.0 - lam_init)


def _ctx_mix_kernel(lam_init, p_ref, cw_ref, cb_ref, sink_ref, lq_ref, lk_ref, sg_ref, wbd_ref,
                    ps_ref, o_ref):
    s_len = CTX_LEN
    for n in range(s_len // BLOCK):
        rows = slice(n * BLOCK, (n + 1) * BLOCK)
        o_ref[rows, 0:GROUP_W] = _chunk_gating(p_ref[rows, COL_AU:COL_AU + GROUP_W],
                                               p_ref[rows, COL_AV:COL_AV + GROUP_W], cw_ref, cb_ref[...])
    for kv in range(2):
        kk = p_ref[:, COL_BK + kv * HEAD_D:COL_BK + (kv + 1) * HEAD_D].astype(BF16)
        vv = p_ref[:, COL_BV + kv * HEAD_D:COL_BV + (kv + 1) * HEAD_D].astype(BF16)
        for g in range(2):
            h = kv * 2 + g
            q = p_ref[:, COL_BQ + h * HEAD_D:COL_BQ + (h + 1) * HEAD_D] * (HEAD_D ** -0.5)
            s = _mm_nt(q, kk)
            sk = sink_ref[h]
            m = jnp.maximum(jnp.max(s, axis=-1, keepdims=True), sk)
            e = jnp.exp(s - m)
            den = jnp.sum(e, axis=-1, keepdims=True) + jnp.exp(sk - m)
            o_ref[:, GROUP_W + h * HEAD_D:GROUP_W + (h + 1) * HEAD_D] = _mm(e, vv) / den
    lam = _diff_lambda(lq_ref, lk_ref, lam_init)
    scale = C_SUB ** -0.5
    for h in range(4):
        vv = p_ref[:, COL_CV + h * HEAD_D:COL_CV + (h + 1) * HEAD_D].astype(BF16)
        parts = []
        for j in range(2):
            c0 = h * HEAD_D + j * C_SUB
            s = _mm_nt(p_ref[:, COL_CQ + c0:COL_CQ + c0 + C_SUB],
                       p_ref[:, COL_CK + c0:COL_CK + c0 + C_SUB]) * scale
            e = jnp.exp(s - jnp.max(s, axis=-1, keepdims=True))
            parts.append(_mm(e, vv) / jnp.sum(e, axis=-1, keepdims=True))
        o = parts[0] - lam * parts[1]
        o_ref[:, 2 * GROUP_W + h * HEAD_D:2 * GROUP_W + (h + 1) * HEAD_D] = _sub_norm(o, sg_ref[...], lam_init)
    z = p_ref[:, COL_DZ:COL_DZ + GROUP_W]
    o_ref[:, 3 * GROUP_W:4 * GROUP_W] = _pool(z, z, 0, 0, s_len, wbd_ref, ps_ref[...])


def _whole(shape):
    nd = len(shape)
    return pl.BlockSpec(shape, lambda *_: (0,) * nd)


def _ctx_mix(p, layer_w, lam_init):
    cw, cb_full, sink, lq, lk, sg, wbd, ps = layer_w
    return pl.pallas_call(
        functools.partial(_ctx_mix_kernel, lam_init),
        grid=(N_CTX_SEQ,),
        in_specs=[pl.BlockSpec((CTX_LEN, P_COLS), lambda b: (b, 0)),
                  _whole(cw.shape), _whole(cb_full.shape),
                  pl.BlockSpec(memory_space=pltpu.SMEM),
                  _whole(lq.shape), _whole(lk.shape), _whole(sg.shape), _whole(wbd.shape),
                  _whole(ps.shape)],
        out_specs=pl.BlockSpec((CTX_LEN, D_MODEL), lambda b: (b, 0)),
        out_shape=jax.ShapeDtypeStruct((N_CTX_TOK, D_MODEL), F32),
        compiler_params=pltpu.CompilerParams(vmem_limit_bytes=VMEM_LIMIT),
        name="ctx_mix",
    )(p, cw, cb_full, sink, lq, lk, sg, wbd, ps)


def _rope(x, cos, sin_up, sin_dn, shift):
    outs = []
    for c0 in range(0, x.shape[1], LANES):
        xs = x[:, c0:c0 + LANES]
        up = pltpu.roll(xs, LANES - shift, 1)
        dn = pltpu.roll(xs, shift, 1)
        outs.append(xs * cos[:, c0:c0 + LANES] + up * sin_up[:, c0:c0 + LANES]
                    + dn * sin_dn[:, c0:c0 + LANES])
    return outs[0] if len(outs) == 1 else jnp.concatenate(outs, axis=1)


def _lat_mix_kernel(lam_init, p_ref, wk_ref, wv_ref, dk_ref, dv_ref,
                    cosb_ref, sbu_ref, sbd_ref, cosc_ref, scu_ref, scd_ref,
                    cw_ref, cb_ref, sink_ref, lq_ref, lk_ref, sg_ref, wbd_ref, ps_ref,
                    o_ref, kb_s, kc_s):
    n = pl.program_id(1)
    s_len = LAT_LEN

    @pl.when(n == 0)
    def _():
        kb_s[...] = _rope(p_ref[:, COL_BK:COL_BK + 128], cosb_ref[:, 0:128], sbu_ref[:, 0:128],
                          sbd_ref[:, 0:128], HEAD_D // 4).astype(BF16)
        kc_s[...] = _rope(p_ref[:, COL_CK:COL_CK + GROUP_W], cosc_ref[...], scu_ref[...],
                          scd_ref[...], C_SUB // 4).astype(BF16)

    r0 = pl.multiple_of(n * BLOCK, BLOCK)
    rows = pl.ds(r0, BLOCK)
    o_ref[:, 0:GROUP_W] = _chunk_gating(p_ref[rows, COL_AU:COL_AU + GROUP_W],
                                        p_ref[rows, COL_AV:COL_AV + GROUP_W], cw_ref, cb_ref[...])
    w0 = pl.multiple_of(jnp.clip((n - 1) * BLOCK, 0, s_len - 3 * BLOCK), BLOCK)
    win = pl.ds(w0, 3 * BLOCK)
    qb = _rope(p_ref[rows, COL_BQ:COL_BQ + GROUP_W], cosb_ref[rows, :], sbu_ref[rows, :],
               sbd_ref[rows, :], HEAD_D // 4) * (HEAD_D ** -0.5)
    qpos = r0 + lax.broadcasted_iota(jnp.int32, (BLOCK, 3 * BLOCK), 0)
    kpos = w0 + lax.broadcasted_iota(jnp.int32, (BLOCK, 3 * BLOCK), 1)
    valid = jnp.abs(kpos - qpos) <= WINDOW
    for kv in range(2):
        cs = slice(kv * HEAD_D, (kv + 1) * HEAD_D)
        k_loc = kb_s[win, cs]
        v_loc = p_ref[win, COL_BV + kv * HEAD_D:COL_BV + (kv + 1) * HEAD_D].astype(BF16)
        k_ctx = wk_ref[:, cs].astype(BF16)
        v_ctx = wv_ref[:, cs].astype(BF16)
        for g in range(2):
            h = kv * 2 + g
            q = qb[:, h * HEAD_D:(h + 1) * HEAD_D]
            s_loc = jnp.where(valid, _mm_nt(q, k_loc), NEG)
            s_ctx = _mm_nt(q, k_ctx)
            sk = sink_ref[h]
            m = jnp.maximum(jnp.maximum(jnp.max(s_loc, axis=-1, keepdims=True),
                                        jnp.max(s_ctx, axis=-1, keepdims=True)), sk)
            e_loc = jnp.exp(s_loc - m)
            e_ctx = jnp.exp(s_ctx - m)
            den = (jnp.sum(e_loc, axis=-1, keepdims=True) + jnp.sum(e_ctx, axis=-1, keepdims=True)
                   + jnp.exp(sk - m))
            o_ref[:, GROUP_W + h * HEAD_D:GROUP_W + (h + 1) * HEAD_D] = (
                _mm(e_loc, v_loc) + _mm(e_ctx, v_ctx)) / den
    lam = _diff_lambda(lq_ref, lk_ref, lam_init)
    scale = C_SUB ** -0.5
    qc = _rope(p_ref[rows, COL_CQ:COL_CQ + GROUP_W], cosc_ref[rows, :], scu_ref[rows, :],
               scd_ref[rows, :], C_SUB // 4)
    for h in range(4):
        v_ctx = dv_ref[:, h * HEAD_D:(h + 1) * HEAD_D].astype(BF16)
        v_lat = p_ref[:, COL_CV + h * HEAD_D:COL_CV + (h + 1) * HEAD_D].astype(BF16)
        parts = []
        for j in range(2):
            cs = slice(h * HEAD_D + j * C_SUB, h * HEAD_D + (j + 1) * C_SUB)
            q = qc[:, cs]
            s_ctx = _mm_nt(q, dk_ref[:, cs]) * scale
            s_lat = _mm_nt(q, kc_s[:, cs]) * scale
            m = jnp.maximum(jnp.max(s_ctx, axis=-1, keepdims=True), jnp.max(s_lat, axis=-1, keepdims=True))
            e_ctx = jnp.exp(s_ctx - m)
            e_lat = jnp.exp(s_lat - m)
            den = jnp.sum(e_ctx, axis=-1, keepdims=True) + jnp.sum(e_lat, axis=-1, keepdims=True)
            parts.append((_mm(e_ctx, v_ctx) + _mm(e_lat, v_lat)) / den)
        o = parts[0] - lam * parts[1]
        o_ref[:, 2 * GROUP_W + h * HEAD_D:2 * GROUP_W + (h + 1) * HEAD_D] = _sub_norm(o, sg_ref[...], lam_init)
    o_ref[:, 3 * GROUP_W:4 * GROUP_W] = _pool(p_ref[rows, COL_DZ:COL_DZ + GROUP_W],
                                              p_ref[win, COL_DZ:COL_DZ + GROUP_W],
                                              r0, w0, s_len, wbd_ref, ps_ref[...])


def _lat_mix(p, caches, rope_tabs, layer_w, lam_init, layer):
    cw, cb_full, sink, lq, lk, sg, wbd, ps = layer_w
    nb = LAT_LEN // BLOCK
    ctx_blocks = N_CTX_TOK // LAT_LEN
    cache_spec = lambda width: pl.BlockSpec((None, None, CTX_LEN, width), lambda b, n: (b, layer, 0, 0))
    rope_spec = pl.BlockSpec((LAT_LEN, GROUP_W), lambda b, n: (0, 0))
    return pl.pallas_call(
        functools.partial(_lat_mix_kernel, lam_init),
        grid=(N_LAT_SEQ, nb),
        in_specs=[pl.BlockSpec((LAT_LEN, P_COLS), lambda b, n: (ctx_blocks + b, 0)),
                  cache_spec(128), cache_spec(128), cache_spec(256), cache_spec(256)]
                 + [rope_spec] * 6
                 + [_whole(cw.shape), _whole(cb_full.shape), pl.BlockSpec(memory_space=pltpu.SMEM),
                    _whole(lq.shape), _whole(lk.shape), _whole(sg.shape), _whole(wbd.shape),
                    _whole(ps.shape)],
        out_specs=pl.BlockSpec((BLOCK, D_MODEL), lambda b, n: (b * nb + n, 0)),
        out_shape=jax.ShapeDtypeStruct((N_LAT_TOK, D_MODEL), F32),
        scratch_shapes=[pltpu.VMEM((LAT_LEN, 128), BF16), pltpu.VMEM((LAT_LEN, GROUP_W), BF16)],
        compiler_params=pltpu.CompilerParams(vmem_limit_bytes=VMEM_LIMIT,
                                             dimension_semantics=("arbitrary", "arbitrary")),
        name="lat_mix",
    )(p, *caches, *rope_tabs, cw, cb_full, sink, lq, lk, sg, wbd, ps)


def _out_proj_kernel(o_ref, x_ref, mod_ref, wo_ref, g_ref, b_ref, wq_ref, x1_ref, h2_ref, q_ref):
    m = mod_ref[...]
    g1 = m[:, 2 * D_MODEL:3 * D_MODEL]
    sh2 = m[:, 3 * D_MODEL:4 * D_MODEL]
    sc2 = m[:, 4 * D_MODEL:5 * D_MODEL]
    y = jnp.dot(o_ref[...].astype(BF16), wo_ref[...], preferred_element_type=F32)
    x1 = _ln(ALPHA * x_ref[...] + g1 * y) * g_ref[0:1, :] + b_ref[0:1, :]
    x1_ref[...] = x1
    h2 = _ln(x1) * (1.0 + sc2) + sh2
    h2_ref[...] = h2
    q_ref[...] = jnp.dot(h2.astype(BF16), wq_ref[...], preferred_element_type=F32)


def _out_proj(o, x, mods, w_out_bf16, ln_g, ln_b, wq_bf16, layer):
    row_spec = pl.BlockSpec((ROW_BLOCK, D_MODEL), lambda i: (i, 0))
    mat_spec = pl.BlockSpec((None, D_MODEL, D_MODEL), lambda i: (layer, 0, 0))
    vec_spec = pl.BlockSpec((None, 2, D_MODEL), lambda i: (layer, 0, 0))
    shape = jax.ShapeDtypeStruct((N_TOK, D_MODEL), F32)
    return pl.pallas_call(
        _out_proj_kernel,
        grid=(N_ROW_BLOCKS,),
        in_specs=[row_spec, row_spec,
                  pl.BlockSpec((None, None, 1, 6 * D_MODEL), lambda i: (layer, _mod_group(i), 0, 0)),
                  mat_spec, vec_spec, vec_spec, mat_spec],
        out_specs=[row_spec, row_spec, row_spec],
        out_shape=[shape, shape, shape],
        compiler_params=pltpu.CompilerParams(vmem_limit_bytes=VMEM_LIMIT),
        name="out_proj",
    )(o, x, mods, w_out_bf16, ln_g, ln_b, wq_bf16)


def _top16_rows(sc, order):
    out_row = lax.broadcasted_iota(jnp.int32, (PK_TOPK, sc.shape[1]), 0)
    vals = jnp.zeros((PK_TOPK, sc.shape[1]), F32)
    sel = jnp.zeros((PK_TOPK, sc.shape[1]), jnp.int32)
    for i in range(PK_TOPK):
        m = jnp.max(sc, axis=0, keepdims=True)
        am = jnp.min(jnp.where(sc == m, order, jnp.int32(2 ** 30)), axis=0, keepdims=True)
        vals = jnp.where(out_row == i, m, vals)
        sel = jnp.where(out_row == i, am, sel)
        sc = jnp.where(order == am, NEG, sc)
    return vals, sel


def _pair_candidates(sv, si):
    n_tok = sv[0].shape[1]
    rows = lax.broadcasted_iota(jnp.int32, (SUBLANES, n_tok), 0)
    cands, flats, experts = [], [], []

    def piece(a, b0, valid):
        v = sv[0][a:a + 1, :] + sv[1][b0:b0 + SUBLANES, :]
        cands.append(v if valid >= SUBLANES else jnp.where(rows < valid, v, NEG))
        flats.append(a * PK_TOPK + b0 + rows)
        experts.append(si[0][a:a + 1, :] * N_KEYS + si[1][b0:b0 + SUBLANES, :])

    piece(0, 0, SUBLANES)
    piece(0, SUBLANES, SUBLANES)
    for a in range(1, SUBLANES):
        piece(a, 0, PK_TOPK // (a + 1))
    cands.append(sv[0][SUBLANES:, :] + sv[1][0:1, :])
    flats.append((SUBLANES + rows) * PK_TOPK)
    experts.append(si[0][SUBLANES:, :] * N_KEYS + si[1][0:1, :])
    order = jnp.concatenate(flats, axis=0) * N_EXPERTS + jnp.concatenate(experts, axis=0)
    return jnp.concatenate(cands, axis=0), order


RUN_EXPERTS = N_KEYS // 2
RUNS_PER_PARITY = N_KEYS // 2


KEY_BITS = N_KEYS.bit_length() - 1


def _table_half(idx):
    return ((idx >> KEY_BITS) ^ (idx >> (KEY_BITS - 1))) & 1


def _half_offsets(idx, half):
    i1 = idx >> KEY_BITS
    local = ((i1 & 1) * RUNS_PER_PARITY + (i1 >> 1)) * RUN_EXPERTS + (idx & (RUN_EXPERTS - 1))
    return jnp.where(_table_half(idx) == half, local, 0) * SUBLANES


def _load_table_half(half, layer, tab_hbm, tab_s, sem):
    @pl.when(pl.program_id(0) == 0)
    def _():
        run_rows = RUN_EXPERTS * SUBLANES
        copies = []
        for parity in range(2):
            for pair in range(RUNS_PER_PARITY):
                first = ((2 * pair + parity) * N_KEYS + (half ^ parity) * RUN_EXPERTS) * SUBLANES
                slot = (parity * RUNS_PER_PARITY + pair) * run_rows
                copies.append(pltpu.make_async_copy(tab_hbm.at[layer, pl.ds(first, run_rows), :],
                                                    tab_s.at[pl.ds(slot, run_rows), :], sem))
        for copy in copies:
            copy.start()
        for copy in copies:
            copy.wait()


def _route_kernel(q_ref, keys_ref, loc0_ref, loc1_ref, loc0_rows_ref, loc1_rows_ref, idxt_ref, gatet_ref,
                  count0_ref, slot_s, idx_s, gate_s):
    n_tok = q_ref.shape[0]
    key_row = lax.broadcasted_iota(jnp.int32, (N_KEYS, n_tok), 0)
    for h in range(PK_HEADS):
        sv = []
        si = []
        for j in range(2):
            c0 = (h * 2 + j) * HEAD_D
            sc = lax.dot_general(keys_ref[j], q_ref[:, c0:c0 + HEAD_D], (((1,), (1,)), ((), ())),
                                 precision=HIGHEST, preferred_element_type=F32)
            v, i = _top16_rows(sc, key_row)
            sv.append(v)
            si.append(i)
        cv, winners = _top16_rows(*_pair_candidates(sv, si))
        ce = winners & (N_EXPERTS - 1)
        e = jnp.exp(cv - jnp.max(cv, axis=0, keepdims=True))
        gate_s[h * PK_TOPK:(h + 1) * PK_TOPK, :] = e / jnp.sum(e, axis=0, keepdims=True)
        idx_s[h * PK_TOPK:(h + 1) * PK_TOPK, :] = ce
    in0 = _table_half(idx_s[...]) == 0
    before = (lax.broadcasted_iota(jnp.int32, (N_PICKS, N_PICKS), 1)
              < lax.broadcasted_iota(jnp.int32, (N_PICKS, N_PICKS), 0))
    rank0 = jnp.dot(jnp.where(before, 1.0, 0.0).astype(BF16), jnp.where(in0, 1.0, 0.0).astype(BF16),
                    preferred_element_type=F32).astype(jnp.int32)
    count0 = jnp.sum(jnp.where(in0, 1, 0), axis=0, keepdims=True)
    pick = lax.broadcasted_iota(jnp.int32, (N_PICKS, n_tok), 0)
    slot_s[...] = jnp.where(in0, rank0, count0 + pick - rank0)
    idx_sorted = jnp.zeros((N_PICKS, n_tok), jnp.int32)
    gate_sorted = jnp.zeros((N_PICKS, n_tok), F32)
    for e in range(N_PICKS):
        hit = pick == slot_s[e:e + 1, :]
        idx_sorted = jnp.where(hit, idx_s[e:e + 1, :], idx_sorted)
        gate_sorted = jnp.where(hit, gate_s[e:e + 1, :], gate_sorted)
    idxt_ref[...] = idx_sorted
    gatet_ref[...] = gate_sorted
    count0_ref[...] = count0
    loc0_ref[...] = _half_offsets(idx_sorted, 0)
    loc1_ref[...] = _half_offsets(idx_sorted, 1)
    idx = idx_sorted.astype(F32).T.astype(jnp.int32)
    loc0_rows_ref[...] = _half_offsets(idx, 0)
    loc1_rows_ref[...] = _half_offsets(idx, 1)


def _route(q, keys):
    row_spec = pl.BlockSpec((ROUTE_BLOCK, N_PICKS), lambda i: (i, 0))
    col_spec = pl.BlockSpec((N_PICKS, ROUTE_BLOCK), lambda i: (0, i))
    return pl.pallas_call(
        _route_kernel,
        grid=(N_TOK // ROUTE_BLOCK,),
        in_specs=[pl.BlockSpec((ROUTE_BLOCK, D_MODEL), lambda i: (i, 0)), _whole(keys.shape)],
        out_specs=[col_spec, col_spec, row_spec, row_spec, col_spec, col_spec,
                   pl.BlockSpec((1, ROUTE_BLOCK), lambda i: (0, i))],
        out_shape=[jax.ShapeDtypeStruct((N_PICKS, N_TOK), jnp.int32),
                   jax.ShapeDtypeStruct((N_PICKS, N_TOK), jnp.int32),
                   jax.ShapeDtypeStruct((N_TOK, N_PICKS), jnp.int32),
                   jax.ShapeDtypeStruct((N_TOK, N_PICKS), jnp.int32),
                   jax.ShapeDtypeStruct((N_PICKS, N_TOK), jnp.int32),
                   jax.ShapeDtypeStruct((N_PICKS, N_TOK), F32),
                   jax.ShapeDtypeStruct((1, N_TOK), jnp.int32)],
        scratch_shapes=[pltpu.VMEM((N_PICKS, ROUTE_BLOCK), jnp.int32),
                        pltpu.VMEM((N_PICKS, ROUTE_BLOCK), jnp.int32),
                        pltpu.VMEM((N_PICKS, ROUTE_BLOCK), F32)],
        compiler_params=pltpu.CompilerParams(vmem_limit_bytes=VMEM_LIMIT),
        name="route",
    )(q, keys)


N_GROUPS = N_PICKS // SUBLANES
WALK_GROUPS = 9


def _group_ranges(half, count0):
    group_bits = SUBLANES.bit_length() - 1
    if half == 0:
        return range(WALK_GROUPS), (WALK_GROUPS, (count0 + SUBLANES - 1) >> group_bits)
    return range(N_GROUPS - WALK_GROUPS, N_GROUPS), (count0 >> group_bits, N_GROUPS - WALK_GROUPS)


_MERGE_ORDER = (0, 4, 2, 6, 1, 5, 3, 7)


def _merge8(tiles, sub):
    m4 = sub < 4
    m2 = (sub & 3) < 2
    m1 = (sub & 1) == 0

    def l1(a, b):
        return jnp.where(m4, a, b) + pltpu.roll(jnp.where(m4, b, a), 4, 0)

    def l2(c, d):
        return jnp.where(m2, c, d) + jnp.where(m2, pltpu.roll(c, 6, 0), pltpu.roll(d, 2, 0))

    def l3(e, f):
        return jnp.where(m1, e, f) + jnp.where(m1, pltpu.roll(e, 7, 0), pltpu.roll(f, 1, 0))

    c = [l1(tiles[2 * i], tiles[2 * i + 1]) for i in range(4)]
    return l3(l2(c[0], c[1]), l2(c[2], c[3]))


def _pick_offset(loc_ref, t, e, picks_major):
    if not picks_major:
        return pl.multiple_of(loc_ref.at[t][e], SUBLANES)
    if isinstance(e, int):
        return pl.multiple_of(loc_ref.at[e][t], SUBLANES)
    return pl.multiple_of(loc_ref[e, t], SUBLANES)


def _with_pick_offsets(loc_hbm, loc_bufs, loc_sems, body):
    step = pl.program_id(0)

    def fetch(block, slot):
        return pltpu.make_async_copy(loc_hbm.at[:, pl.ds(block * PEER_BLOCK, PEER_BLOCK)], loc_bufs[slot],
                                     loc_sems.at[slot])

    @pl.when(step == 0)
    def _():
        fetch(0, 0).start()

    for slot in range(2):
        @pl.when(lax.rem(step, 2) == slot)
        def _(slot=slot):
            fetch(step, slot).wait()

            @pl.when(step + 1 < pl.num_programs(0))
            def _():
                fetch(step + 1, 1 - slot).start()

            body(loc_bufs[slot])


def _peer_dots(half, cnt_ref, loc_ref, h_ref, tab_ref, actt_ref, gm):
    sub = lax.broadcasted_iota(jnp.int32, (SUBLANES, LANES), 0)
    lane = lax.broadcasted_iota(jnp.int32, (N_PICKS, PEER_BLOCK), 1)
    tok0 = pl.program_id(0) * PEER_BLOCK
    actt_ref[...] = jnp.zeros((N_PICKS, PEER_BLOCK), F32)
    for buf in gm:
        buf[...] = jnp.zeros((N_PICKS, LANES), F32)

    def group_sums(t, g):
        ht = h_ref[pl.ds(pl.multiple_of(t * SUBLANES, SUBLANES), SUBLANES), :]
        prods = []
        for s in range(SUBLANES):
            r = _pick_offset(loc_ref, t, g * SUBLANES + _MERGE_ORDER[s], True)
            prods.append(tab_ref[pl.ds(r, SUBLANES), :] * ht)
        return _merge8(prods, sub)

    def partial_sums(t, buf):
        for g in _group_ranges(half, 0)[0]:
            buf[g * SUBLANES:(g + 1) * SUBLANES, :] = group_sums(t, g)

    def extra_partial_sums(t, bufs):
        bounds = [_group_ranges(half, cnt_ref[tok0 + t + i])[1] for i in range(2)]

        @pl.when((bounds[0][1] > bounds[0][0]) | (bounds[1][1] > bounds[1][0]))
        def _():
            for i in range(2):
                def one(g, carry, i=i):
                    bufs[i][pl.ds(pl.multiple_of(g * SUBLANES, SUBLANES), SUBLANES), :] = group_sums(t + i, g)
                    return carry

                lax.fori_loop(*bounds[i], one, 0)

    def reduce_into_column(t, buf):
        col = jnp.sum(buf[...], axis=1, keepdims=True)
        pltpu.store(actt_ref, jnp.broadcast_to(col, (N_PICKS, PEER_BLOCK)), mask=lane == t)

    def four_tokens(k, carry):
        t = 4 * k
        reduce_into_column(t - 2, gm[2])
        reduce_into_column(t - 1, gm[3])
        partial_sums(t, gm[0])
        partial_sums(t + 1, gm[1])
        extra_partial_sums(t, gm[0:2])
        reduce_into_column(t, gm[0])
        reduce_into_column(t + 1, gm[1])
        partial_sums(t + 2, gm[2])
        partial_sums(t + 3, gm[3])
        extra_partial_sums(t + 2, gm[2:4])
        return carry

    lax.fori_loop(0, PEER_BLOCK // 4, four_tokens, 0)
    reduce_into_column(PEER_BLOCK - 2, gm[2])
    reduce_into_column(PEER_BLOCK - 1, gm[3])


def _peer_u_first_kernel(layer, cnt_ref, loc_hbm, h_ref, tab_hbm, actt_ref, tab_s, sem, loc_a, loc_b,
                         loc_sems, *gm):
    _load_table_half(0, layer, tab_hbm, tab_s, sem)
    _with_pick_offsets(loc_hbm, (loc_a, loc_b), loc_sems,
                       lambda loc: _peer_dots(0, cnt_ref, loc, h_ref, tab_s, actt_ref, gm))


def _peer_u_second_kernel(layer, cnt_ref, loc_hbm, h_ref, tab_hbm, act0t_ref, idxt_ref, gatet_ref,
                          w0_ref, w1_ref, tab_s, sem, loc_a, loc_b, loc_sems, actt_s, *gm):
    _load_table_half(1, layer, tab_hbm, tab_s, sem)
    _with_pick_offsets(loc_hbm, (loc_a, loc_b), loc_sems,
                       lambda loc: _peer_dots(1, cnt_ref, loc, h_ref, tab_s, actt_s, gm))
    in0 = _table_half(idxt_ref[...]) == 0
    w = gatet_ref[...] * _gelu(jnp.where(in0, act0t_ref[...], actt_s[...]))
    w0_ref[...] = jnp.where(in0, w, 0.0)
    w1_ref[...] = jnp.where(in0, 0.0, w)


def _peer_sum(half, cnt_ref, loc_ref, w_ref, tab_ref, init, f_ref, wb):
    tok0 = pl.program_id(0) * PEER_BLOCK

    def splat_weights(t, buf):
        buf[...] = jnp.take_along_axis(w_ref[...], jnp.full((N_PICKS, PEER_BLOCK), t, jnp.int32), axis=1)

    def add_group(t, g, accs, weight_row):
        accs = list(accs)
        for s in range(SUBLANES):
            r = _pick_offset(loc_ref, t, g * SUBLANES + s, False)
            w = jnp.broadcast_to(weight_row(g * SUBLANES + s), (SUBLANES, LANES))
            accs[s % 4] = accs[s % 4] + tab_ref[pl.ds(r, SUBLANES), :] * w
        return tuple(accs)

    def token_rows(t):
        return pl.ds(pl.multiple_of(t * SUBLANES, SUBLANES), SUBLANES)

    def weighted_sum(t, buf):
        accs = (init(token_rows(t)),) + (jnp.zeros((SUBLANES, LANES), F32),) * 3
        for g in _group_ranges(half, 0)[0]:
            accs = add_group(t, g, accs, lambda e: buf[e:e + 1, :])
        return accs

    def total(accs):
        return (accs[0] + accs[1]) + (accs[2] + accs[3])

    def extra_weighted_sums(t, bufs):
        bounds = [_group_ranges(half, cnt_ref[tok0 + t + i])[1] for i in range(2)]

        @pl.when((bounds[0][1] > bounds[0][0]) | (bounds[1][1] > bounds[1][0]))
        def _():
            for i in range(2):
                accs = (f_ref[token_rows(t + i), :],) + (jnp.zeros((SUBLANES, LANES), F32),) * 3
                accs = lax.fori_loop(
                    *bounds[i],
                    lambda g, a, i=i: add_group(t + i, g, a, lambda e: bufs[i][pl.ds(e, 1), :]), accs)
                f_ref[token_rows(t + i), :] = total(accs)

    def token_pair(t, bufs, next_t, next_bufs):
        splat_weights(jnp.minimum(next_t, PEER_BLOCK - 1), next_bufs[0])
        splat_weights(jnp.minimum(next_t + 1, PEER_BLOCK - 1), next_bufs[1])
        f_ref[token_rows(t), :] = total(weighted_sum(t, bufs[0]))
        f_ref[token_rows(t + 1), :] = total(weighted_sum(t + 1, bufs[1]))
        extra_weighted_sums(t, bufs)

    splat_weights(0, wb[0])
    splat_weights(1, wb[1])

    def four_tokens(k, carry):
        t = 4 * k
        token_pair(t, wb[0:2], t + 2, wb[2:4])
        token_pair(t + 2, wb[2:4], t + 4, wb[0:2])
        return carry

    lax.fori_loop(0, PEER_BLOCK // 4, four_tokens, 0)


def _peer_v_first_kernel(layer, cnt_ref, loc_ref, w_ref, tab_hbm, f_ref, tab_s, sem, *wb):
    _load_table_half(0, layer, tab_hbm, tab_s, sem)
    _peer_sum(0, cnt_ref, loc_ref, w_ref, tab_s, lambda rows: jnp.zeros((SUBLANES, LANES), F32), f_ref, wb)


def _peer_v_second_kernel(layer, cnt_ref, loc_ref, w_ref, tab_hbm, f0_ref, f_ref, tab_s, sem, *wb):
    _load_table_half(1, layer, tab_hbm, tab_s, sem)
    _peer_sum(1, cnt_ref, loc_ref, w_ref, tab_s, lambda rows: f0_ref[rows, :], f_ref, wb)


def _peer(h2, loc0, loc1, loc0_rows, loc1_rows, idx_t, gate_t, count0, u_tab, v_tab, layer):
    nblk = N_TOK // PEER_BLOCK
    h_tiles = h2.reshape(N_TOK * SUBLANES, LANES)
    cnt_spec = pl.BlockSpec(memory_space=pltpu.SMEM)
    smem_spec = pl.BlockSpec((PEER_BLOCK, N_PICKS), lambda i: (i, 0), memory_space=pltpu.SMEM)
    row_spec = pl.BlockSpec((PEER_BLOCK, N_PICKS), lambda i: (i, 0))
    col_spec = pl.BlockSpec((N_PICKS, PEER_BLOCK), lambda i: (0, i))
    tile_spec = pl.BlockSpec((PEER_BLOCK * SUBLANES, LANES), lambda i: (i, 0))
    hbm_spec = pl.BlockSpec(memory_space=pl.ANY)
    row_shape = jax.ShapeDtypeStruct((N_TOK, N_PICKS), F32)
    col_shape = jax.ShapeDtypeStruct((N_PICKS, N_TOK), F32)
    tile_shape = jax.ShapeDtypeStruct((N_TOK * SUBLANES, LANES), F32)
    params = pltpu.CompilerParams(vmem_limit_bytes=VMEM_LIMIT, dimension_semantics=("arbitrary",))
    table_scratch = [pltpu.VMEM((HALF_EXPERTS * SUBLANES, LANES), F32), pltpu.SemaphoreType.DMA]
    pick_scratch = [pltpu.SMEM((N_PICKS, PEER_BLOCK), jnp.int32)] * 2 + [pltpu.SemaphoreType.DMA((2,))]
    pair_scratch = [pltpu.VMEM((N_PICKS, LANES), F32)] * 4

    act0_t = pl.pallas_call(
        functools.partial(_peer_u_first_kernel, layer), grid=(nblk,),
        in_specs=[cnt_spec, hbm_spec, tile_spec, hbm_spec],
        out_specs=col_spec, out_shape=col_shape, scratch_shapes=table_scratch + pick_scratch + pair_scratch,
        compiler_params=params, name="peer_u0",
    )(count0, loc0, h_tiles, u_tab)
    w0, w1 = pl.pallas_call(
        functools.partial(_peer_u_second_kernel, layer), grid=(nblk,),
        in_specs=[cnt_spec, hbm_spec, tile_spec, hbm_spec, col_spec, col_spec, col_spec],
        out_specs=[col_spec, col_spec], out_shape=[col_shape, col_shape],
        scratch_shapes=(table_scratch + pick_scratch + [pltpu.VMEM((N_PICKS, PEER_BLOCK), F32)]
                        + pair_scratch),
        compiler_params=params, name="peer_u1",
    )(count0, loc1, h_tiles, u_tab, act0_t, idx_t, gate_t)
    f0 = pl.pallas_call(
        functools.partial(_peer_v_first_kernel, layer), grid=(nblk,),
        in_specs=[cnt_spec, smem_spec, col_spec, hbm_spec],
        out_specs=tile_spec, out_shape=tile_shape, scratch_shapes=table_scratch + pair_scratch,
        compiler_params=params, name="peer_v0",
    )(count0, loc0_rows, w0, v_tab)
    f = pl.pallas_call(
        functools.partial(_peer_v_second_kernel, layer), grid=(nblk,),
        in_specs=[cnt_spec, smem_spec, col_spec, hbm_spec, tile_spec],
        out_specs=tile_spec, out_shape=tile_shape, scratch_shapes=table_scratch + pair_scratch,
        compiler_params=params, name="peer_v1",
    )(count0, loc1_rows, w1, v_tab, f0)
    return f.reshape(N_TOK, D_MODEL)


def _close_kernel(x1_ref, f_ref, mod_ref, g_ref, b_ref, o_ref):
    g2 = mod_ref[...][:, 5 * D_MODEL:6 * D_MODEL]
    o_ref[...] = _ln(ALPHA * x1_ref[...] + g2 * f_ref[...]) * g_ref[1:2, :] + b_ref[1:2, :]


def _close(x1, f, mods, ln_g, ln_b, layer):
    row_spec = pl.BlockSpec((ROW_BLOCK, D_MODEL), lambda i: (i, 0))
    vec_spec = pl.BlockSpec((None, 2, D_MODEL), lambda i: (layer, 0, 0))
    return pl.pallas_call(
        _close_kernel,
        grid=(N_ROW_BLOCKS,),
        in_specs=[row_spec, row_spec,
                  pl.BlockSpec((None, None, 1, 6 * D_MODEL), lambda i: (layer, _mod_group(i), 0, 0)),
                  vec_spec, vec_spec],
        out_specs=row_spec,
        out_shape=jax.ShapeDtypeStruct((N_TOK, D_MODEL), F32),
        compiler_params=pltpu.CompilerParams(vmem_limit_bytes=VMEM_LIMIT),
        name="close",
    )(x1, f, mods, ln_g, ln_b)


def _rope_tables(dim, copies):
    rows = LAT_LEN // GRID_W
    row = jnp.repeat(jnp.arange(rows, dtype=F32), GRID_W)
    col = jnp.tile(jnp.arange(GRID_W, dtype=F32), rows)
    nf = dim // 4
    inv = ROPE_BASE ** (-jnp.arange(nf, dtype=F32) / nf)
    ar = row[:, None] * inv
    ac = col[:, None] * inv
    ang = jnp.concatenate([ar, ar, ac, ac], axis=-1)
    cos, sin = jnp.cos(ang), jnp.sin(ang)
    even_quarter = (jnp.arange(dim) // nf) % 2 == 0
    sin_up = jnp.where(even_quarter, -sin, 0.0)
    sin_dn = jnp.where(even_quarter, 0.0, sin)
    return tuple(jnp.tile(t, (1, copies)) for t in (cos, sin_up, sin_dn))


def kernel(x_prompt, x_sample, c, cache_win_k, cache_win_v, cache_diff_k, cache_diff_v, c_ctx, w_mod, b_mod, w_in, w_out, chunk_w, chunk_b, win_sink, diff_lam_q, diff_lam_k, diff_subln_g, pool_w, pool_scale, ln_g, ln_b, peer_wq, peer_keys, peer_u, peer_v):
    cvec = jnp.concatenate([c_ctx[None, :], c, jnp.zeros((8 - 1 - N_LAT_SEQ, D_MODEL), F32)], axis=0)
    mods = _modulation(cvec, w_mod, b_mod)[:, :1 + N_LAT_SEQ].reshape(DEPTH, 1 + N_LAT_SEQ, 1, 6 * D_MODEL)

    w_in_b = w_in.astype(BF16)
    w_out_b = w_out.astype(BF16)
    wq_b = peer_wq.astype(BF16)
    cw_b = chunk_w.astype(BF16)
    cb_full = jnp.repeat(jnp.swapaxes(chunk_b, 1, 2), HEAD_D, axis=2)
    eye = jnp.eye(4, dtype=F32)
    wbd = jnp.einsum('lgcd,gh->lgchd', pool_w, eye).reshape(DEPTH, GROUP_W, GROUP_W).astype(BF16)
    caches = (cache_win_k.reshape(N_LAT_SEQ, DEPTH, CTX_LEN, 128),
              cache_win_v.reshape(N_LAT_SEQ, DEPTH, CTX_LEN, 128),
              cache_diff_k.reshape(N_LAT_SEQ, DEPTH, CTX_LEN, GROUP_W),
              cache_diff_v.reshape(N_LAT_SEQ, DEPTH, CTX_LEN, GROUP_W))
    rope_tabs = _rope_tables(HEAD_D, GROUP_W // HEAD_D) + _rope_tables(C_SUB, GROUP_W // C_SUB)
    u_tiles = peer_u.reshape(DEPTH, N_EXPERTS * SUBLANES, LANES)
    v_tiles = peer_v.reshape(DEPTH, N_EXPERTS * SUBLANES, LANES)

    x = jnp.concatenate([x_prompt.reshape(N_CTX_TOK, D_MODEL), x_sample.reshape(N_LAT_TOK, D_MODEL)], axis=0)
    kbs, vbs, kcs, vcs = [], [], [], []
    for l in range(DEPTH):
        lam_init = 0.8 - 0.6 * math.exp(-0.3 * l)
        layer_w = (cw_b[l], cb_full[l], win_sink[l], diff_lam_q[l], diff_lam_k[l],
                   diff_subln_g[l].reshape(1, HEAD_D), wbd[l], pool_scale[l].reshape(1, GROUP_W))
        p = _in_proj(x, mods, w_in_b, l)
        pc = p[:N_CTX_TOK].reshape(N_CTX_SEQ, CTX_LEN, P_COLS)
        kbs.append(pc[..., COL_BK:COL_BV].reshape(N_CTX_SEQ, CTX_LEN, 2, HEAD_D))
        vbs.append(pc[..., COL_BV:COL_CQ].reshape(N_CTX_SEQ, CTX_LEN, 2, HEAD_D))
        kcs.append(pc[..., COL_CK:COL_CV].reshape(N_CTX_SEQ, CTX_LEN, 4, 2, C_SUB))
        vcs.append(pc[..., COL_CV:COL_DZ].reshape(N_CTX_SEQ, CTX_LEN, 4, HEAD_D))
        o = jnp.concatenate([_ctx_mix(p, layer_w, lam_init),
                             _lat_mix(p, caches, rope_tabs, layer_w, lam_init, l)], axis=0)
        x1, h2, q = _out_proj(o, x, mods, w_out_b, ln_g, ln_b, wq_b, l)
        loc0, loc1, loc0_rows, loc1_rows, idx_t, gate_t, count0 = _route(q, peer_keys[l])
        f = _peer(h2, loc0, loc1, loc0_rows, loc1_rows, idx_t, gate_t, count0.reshape(N_TOK),
                  u_tiles, v_tiles, l)
        x = _close(x1, f, mods, ln_g, ln_b, l)
    return (x[:N_CTX_TOK].reshape(N_CTX_SEQ, CTX_LEN, D_MODEL),
            x[N_CTX_TOK:].reshape(N_LAT_SEQ, LAT_LEN, D_MODEL),
            jnp.stack(kbs, axis=1), jnp.stack(vbs, axis=1), jnp.stack(kcs, axis=1), jnp.stack(vcs, axis=1))
```

```python
import functools
import math

import jax
import jax.numpy as jnp
import numpy as np
from jax import lax
from jax.experimental import pallas as pl
from jax.experimental.pallas import tpu as pltpu

F32 = jnp.float32
BF16 = jnp.bfloat16
HIGHEST = lax.Precision.HIGHEST

D_MODEL = 1024
DEPTH = 4
N_CTX_SEQ = 16
CTX_LEN = 256
N_LAT_SEQ = 2
LAT_LEN = 1024
N_CTX_TOK = N_CTX_SEQ * CTX_LEN
N_LAT_TOK = N_LAT_SEQ * LAT_LEN
N_TOK = N_CTX_TOK + N_LAT_TOK
GRID_W = 64
BLOCK = 128
GROUP_W = 256
HEAD_D = 64
C_SUB = 32
WINDOW = 128
POOL_SIZES = (2, 4, 8, 16)
P_COLS = 2048
COL_AU, COL_AV, COL_BQ, COL_BK, COL_BV, COL_CQ, COL_CK, COL_CV, COL_DZ = (
    0, 256, 512, 768, 896, 1024, 1280, 1536, 1792)
PK_HEADS = 8
N_KEYS = 128
N_EXPERTS = N_KEYS * N_KEYS
PK_TOPK = 16
N_PICKS = PK_HEADS * PK_TOPK
ROPE_BASE = 10000.0
ALPHA = (2 * DEPTH) ** 0.25
LN_EPS = 1e-5
NEG = -1e30

ROW_BLOCK = 512
N_ROW_BLOCKS = N_TOK // ROW_BLOCK
N_CTX_ROW_BLOCKS = N_CTX_TOK // ROW_BLOCK
ROW_BLOCKS_PER_LAT_SEQ = LAT_LEN // ROW_BLOCK
ROUTE_BLOCK = 128
PEER_BLOCK = 128
SUBLANES = 8
LANES = 128
HALF_EXPERTS = N_EXPERTS // 2
VMEM_LIMIT = 56 * 1024 * 1024


def _mm(a, b):
    return jnp.dot(a.astype(BF16), b.astype(BF16), preferred_element_type=F32)


def _mm_nt(a, b):
    return lax.dot_general(a.astype(BF16), b.astype(BF16), (((1,), (1,)), ((), ())),
                           preferred_element_type=F32)


def _gelu(x):
    return x * (0.5 * (1.0 + jnp.tanh(0.7978845608028654 * (x + 0.044715 * (x * x * x)))))


def _ln(x):
    mu = jnp.mean(x, axis=-1, keepdims=True)
    xc = x - mu
    var = jnp.mean(xc * xc, axis=-1, keepdims=True)
    return xc * lax.rsqrt(var + LN_EPS)


def _mod_group(i):
    return jnp.where(i < N_CTX_ROW_BLOCKS, 0, 1 + (i - N_CTX_ROW_BLOCKS) // ROW_BLOCKS_PER_LAT_SEQ)


def _mod_kernel(c_ref, w_ref, b_ref, o_ref):
    c = c_ref[...]
    s = c * jax.nn.sigmoid(c)
    w = w_ref[...].astype(BF16)
    s_hi = s.astype(BF16)
    s_lo = (s - s_hi.astype(F32)).astype(BF16)
    o_ref[...] = (jnp.dot(s_hi, w, preferred_element_type=F32)
                  + jnp.dot(s_lo, w, preferred_element_type=F32) + b_ref[...])


def _modulation(cvec, w_mod, b_mod):
    nj = 4
    cols = 6 * D_MODEL // nj
    return pl.pallas_call(
        _mod_kernel,
        grid=(DEPTH, nj),
        in_specs=[pl.BlockSpec((8, D_MODEL), lambda l, j: (0, 0)),
                  pl.BlockSpec((None, D_MODEL, cols), lambda l, j: (l, 0, j)),
                  pl.BlockSpec((None, 1, cols), lambda l, j: (l, 0, j))],
        out_specs=pl.BlockSpec((None, 8, cols), lambda l, j: (l, 0, j)),
        out_shape=jax.ShapeDtypeStruct((DEPTH, 8, 6 * D_MODEL), F32),
        compiler_params=pltpu.CompilerParams(vmem_limit_bytes=VMEM_LIMIT),
        name="modulation",
    )(cvec, w_mod, b_mod.reshape(DEPTH, 1, 6 * D_MODEL))


def _in_proj_kernel(x_ref, mod_ref, w_ref, o_ref):
    m = mod_ref[...]
    h = _ln(x_ref[...]) * (1.0 + m[:, D_MODEL:2 * D_MODEL]) + m[:, 0:D_MODEL]
    o_ref[...] = jnp.dot(h.astype(BF16), w_ref[...], preferred_element_type=F32)


def _in_proj(x, mods, w_in_bf16, layer):
    return pl.pallas_call(
        _in_proj_kernel,
        grid=(N_ROW_BLOCKS,),
        in_specs=[pl.BlockSpec((ROW_BLOCK, D_MODEL), lambda i: (i, 0)),
                  pl.BlockSpec((None, None, 1, 6 * D_MODEL), lambda i: (layer, _mod_group(i), 0, 0)),
                  pl.BlockSpec((None, D_MODEL, P_COLS), lambda i: (layer, 0, 0))],
        out_specs=pl.BlockSpec((ROW_BLOCK, P_COLS), lambda i: (i, 0)),
        out_shape=jax.ShapeDtypeStruct((N_TOK, P_COLS), F32),
        compiler_params=pltpu.CompilerParams(vmem_limit_bytes=VMEM_LIMIT),
        name="in_proj",
    )(x, mods, w_in_bf16)


def _chunk_gating(u_raw, v_raw, cw_ref, cb_full):
    u = _gelu(u_raw)
    v = _gelu(v_raw).astype(BF16)
    head = lax.broadcasted_iota(jnp.int32, (BLOCK, GROUP_W), 1) // HEAD_D
    mixed = cb_full
    for h in range(4):
        mixed = mixed + jnp.where(head == h, jnp.dot(cw_ref[h], v, preferred_element_type=F32), 0.0)
    return u * mixed


def _pool(z_rows, z_win, row0, key0, seq_len, wbd_ref, pool_scale):
    r, k = z_rows.shape[0], z_win.shape[0]
    t = row0 + lax.broadcasted_iota(jnp.int32, (r, k), 0)
    pos = key0 + lax.broadcasted_iota(jnp.int32, (r, k), 1)
    t1 = row0 + lax.broadcasted_iota(jnp.int32, (r, 1), 0)
    z_hi = z_win.astype(BF16)
    z_lo = (z_win - z_hi.astype(F32)).astype(BF16)
    group = lax.broadcasted_iota(jnp.int32, (r, GROUP_W), 1) // HEAD_D
    mean = jnp.zeros((r, GROUP_W), F32)
    for g, w in enumerate(POOL_SIZES):
        lo = jnp.maximum(t - w // 2, 0)
        hi = jnp.minimum(t + w // 2, seq_len)
        band = jnp.where((pos >= lo) & (pos < hi), 1.0, 0.0).astype(BF16)
        tot = (jnp.dot(band, z_hi, preferred_element_type=F32)
               + jnp.dot(band, z_lo, preferred_element_type=F32))
        cnt = (jnp.minimum(t1 + w // 2, seq_len) - jnp.maximum(t1 - w // 2, 0)).astype(F32)
        mean = mean + jnp.where(group == g, tot / cnt, 0.0)
    pooled = mean - z_rows
    return jnp.dot(pooled.astype(BF16), wbd_ref[...], preferred_element_type=F32) * pool_scale


def _diff_lambda(lq_ref, lk_ref, lam_init):
    lq = lq_ref[...]
    lk = lk_ref[...]
    s0 = jnp.sum(lq[0:1] * lk[0:1], axis=-1, keepdims=True)
    s1 = jnp.sum(lq[1:2] * lk[1:2], axis=-1, keepdims=True)
    return jnp.exp(s0) - jnp.exp(s1) + lam_init


def _sub_norm(o, g, lam_init):
    ms = jnp.mean(o * o, axis=-1, keepdims=True)
    return o * lax.rsqrt(ms + LN_EPS) * g * (1.0 - lam_init)


def _ctx_mix_kernel(lam_init, p_ref, cw_ref, cb_ref, sink_ref, lq_ref, lk_ref, sg_ref, wbd_ref,
                    ps_ref, o_ref):
    s_len = CTX_LEN
    for n in range(s_len // BLOCK):
        rows = slice(n * BLOCK, (n + 1) * BLOCK)
        o_ref[rows, 0:GROUP_W] = _chunk_gating(p_ref[rows, COL_AU:COL_AU + GROUP_W],
                                               p_ref[rows, COL_AV:COL_AV + GROUP_W], cw_ref, cb_ref[...])
    for kv in range(2):
        kk = p_ref[:, COL_BK + kv * HEAD_D:COL_BK + (kv + 1) * HEAD_D].astype(BF16)
        vv = p_ref[:, COL_BV + kv * HEAD_D:COL_BV + (kv + 1) * HEAD_D].astype(BF16)
        for g in range(2):
            h = kv * 2 + g
            q = p_ref[:, COL_BQ + h * HEAD_D:COL_BQ + (h + 1) * HEAD_D] * (HEAD_D ** -0.5)
            s = _mm_nt(q, kk)
            sk = sink_ref[h]
            m = jnp.maximum(jnp.max(s, axis=-1, keepdims=True), sk)
            e = jnp.exp(s - m)
            den = jnp.sum(e, axis=-1, keepdims=True) + jnp.exp(sk - m)
            o_ref[:, GROUP_W + h * HEAD_D:GROUP_W + (h + 1) * HEAD_D] = _mm(e, vv) / den
    lam = _diff_lambda(lq_ref, lk_ref, lam_init)
    scale = C_SUB ** -0.5
    for h in range(4):
        vv = p_ref[:, COL_CV + h * HEAD_D:COL_CV + (h + 1) * HEAD_D].astype(BF16)
        parts = []
        for j in range(2):
            c0 = h * HEAD_D + j * C_SUB
            s = _mm_nt(p_ref[:, COL_CQ + c0:COL_CQ + c0 + C_SUB],
                       p_ref[:, COL_CK + c0:COL_CK + c0 + C_SUB]) * scale
            e = jnp.exp(s - jnp.max(s, axis=-1, keepdims=True))
            parts.append(_mm(e, vv) / jnp.sum(e, axis=-1, keepdims=True))
        o = parts[0] - lam * parts[1]
        o_ref[:, 2 * GROUP_W + h * HEAD_D:2 * GROUP_W + (h + 1) * HEAD_D] = _sub_norm(o, sg_ref[...], lam_init)
    z = p_ref[:, COL_DZ:COL_DZ + GROUP_W]
    o_ref[:, 3 * GROUP_W:4 * GROUP_W] = _pool(z, z, 0, 0, s_len, wbd_ref, ps_ref[...])


def _whole(shape):
    nd = len(shape)
    return pl.BlockSpec(shape, lambda *_: (0,) * nd)


def _ctx_mix(p, layer_w, lam_init):
    cw, cb_full, sink, lq, lk, sg, wbd, ps = layer_w
    return pl.pallas_call(
        functools.partial(_ctx_mix_kernel, lam_init),
        grid=(N_CTX_SEQ,),
        in_specs=[pl.BlockSpec((CTX_LEN, P_COLS), lambda b: (b, 0)),
                  _whole(cw.shape), _whole(cb_full.shape),
                  pl.BlockSpec(memory_space=pltpu.SMEM),
                  _whole(lq.shape), _whole(lk.shape), _whole(sg.shape), _whole(wbd.shape),
                  _whole(ps.shape)],
        out_specs=pl.BlockSpec((CTX_LEN, D_MODEL), lambda b: (b, 0)),
        out_shape=jax.ShapeDtypeStruct((N_CTX_TOK, D_MODEL), F32),
        compiler_params=pltpu.CompilerParams(vmem_limit_bytes=VMEM_LIMIT),
        name="ctx_mix",
    )(p, cw, cb_full, sink, lq, lk, sg, wbd, ps)


def _rope(x, cos, sin_up, sin_dn, shift):
    outs = []
    for c0 in range(0, x.shape[1], LANES):
        xs = x[:, c0:c0 + LANES]
        up = pltpu.roll(xs, LANES - shift, 1)
        dn = pltpu.roll(xs, shift, 1)
        outs.append(xs * cos[:, c0:c0 + LANES] + up * sin_up[:, c0:c0 + LANES]
                    + dn * sin_dn[:, c0:c0 + LANES])
    return outs[0] if len(outs) == 1 else jnp.concatenate(outs, axis=1)


def _lat_mix_kernel(lam_init, p_ref, wk_ref, wv_ref, dk_ref, dv_ref,
                    cosb_ref, sbu_ref, sbd_ref, cosc_ref, scu_ref, scd_ref,
                    cw_ref, cb_ref, sink_ref, lq_ref, lk_ref, sg_ref, wbd_ref, ps_ref,
                    o_ref, kb_s, kc_s):
    n = pl.program_id(1)
    s_len = LAT_LEN

    @pl.when(n == 0)
    def _():
        kb_s[...] = _rope(p_ref[:, COL_BK:COL_BK + 128], cosb_ref[:, 0:128], sbu_ref[:, 0:128],
                          sbd_ref[:, 0:128], HEAD_D // 4).astype(BF16)
        kc_s[...] = _rope(p_ref[:, COL_CK:COL_CK + GROUP_W], cosc_ref[...], scu_ref[...],
                          scd_ref[...], C_SUB // 4).astype(BF16)

    r0 = pl.multiple_of(n * BLOCK, BLOCK)
    rows = pl.ds(r0, BLOCK)
    o_ref[:, 0:GROUP_W] = _chunk_gating(p_ref[rows, COL_AU:COL_AU + GROUP_W],
                                        p_ref[rows, COL_AV:COL_AV + GROUP_W], cw_ref, cb_ref[...])
    w0 = pl.multiple_of(jnp.clip((n - 1) * BLOCK, 0, s_len - 3 * BLOCK), BLOCK)
    win = pl.ds(w0, 3 * BLOCK)
    qb = _rope(p_ref[rows, COL_BQ:COL_BQ + GROUP_W], cosb_ref[rows, :], sbu_ref[rows, :],
               sbd_ref[rows, :], HEAD_D // 4) * (HEAD_D ** -0.5)
    qpos = r0 + lax.broadcasted_iota(jnp.int32, (BLOCK, 3 * BLOCK), 0)
    kpos = w0 + lax.broadcasted_iota(jnp.int32, (BLOCK, 3 * BLOCK), 1)
    valid = jnp.abs(kpos - qpos) <= WINDOW
    for kv in range(2):
        cs = slice(kv * HEAD_D, (kv + 1) * HEAD_D)
        k_loc = kb_s[win, cs]
        v_loc = p_ref[win, COL_BV + kv * HEAD_D:COL_BV + (kv + 1) * HEAD_D].astype(BF16)
        k_ctx = wk_ref[:, cs].astype(BF16)
        v_ctx = wv_ref[:, cs].astype(BF16)
        for g in range(2):
            h = kv * 2 + g
            q = qb[:, h * HEAD_D:(h + 1) * HEAD_D]
            s_loc = jnp.where(valid, _mm_nt(q, k_loc), NEG)
            s_ctx = _mm_nt(q, k_ctx)
            sk = sink_ref[h]
            m = jnp.maximum(jnp.maximum(jnp.max(s_loc, axis=-1, keepdims=True),
                                        jnp.max(s_ctx, axis=-1, keepdims=True)), sk)
            e_loc = jnp.exp(s_loc - m)
            e_ctx = jnp.exp(s_ctx - m)
            den = (jnp.sum(e_loc, axis=-1, keepdims=True) + jnp.sum(e_ctx, axis=-1, keepdims=True)
                   + jnp.exp(sk - m))
            o_ref[:, GROUP_W + h * HEAD_D:GROUP_W + (h + 1) * HEAD_D] = (
                _mm(e_loc, v_loc) + _mm(e_ctx, v_ctx)) / den
    lam = _diff_lambda(lq_ref, lk_ref, lam_init)
    scale = C_SUB ** -0.5
    qc = _rope(p_ref[rows, COL_CQ:COL_CQ + GROUP_W], cosc_ref[rows, :], scu_ref[rows, :],
               scd_ref[rows, :], C_SUB // 4)
    for h in range(4):
        v_ctx = dv_ref[:, h * HEAD_D:(h + 1) * HEAD_D].astype(BF16)
        v_lat = p_ref[:, COL_CV + h * HEAD_D:COL_CV + (h + 1) * HEAD_D].astype(BF16)
        parts = []
        for j in range(2):
            cs = slice(h * HEAD_D + j * C_SUB, h * HEAD_D + (j + 1) * C_SUB)
            q = qc[:, cs]
            s_ctx = _mm_nt(q, dk_ref[:, cs]) * scale
            s_lat = _mm_nt(q, kc_s[:, cs]) * scale
            m = jnp.maximum(jnp.max(s_ctx, axis=-1, keepdims=True), jnp.max(s_lat, axis=-1, keepdims=True))
            e_ctx = jnp.exp(s_ctx - m)
            e_lat = jnp.exp(s_lat - m)
            den = jnp.sum(e_ctx, axis=-1, keepdims=True) + jnp.sum(e_lat, axis=-1, keepdims=True)
            parts.append((_mm(e_ctx, v_ctx) + _mm(e_lat, v_lat)) / den)
        o = parts[0] - lam * parts[1]
        o_ref[:, 2 * GROUP_W + h * HEAD_D:2 * GROUP_W + (h + 1) * HEAD_D] = _sub_norm(o, sg_ref[...], lam_init)
    o_ref[:, 3 * GROUP_W:4 * GROUP_W] = _pool(p_ref[rows, COL_DZ:COL_DZ + GROUP_W],
                                              p_ref[win, COL_DZ:COL_DZ + GROUP_W],
                                              r0, w0, s_len, wbd_ref, ps_ref[...])


def _lat_mix(p, caches, rope_tabs, layer_w, lam_init, layer):
    cw, cb_full, sink, lq, lk, sg, wbd, ps = layer_w
    nb = LAT_LEN // BLOCK
    ctx_blocks = N_CTX_TOK // LAT_LEN
    cache_spec = lambda width: pl.BlockSpec((None, None, CTX_LEN, width), lambda b, n: (b, layer, 0, 0))
    rope_spec = pl.BlockSpec((LAT_LEN, GROUP_W), lambda b, n: (0, 0))
    return pl.pallas_call(
        functools.partial(_lat_mix_kernel, lam_init),
        grid=(N_LAT_SEQ, nb),
        in_specs=[pl.BlockSpec((LAT_LEN, P_COLS), lambda b, n: (ctx_blocks + b, 0)),
                  cache_spec(128), cache_spec(128), cache_spec(256), cache_spec(256)]
                 + [rope_spec] * 6
                 + [_whole(cw.shape), _whole(cb_full.shape), pl.BlockSpec(memory_space=pltpu.SMEM),
                    _whole(lq.shape), _whole(lk.shape), _whole(sg.shape), _whole(wbd.shape),
                    _whole(ps.shape)],
        out_specs=pl.BlockSpec((BLOCK, D_MODEL), lambda b, n: (b * nb + n, 0)),
        out_shape=jax.ShapeDtypeStruct((N_LAT_TOK, D_MODEL), F32),
        scratch_shapes=[pltpu.VMEM((LAT_LEN, 128), BF16), pltpu.VMEM((LAT_LEN, GROUP_W), BF16)],
        compiler_params=pltpu.CompilerParams(vmem_limit_bytes=VMEM_LIMIT,
                                             dimension_semantics=("arbitrary", "arbitrary")),
        name="lat_mix",
    )(p, *caches, *rope_tabs, cw, cb_full, sink, lq, lk, sg, wbd, ps)


def _out_proj_kernel(o_ref, x_ref, mod_ref, wo_ref, g_ref, b_ref, wq_ref, x1_ref, h2_ref, q_ref):
    m = mod_ref[...]
    g1 = m[:, 2 * D_MODEL:3 * D_MODEL]
    sh2 = m[:, 3 * D_MODEL:4 * D_MODEL]
    sc2 = m[:, 4 * D_MODEL:5 * D_MODEL]
    y = jnp.dot(o_ref[...].astype(BF16), wo_ref[...], preferred_element_type=F32)
    x1 = _ln(ALPHA * x_ref[...] + g1 * y) * g_ref[0:1, :] + b_ref[0:1, :]
    x1_ref[...] = x1
    h2 = _ln(x1) * (1.0 + sc2) + sh2
    h2_ref[...] = h2
    q_ref[...] = jnp.dot(h2.astype(BF16), wq_ref[...], preferred_element_type=F32)


def _out_proj(o, x, mods, w_out_bf16, ln_g, ln_b, wq_bf16, layer):
    row_spec = pl.BlockSpec((ROW_BLOCK, D_MODEL), lambda i: (i, 0))
    mat_spec = pl.BlockSpec((None, D_MODEL, D_MODEL), lambda i: (layer, 0, 0))
    vec_spec = pl.BlockSpec((None, 2, D_MODEL), lambda i: (layer, 0, 0))
    shape = jax.ShapeDtypeStruct((N_TOK, D_MODEL), F32)
    return pl.pallas_call(
        _out_proj_kernel,
        grid=(N_ROW_BLOCKS,),
        in_specs=[row_spec, row_spec,
                  pl.BlockSpec((None, None, 1, 6 * D_MODEL), lambda i: (layer, _mod_group(i), 0, 0)),
                  mat_spec, vec_spec, vec_spec, mat_spec],
        out_specs=[row_spec, row_spec, row_spec],
        out_shape=[shape, shape, shape],
        compiler_params=pltpu.CompilerParams(vmem_limit_bytes=VMEM_LIMIT),
        name="out_proj",
    )(o, x, mods, w_out_bf16, ln_g, ln_b, wq_bf16)


def _top16_rows(sc, order):
    out_row = lax.broadcasted_iota(jnp.int32, (PK_TOPK, sc.shape[1]), 0)
    vals = jnp.zeros((PK_TOPK, sc.shape[1]), F32)
    sel = jnp.zeros((PK_TOPK, sc.shape[1]), jnp.int32)
    for i in range(PK_TOPK):
        m = jnp.max(sc, axis=0, keepdims=True)
        am = jnp.min(jnp.where(sc == m, order, jnp.int32(2 ** 30)), axis=0, keepdims=True)
        vals = jnp.where(out_row == i, m, vals)
        sel = jnp.where(out_row == i, am, sel)
        sc = jnp.where(order == am, NEG, sc)
    return vals, sel


def _pair_candidates(sv, si):
    n_tok = sv[0].shape[1]
    rows = lax.broadcasted_iota(jnp.int32, (SUBLANES, n_tok), 0)
    cands, flats, experts = [], [], []

    def piece(a, b0, valid):
        v = sv[0][a:a + 1, :] + sv[1][b0:b0 + SUBLANES, :]
        cands.append(v if valid >= SUBLANES else jnp.where(rows < valid, v, NEG))
        flats.append(a * PK_TOPK + b0 + rows)
        experts.append(si[0][a:a + 1, :] * N_KEYS + si[1][b0:b0 + SUBLANES, :])

    piece(0, 0, SUBLANES)
    piece(0, SUBLANES, SUBLANES)
    for a in range(1, SUBLANES):
        piece(a, 0, PK_TOPK // (a + 1))
    cands.append(sv[0][SUBLANES:, :] + sv[1][0:1, :])
    flats.append((SUBLANES + rows) * PK_TOPK)
    experts.append(si[0][SUBLANES:, :] * N_KEYS + si[1][0:1, :])
    order = jnp.concatenate(flats, axis=0) * N_EXPERTS + jnp.concatenate(experts, axis=0)
    return jnp.concatenate(cands, axis=0), order


RUN_EXPERTS = N_KEYS // 2
RUNS_PER_PARITY = N_KEYS // 2


KEY_BITS = N_KEYS.bit_length() - 1


def _table_half(idx):
    return ((idx >> KEY_BITS) ^ (idx >> (KEY_BITS - 1))) & 1


def _half_offsets(idx, half):
    i1 = idx >> KEY_BITS
    local = ((i1 & 1) * RUNS_PER_PARITY + (i1 >> 1)) * RUN_EXPERTS + (idx & (RUN_EXPERTS - 1))
    return jnp.where(_table_half(idx) == half, local, 0) * SUBLANES


def _load_table_half(half, layer, tab_hbm, tab_s, sem):
    @pl.when(pl.program_id(0) == 0)
    def _():
        run_rows = RUN_EXPERTS * SUBLANES
        copies = []
        for parity in range(2):
            for pair in range(RUNS_PER_PARITY):
                first = ((2 * pair + parity) * N_KEYS + (half ^ parity) * RUN_EXPERTS) * SUBLANES
                slot = (parity * RUNS_PER_PARITY + pair) * run_rows
                copies.append(pltpu.make_async_copy(tab_hbm.at[layer, pl.ds(first, run_rows), :],
                                                    tab_s.at[pl.ds(slot, run_rows), :], sem))
        for copy in copies:
            copy.start()
        for copy in copies:
            copy.wait()


def _route_kernel(q_ref, keys_ref, loc0_ref, loc1_ref, loc0_rows_ref, loc1_rows_ref, idxt_ref, gatet_ref,
                  count0_ref, slot_s, idx_s, gate_s):
    n_tok = q_ref.shape[0]
    key_row = lax.broadcasted_iota(jnp.int32, (N_KEYS, n_tok), 0)
    for h in range(PK_HEADS):
        sv = []
        si = []
        for j in range(2):
            c0 = (h * 2 + j) * HEAD_D
            sc = lax.dot_general(keys_ref[j], q_ref[:, c0:c0 + HEAD_D], (((1,), (1,)), ((), ())),
                                 precision=HIGHEST, preferred_element_type=F32)
            v, i = _top16_rows(sc, key_row)
            sv.append(v)
            si.append(i)
        cv, winners = _top16_rows(*_pair_candidates(sv, si))
        ce = winners & (N_EXPERTS - 1)
        e = jnp.exp(cv - jnp.max(cv, axis=0, keepdims=True))
        gate_s[h * PK_TOPK:(h + 1) * PK_TOPK, :] = e / jnp.sum(e, axis=0, keepdims=True)
        idx_s[h * PK_TOPK:(h + 1) * PK_TOPK, :] = ce
    in0 = _table_half(idx_s[...]) == 0
    before = (lax.broadcasted_iota(jnp.int32, (N_PICKS, N_PICKS), 1)
              < lax.broadcasted_iota(jnp.int32, (N_PICKS, N_PICKS), 0))
    rank0 = jnp.dot(jnp.where(before, 1.0, 0.0).astype(BF16), jnp.where(in0, 1.0, 0.0).astype(BF16),
                    preferred_element_type=F32).astype(jnp.int32)
    count0 = jnp.sum(jnp.where(in0, 1, 0), axis=0, keepdims=True)
    pick = lax.broadcasted_iota(jnp.int32, (N_PICKS, n_tok), 0)
    slot_s[...] = jnp.where(in0, rank0, count0 + pick - rank0)
    idx_sorted = jnp.zeros((N_PICKS, n_tok), jnp.int32)
    gate_sorted = jnp.zeros((N_PICKS, n_tok), F32)
    for e in range(N_PICKS):
        hit = pick == slot_s[e:e + 1, :]
        idx_sorted = jnp.where(hit, idx_s[e:e + 1, :], idx_sorted)
        gate_sorted = jnp.where(hit, gate_s[e:e + 1, :], gate_sorted)
    idxt_ref[...] = idx_sorted
    gatet_ref[...] = gate_sorted
    count0_ref[...] = count0
    loc0_ref[...] = _half_offsets(idx_sorted, 0)
    loc1_ref[...] = _half_offsets(idx_sorted, 1)
    idx = idx_sorted.astype(F32).T.astype(jnp.int32)
    loc0_rows_ref[...] = _half_offsets(idx, 0)
    loc1_rows_ref[...] = _half_offsets(idx, 1)


def _route(q, keys):
    row_spec = pl.BlockSpec((ROUTE_BLOCK, N_PICKS), lambda i: (i, 0))
    col_spec = pl.BlockSpec((N_PICKS, ROUTE_BLOCK), lambda i: (0, i))
    return pl.pallas_call(
        _route_kernel,
        grid=(N_TOK // ROUTE_BLOCK,),
        in_specs=[pl.BlockSpec((ROUTE_BLOCK, D_MODEL), lambda i: (i, 0)), _whole(keys.shape)],
        out_specs=[col_spec, col_spec, row_spec, row_spec, col_spec, col_spec,
                   pl.BlockSpec((1, ROUTE_BLOCK), lambda i: (0, i))],
        out_shape=[jax.ShapeDtypeStruct((N_PICKS, N_TOK), jnp.int32),
                   jax.ShapeDtypeStruct((N_PICKS, N_TOK), jnp.int32),
                   jax.ShapeDtypeStruct((N_TOK, N_PICKS), jnp.int32),
                   jax.ShapeDtypeStruct((N_TOK, N_PICKS), jnp.int32),
                   jax.ShapeDtypeStruct((N_PICKS, N_TOK), jnp.int32),
                   jax.ShapeDtypeStruct((N_PICKS, N_TOK), F32),
                   jax.ShapeDtypeStruct((1, N_TOK), jnp.int32)],
        scratch_shapes=[pltpu.VMEM((N_PICKS, ROUTE_BLOCK), jnp.int32),
                        pltpu.VMEM((N_PICKS, ROUTE_BLOCK), jnp.int32),
                        pltpu.VMEM((N_PICKS, ROUTE_BLOCK), F32)],
        compiler_params=pltpu.CompilerParams(vmem_limit_bytes=VMEM_LIMIT),
        name="route",
    )(q, keys)


N_GROUPS = N_PICKS // SUBLANES
WALK_GROUPS = 9


def _group_ranges(half, count0):
    group_bits = SUBLANES.bit_length() - 1
    if half == 0:
        return range(WALK_GROUPS), (WALK_GROUPS, (count0 + SUBLANES - 1) >> group_bits)
    return range(N_GROUPS - WALK_GROUPS, N_GROUPS), (count0 >> group_bits, N_GROUPS - WALK_GROUPS)


_MERGE_ORDER = (0, 4, 2, 6, 1, 5, 3, 7)


def _merge8(tiles, sub):
    m4 = sub < 4
    m2 = (sub & 3) < 2
    m1 = (sub & 1) == 0

    def l1(a, b):
        return jnp.where(m4, a, b) + pltpu.roll(jnp.where(m4, b, a), 4, 0)

    def l2(c, d):
        return jnp.where(m2, c, d) + jnp.where(m2, pltpu.roll(c, 6, 0), pltpu.roll(d, 2, 0))

    def l3(e, f):
        return jnp.where(m1, e, f) + jnp.where(m1, pltpu.roll(e, 7, 0), pltpu.roll(f, 1, 0))

    c = [l1(tiles[2 * i], tiles[2 * i + 1]) for i in range(4)]
    return l3(l2(c[0], c[1]), l2(c[2], c[3]))


def _pick_offset(loc_ref, t, e, picks_major):
    if not picks_major:
        return pl.multiple_of(loc_ref.at[t][e], SUBLANES)
    if isinstance(e, int):
        return pl.multiple_of(loc_ref.at[e][t], SUBLANES)
    return pl.multiple_of(loc_ref[e, t], SUBLANES)


def _with_pick_offsets(loc_hbm, loc_bufs, loc_sems, body):
    step = pl.program_id(0)

    def fetch(block, slot):
        return pltpu.make_async_copy(loc_hbm.at[:, pl.ds(block * PEER_BLOCK, PEER_BLOCK)], loc_bufs[slot],
                                     loc_sems.at[slot])

    @pl.when(step == 0)
    def _():
        fetch(0, 0).start()

    for slot in range(2):
        @pl.when(lax.rem(step, 2) == slot)
        def _(slot=slot):
            fetch(step, slot).wait()

            @pl.when(step + 1 < pl.num_programs(0))
            def _():
                fetch(step + 1, 1 - slot).start()

            body(loc_bufs[slot])


def _peer_dots(half, cnt_ref, loc_ref, h_ref, tab_ref, actt_ref, gm):
    sub = lax.broadcasted_iota(jnp.int32, (SUBLANES, LANES), 0)
    lane = lax.broadcasted_iota(jnp.int32, (N_PICKS, PEER_BLOCK), 1)
    tok0 = pl.program_id(0) * PEER_BLOCK
    actt_ref[...] = jnp.zeros((N_PICKS, PEER_BLOCK), F32)
    for buf in gm:
        buf[...] = jnp.zeros((N_PICKS, LANES), F32)

    def group_sums(t, g):
        ht = h_ref[pl.ds(pl.multiple_of(t * SUBLANES, SUBLANES), SUBLANES), :]
        prods = []
        for s in range(SUBLANES):
            r = _pick_offset(loc_ref, t, g * SUBLANES + _MERGE_ORDER[s], True)
            prods.append(tab_ref[pl.ds(r, SUBLANES), :] * ht)
        return _merge8(prods, sub)

    def partial_sums(t, buf):
        for g in _group_ranges(half, 0)[0]:
            buf[g * SUBLANES:(g + 1) * SUBLANES, :] = group_sums(t, g)

    def extra_partial_sums(t, bufs):
        bounds = [_group_ranges(half, cnt_ref[tok0 + t + i])[1] for i in range(2)]

        @pl.when((bounds[0][1] > bounds[0][0]) | (bounds[1][1] > bounds[1][0]))
        def _():
            for i in range(2):
                def one(g, carry, i=i):
                    bufs[i][pl.ds(pl.multiple_of(g * SUBLANES, SUBLANES), SUBLANES), :] = group_sums(t + i, g)
                    return carry

                lax.fori_loop(*bounds[i], one, 0)

    def reduce_into_column(t, buf):
        col = jnp.sum(buf[...], axis=1, keepdims=True)
        pltpu.store(actt_ref, jnp.broadcast_to(col, (N_PICKS, PEER_BLOCK)), mask=lane == t)

    def four_tokens(k, carry):
        t = 4 * k
        reduce_into_column(t - 2, gm[2])
        reduce_into_column(t - 1, gm[3])
        partial_sums(t, gm[0])
        partial_sums(t + 1, gm[1])
        extra_partial_sums(t, gm[0:2])
        reduce_into_column(t, gm[0])
        reduce_into_column(t + 1, gm[1])
        partial_sums(t + 2, gm[2])
        partial_sums(t + 3, gm[3])
        extra_partial_sums(t + 2, gm[2:4])
        return carry

    lax.fori_loop(0, PEER_BLOCK // 4, four_tokens, 0)
    reduce_into_column(PEER_BLOCK - 2, gm[2])
    reduce_into_column(PEER_BLOCK - 1, gm[3])


def _peer_u_first_kernel(layer, cnt_ref, loc_hbm, h_ref, tab_hbm, actt_ref, tab_s, sem, loc_a, loc_b,
                         loc_sems, *gm):
    _load_table_half(0, layer, tab_hbm, tab_s, sem)
    _with_pick_offsets(loc_hbm, (loc_a, loc_b), loc_sems,
                       lambda loc: _peer_dots(0, cnt_ref, loc, h_ref, tab_s, actt_ref, gm))


def _peer_u_second_kernel(layer, cnt_ref, loc_hbm, h_ref, tab_hbm, act0t_ref, idxt_ref, gatet_ref,
                          w0_ref, w1_ref, tab_s, sem, loc_a, loc_b, loc_sems, actt_s, *gm):
    _load_table_half(1, layer, tab_hbm, tab_s, sem)
    _with_pick_offsets(loc_hbm, (loc_a, loc_b), loc_sems,
                       lambda loc: _peer_dots(1, cnt_ref, loc, h_ref, tab_s, actt_s, gm))
    in0 = _table_half(idxt_ref[...]) == 0
    w = gatet_ref[...] * _gelu(jnp.where(in0, act0t_ref[...], actt_s[...]))
    w0_ref[...] = jnp.where(in0, w, 0.0)
    w1_ref[...] = jnp.where(in0, 0.0, w)


def _peer_sum(half, cnt_ref, loc_ref, w_ref, tab_ref, init, f_ref, wb):
    tok0 = pl.program_id(0) * PEER_BLOCK

    def splat_weights(t, buf):
        buf[...] = jnp.take_along_axis(w_ref[...], jnp.full((N_PICKS, PEER_BLOCK), t, jnp.int32), axis=1)

    def add_group(t, g, accs, weight_row):
        accs = list(accs)
        for s in range(SUBLANES):
            r = _pick_offset(loc_ref, t, g * SUBLANES + s, False)
            w = jnp.broadcast_to(weight_row(g * SUBLANES + s), (SUBLANES, LANES))
            accs[s % 4] = accs[s % 4] + tab_ref[pl.ds(r, SUBLANES), :] * w
        return tuple(accs)

    def token_rows(t):
        return pl.ds(pl.multiple_of(t * SUBLANES, SUBLANES), SUBLANES)

    def weighted_sum(t, buf):
        accs = (init(token_rows(t)),) + (jnp.zeros((SUBLANES, LANES), F32),) * 3
        for g in _group_ranges(half, 0)[0]:
            accs = add_group(t, g, accs, lambda e: buf[e:e + 1, :])
        return accs

    def total(accs):
        return (accs[0] + accs[1]) + (accs[2] + accs[3])

    def extra_weighted_sums(t, bufs):
        bounds = [_group_ranges(half, cnt_ref[tok0 + t + i])[1] for i in range(2)]

        @pl.when((bounds[0][1] > bounds[0][0]) | (bounds[1][1] > bounds[1][0]))
        def _():
            for i in range(2):
                accs = (f_ref[token_rows(t + i), :],) + (jnp.zeros((SUBLANES, LANES), F32),) * 3
                accs = lax.fori_loop(
                    *bounds[i],
                    lambda g, a, i=i: add_group(t + i, g, a, lambda e: bufs[i][pl.ds(e, 1), :]), accs)
                f_ref[token_rows(t + i), :] = total(accs)

    def token_pair(t, bufs, next_t, next_bufs):
        splat_weights(jnp.minimum(next_t, PEER_BLOCK - 1), next_bufs[0])
        splat_weights(jnp.minimum(next_t + 1, PEER_BLOCK - 1), next_bufs[1])
        f_ref[token_rows(t), :] = total(weighted_sum(t, bufs[0]))
        f_ref[token_rows(t + 1), :] = total(weighted_sum(t + 1, bufs[1]))
        extra_weighted_sums(t, bufs)

    splat_weights(0, wb[0])
    splat_weights(1, wb[1])

    def four_tokens(k, carry):
        t = 4 * k
        token_pair(t, wb[0:2], t + 2, wb[2:4])
        token_pair(t + 2, wb[2:4], t + 4, wb[0:2])
        return carry

    lax.fori_loop(0, PEER_BLOCK // 4, four_tokens, 0)


def _peer_v_first_kernel(layer, cnt_ref, loc_ref, w_ref, tab_hbm, f_ref, tab_s, sem, *wb):
    _load_table_half(0, layer, tab_hbm, tab_s, sem)
    _peer_sum(0, cnt_ref, loc_ref, w_ref, tab_s, lambda rows: jnp.zeros((SUBLANES, LANES), F32), f_ref, wb)


def _peer_v_second_kernel(layer, cnt_ref, loc_ref, w_ref, tab_hbm, f0_ref, f_ref, tab_s, sem, *wb):
    _load_table_half(1, layer, tab_hbm, tab_s, sem)
    _peer_sum(1, cnt_ref, loc_ref, w_ref, tab_s, lambda rows: f0_ref[rows, :], f_ref, wb)


def _peer(h2, loc0, loc1, loc0_rows, loc1_rows, idx_t, gate_t, count0, u_tab, v_tab, layer):
    nblk = N_TOK // PEER_BLOCK
    h_tiles = h2.reshape(N_TOK * SUBLANES, LANES)
    cnt_spec = pl.BlockSpec(memory_space=pltpu.SMEM)
    smem_spec = pl.BlockSpec((PEER_BLOCK, N_PICKS), lambda i: (i, 0), memory_space=pltpu.SMEM)
    row_spec = pl.BlockSpec((PEER_BLOCK, N_PICKS), lambda i: (i, 0))
    col_spec = pl.BlockSpec((N_PICKS, PEER_BLOCK), lambda i: (0, i))
    tile_spec = pl.BlockSpec((PEER_BLOCK * SUBLANES, LANES), lambda i: (i, 0))
    hbm_spec = pl.BlockSpec(memory_space=pl.ANY)
    row_shape = jax.ShapeDtypeStruct((N_TOK, N_PICKS), F32)
    col_shape = jax.ShapeDtypeStruct((N_PICKS, N_TOK), F32)
    tile_shape = jax.ShapeDtypeStruct((N_TOK * SUBLANES, LANES), F32)
    params = pltpu.CompilerParams(vmem_limit_bytes=VMEM_LIMIT, dimension_semantics=("arbitrary",))
    table_scratch = [pltpu.VMEM((HALF_EXPERTS * SUBLANES, LANES), F32), pltpu.SemaphoreType.DMA]
    pick_scratch = [pltpu.SMEM((N_PICKS, PEER_BLOCK), jnp.int32)] * 2 + [pltpu.SemaphoreType.DMA((2,))]
    pair_scratch = [pltpu.VMEM((N_PICKS, LANES), F32)] * 4

    act0_t = pl.pallas_call(
        functools.partial(_peer_u_first_kernel, layer), grid=(nblk,),
        in_specs=[cnt_spec, hbm_spec, tile_spec, hbm_spec],
        out_specs=col_spec, out_shape=col_shape, scratch_shapes=table_scratch + pick_scratch + pair_scratch,
        compiler_params=params, name="peer_u0",
    )(count0, loc0, h_tiles, u_tab)
    w0, w1 = pl.pallas_call(
        functools.partial(_peer_u_second_kernel, layer), grid=(nblk,),
        in_specs=[cnt_spec, hbm_spec, tile_spec, hbm_spec, col_spec, col_spec, col_spec],
        out_specs=[col_spec, col_spec], out_shape=[col_shape, col_shape],
        scratch_shapes=(table_scratch + pick_scratch + [pltpu.VMEM((N_PICKS, PEER_BLOCK), F32)]
                        + pair_scratch),
        compiler_params=params, name="peer_u1",
    )(count0, loc1, h_tiles, u_tab, act0_t, idx_t, gate_t)
    f0 = pl.pallas_call(
        functools.partial(_peer_v_first_kernel, layer), grid=(nblk,),
        in_specs=[cnt_spec, smem_spec, col_spec, hbm_spec],
        out_specs=tile_spec, out_shape=tile_shape, scratch_shapes=table_scratch + pair_scratch,
        compiler_params=params, name="peer_v0",
    )(count0, loc0_rows, w0, v_tab)
    f = pl.pallas_call(
        functools.partial(_peer_v_second_kernel, layer), grid=(nblk,),
        in_specs=[cnt_spec, smem_spec, col_spec, hbm_spec, tile_spec],
        out_specs=tile_spec, out_shape=tile_shape, scratch_shapes=table_scratch + pair_scratch,
        compiler_params=params, name="peer_v1",
    )(count0, loc1_rows, w1, v_tab, f0)
    return f.reshape(N_TOK, D_MODEL)


def _close_kernel(x1_ref, f_ref, mod_ref, g_ref, b_ref, o_ref):
    g2 = mod_ref[...][:, 5 * D_MODEL:6 * D_MODEL]
    o_ref[...] = _ln(ALPHA * x1_ref[...] + g2 * f_ref[...]) * g_ref[1:2, :] + b_ref[1:2, :]


def _close(x1, f, mods, ln_g, ln_b, layer):
    row_spec = pl.BlockSpec((ROW_BLOCK, D_MODEL), lambda i: (i, 0))
    vec_spec = pl.BlockSpec((None, 2, D_MODEL), lambda i: (layer, 0, 0))
    return pl.pallas_call(
        _close_kernel,
        grid=(N_ROW_BLOCKS,),
        in_specs=[row_spec, row_spec,
                  pl.BlockSpec((None, None, 1, 6 * D_MODEL), lambda i: (layer, _mod_group(i), 0, 0)),
                  vec_spec, vec_spec],
        out_specs=row_spec,
        out_shape=jax.ShapeDtypeStruct((N_TOK, D_MODEL), F32),
        compiler_params=pltpu.CompilerParams(vmem_limit_bytes=VMEM_LIMIT),
        name="close",
    )(x1, f, mods, ln_g, ln_b)


def _rope_tables(dim, copies):
    rows = LAT_LEN // GRID_W
    row = jnp.repeat(jnp.arange(rows, dtype=F32), GRID_W)
    col = jnp.tile(jnp.arange(GRID_W, dtype=F32), rows)
    nf = dim // 4
    inv = ROPE_BASE ** (-jnp.arange(nf, dtype=F32) / nf)
    ar = row[:, None] * inv
    ac = col[:, None] * inv
    ang = jnp.concatenate([ar, ar, ac, ac], axis=-1)
    cos, sin = jnp.cos(ang), jnp.sin(ang)
    even_quarter = (jnp.arange(dim) // nf) % 2 == 0
    sin_up = jnp.where(even_quarter, -sin, 0.0)
    sin_dn = jnp.where(even_quarter, 0.0, sin)
    return tuple(jnp.tile(t, (1, copies)) for t in (cos, sin_up, sin_dn))


def kernel(x_prompt, x_sample, c, cache_win_k, cache_win_v, cache_diff_k, cache_diff_v, c_ctx, w_mod, b_mod, w_in, w_out, chunk_w, chunk_b, win_sink, diff_lam_q, diff_lam_k, diff_subln_g, pool_w, pool_scale, ln_g, ln_b, peer_wq, peer_keys, peer_u, peer_v):
    cvec = jnp.concatenate([c_ctx[None, :], c, jnp.zeros((8 - 1 - N_LAT_SEQ, D_MODEL), F32)], axis=0)
    mods = _modulation(cvec, w_mod, b_mod)[:, :1 + N_LAT_SEQ].reshape(DEPTH, 1 + N_LAT_SEQ, 1, 6 * D_MODEL)

    w_in_b = w_in.astype(BF16)
    w_out_b = w_out.astype(BF16)
    wq_b = peer_wq.astype(BF16)
    cw_b = chunk_w.astype(BF16)
    cb_full = jnp.repeat(jnp.swapaxes(chunk_b, 1, 2), HEAD_D, axis=2)
    eye = jnp.eye(4, dtype=F32)
    wbd = jnp.einsum('lgcd,gh->lgchd', pool_w, eye).reshape(DEPTH, GROUP_W, GROUP_W).astype(BF16)
    caches = (cache_win_k.reshape(N_LAT_SEQ, DEPTH, CTX_LEN, 128),
              cache_win_v.reshape(N_LAT_SEQ, DEPTH, CTX_LEN, 128),
              cache_diff_k.reshape(N_LAT_SEQ, DEPTH, CTX_LEN, GROUP_W),
              cache_diff_v.reshape(N_LAT_SEQ, DEPTH, CTX_LEN, GROUP_W))
    rope_tabs = _rope_tables(HEAD_D, GROUP_W // HEAD_D) + _rope_tables(C_SUB, GROUP_W // C_SUB)
    u_tiles = peer_u.reshape(DEPTH, N_EXPERTS * SUBLANES, LANES)
    v_tiles = peer_v.reshape(DEPTH, N_EXPERTS * SUBLANES, LANES)

    x = jnp.concatenate([x_prompt.reshape(N_CTX_TOK, D_MODEL), x_sample.reshape(N_LAT_TOK, D_MODEL)], axis=0)
    kbs, vbs, kcs, vcs = [], [], [], []
    for l in range(DEPTH):
        lam_init = 0.8 - 0.6 * math.exp(-0.3 * l)
        layer_w = (cw_b[l], cb_full[l], win_sink[l], diff_lam_q[l], diff_lam_k[l],
                   diff_subln_g[l].reshape(1, HEAD_D), wbd[l], pool_scale[l].reshape(1, GROUP_W))
        p = _in_proj(x, mods, w_in_b, l)
        pc = p[:N_CTX_TOK].reshape(N_CTX_SEQ, CTX_LEN, P_COLS)
        kbs.append(pc[..., COL_BK:COL_BV].reshape(N_CTX_SEQ, CTX_LEN, 2, HEAD_D))
        vbs.append(pc[..., COL_BV:COL_CQ].reshape(N_CTX_SEQ, CTX_LEN, 2, HEAD_D))
        kcs.append(pc[..., COL_CK:COL_CV].reshape(N_CTX_SEQ, CTX_LEN, 4, 2, C_SUB))
        vcs.append(pc[..., COL_CV:COL_DZ].reshape(N_CTX_SEQ, CTX_LEN, 4, HEAD_D))
        o = jnp.concatenate([_ctx_mix(p, layer_w, lam_init),
                             _lat_mix(p, caches, rope_tabs, layer_w, lam_init, l)], axis=0)
        x1, h2, q = _out_proj(o, x, mods, w_out_b, ln_g, ln_b, wq_b, l)
        loc0, loc1, loc0_rows, loc1_rows, idx_t, gate_t, count0 = _route(q, peer_keys[l])
        f = _peer(h2, loc0, loc1, loc0_rows, loc1_rows, idx_t, gate_t, count0.reshape(N_TOK),
                  u_tiles, v_tiles, l)
        x = _close(x1, f, mods, ln_g, ln_b, l)
    return (x[:N_CTX_TOK].reshape(N_CTX_SEQ, CTX_LEN, D_MODEL),
            x[N_CTX_TOK:].reshape(N_LAT_SEQ, LAT_LEN, D_MODEL),
            jnp.stack(kbs, axis=1), jnp.stack(vbs, axis=1), jnp.stack(kcs, axis=1), jnp.stack(vcs, axis=1))
```
